```python
import math
import jax, jax.numpy as jnp
from jax import lax
import numpy as np

D_MODEL = 1024
BATCH = 4
SEQ = 8192
DEPTH = 4
DEC_BATCH = 8
DEC_SEQ = 4096
PAST_LEN = 128

N_MIXERS = 3
ROPE_THETA = 500000.0
NORM_EPS = 1e-6

A_HEADS = 8
A_HEAD_DIM = D_MODEL // (2 * A_HEADS)
A_V_DIM = 2 * A_HEAD_DIM
A_QK_WIDTH = 2 * A_HEADS * A_HEAD_DIM
A_V_WIDTH = A_HEADS * A_V_DIM
A_Q_BLOCK = 128
A_LAYERS = (DEPTH + 2) // N_MIXERS

B_HEADS = 16
B_HEAD_DIM = D_MODEL // B_HEADS
B_PATTERNS = ((128, 1), (512, 4), (2048, 16))
B_GROUPS = len(B_PATTERNS)
B_LAYERS = (DEPTH + 1) // N_MIXERS

C_EXPAND = 128
C_HEADS = D_MODEL // C_EXPAND
C_KEY_DIM = C_EXPAND
C_VAL_DIM = D_MODEL // C_HEADS
C_WIDTH = C_HEADS * C_KEY_DIM
C_VWIDTH = C_HEADS * C_VAL_DIM
C_CHUNK = 64
C_LAYERS = DEPTH // N_MIXERS

D_FF = 2816
CONV_WIDTH = 3

kernel_name = 'hybrid_diffattn_dilated_hgrn2_encoder'


def rms_norm(x, g, eps=NORM_EPS):
    xf = x.astype(jnp.float32)
    y = xf * lax.rsqrt(jnp.mean(xf * xf, axis=-1, keepdims=True) + eps)
    return (y * g.astype(jnp.float32)).astype(x.dtype)


def rope_partial(x, pos):
    hd = x.shape[-1]
    rot = hd // 4
    half = rot // 2
    inv_freq = ROPE_THETA ** (-jnp.arange(half, dtype=jnp.float32) / half)
    ang = pos[:, None] * inv_freq[None, :]
    cos = jnp.cos(ang)[:, None, :]
    sin = jnp.sin(ang)[:, None, :]
    xf = x.astype(jnp.float32)
    x1, x2, rest = xf[..., :half], xf[..., half:rot], xf[..., rot:]
    out = jnp.concatenate([x1 * cos - x2 * sin, x2 * cos + x1 * sin, rest], axis=-1)
    return out.astype(x.dtype)


def diff_lambda_init(layer):
    return 0.8 - 0.6 * math.exp(-0.3 * layer)


def diff_attention(h, w_qkv, lam_vecs, subln_g, w_o, pos, lam_init):
    B, S, _ = h.shape
    q, k, v = jnp.split(h @ w_qkv, [A_QK_WIDTH, 2 * A_QK_WIDTH], axis=-1)
    q = rope_partial(q.reshape(B, S, 2 * A_HEADS, A_HEAD_DIM), pos)
    k = rope_partial(k.reshape(B, S, 2 * A_HEADS, A_HEAD_DIM), pos)
    q = q.reshape(B, S, A_HEADS, 2, A_HEAD_DIM).transpose(0, 2, 3, 1, 4) * (A_HEAD_DIM ** -0.5)
    k = k.reshape(B, S, A_HEADS, 2, A_HEAD_DIM).transpose(0, 2, 3, 1, 4)
    v = v.reshape(B, S, A_HEADS, A_V_DIM).transpose(0, 2, 1, 3)
    lf = lam_vecs.astype(jnp.float32)
    lam = jnp.exp(jnp.sum(lf[0] * lf[1])) - jnp.exp(jnp.sum(lf[2] * lf[3])) + lam_init
    n_blk = S // A_Q_BLOCK
    q_blocks = jnp.moveaxis(q.reshape(B, A_HEADS, 2, n_blk, A_Q_BLOCK, A_HEAD_DIM), 3, 0)

    def attend(qb):
        s = jnp.einsum('bhiqd,bhikd->bhiqk', qb, k).astype(jnp.float32)
        p = jax.nn.softmax(s, axis=-1)
        a = p[:, :, 0] - lam * p[:, :, 1]
        return jnp.einsum('bhqk,bhkv->bhqv', a.astype(v.dtype), v)

    o = lax.map(attend, q_blocks)
    o = jnp.moveaxis(o, 0, 2).reshape(B, A_HEADS, S, A_V_DIM)
    o = rms_norm(o, subln_g, 1e-5) * (1.0 - lam_init)
    o = o.transpose(0, 2, 1, 3).reshape(B, S, A_V_WIDTH)
    return o @ w_o


def dilated_band_attention(q, k, v, dilation, half_w):
    B, S, H, hd = q.shape
    L = S // dilation
    nb = -(-L // half_w)
    Lp = nb * half_w

    def to_residue(t):
        t = t.reshape(B, L, dilation, H, t.shape[-1]).transpose(0, 2, 3, 1, 4)
        return jnp.pad(t, ((0, 0), (0, 0), (0, 0), (0, Lp - L), (0, 0)))

    qb = to_residue(q).reshape(B, dilation, H, nb, half_w, hd)

    def band(t):
        tp = jnp.pad(t, ((0, 0), (0, 0), (0, 0), (half_w, half_w), (0, 0)))
        tp = tp.reshape(B, dilation, H, nb + 2, half_w, hd)
        return jnp.concatenate([tp[:, :, :, :-2], tp[:, :, :, 1:-1], tp[:, :, :, 2:]], axis=4)

    kb = band(to_residue(k))
    vb = band(to_residue(v))
    s = jnp.einsum('brhnqe,brhnke->brhnqk', qb, kb).astype(jnp.float32) * (hd ** -0.5)
    qi = jnp.arange(nb)[:, None] * half_w + jnp.arange(half_w)[None, :]
    ki = jnp.arange(nb)[:, None] * half_w - half_w + jnp.arange(3 * half_w)[None, :]
    valid = ((jnp.abs(qi[:, :, None] - ki[:, None, :]) <= half_w)
             & (ki[:, None, :] >= 0) & (ki[:, None, :] < L))
    s = jnp.where(valid, s, -jnp.inf)
    lse = jax.nn.logsumexp(s, axis=-1)
    p = jnp.exp(s - lse[..., None])
    o = jnp.einsum('brhnqk,brhnke->brhnqe', p.astype(v.dtype), vb)

    def from_residue(t):
        c = t.shape[-1]
        t = t.reshape(B, dilation, H, Lp, c)[:, :, :, :L]
        return t.transpose(0, 3, 1, 2, 4).reshape(B, S, H, c)

    return from_residue(o), from_residue(lse[..., None])[..., 0]


def dilated_mixture_attention(h, w_qkv, w_o, pos):
    B, S, _ = h.shape
    qkv = (h @ w_qkv).reshape(B, S, B_GROUPS, 3, B_HEADS, B_HEAD_DIM)
    outs, lses = [], []
    for g, (window, dilation) in enumerate(B_PATTERNS):
        q = rope_partial(qkv[:, :, g, 0], pos)
        k = rope_partial(qkv[:, :, g, 1], pos)
        o, lse = dilated_band_attention(q, k, qkv[:, :, g, 2], dilation, window // (2 * dilation))
        outs.append(o)
        lses.append(lse)
    wts = jax.nn.softmax(jnp.stack(lses, axis=0), axis=0)
    o = jnp.einsum('gbsh,gbshe->bshe', wts, jnp.stack(outs, axis=0).astype(jnp.float32))
    return o.reshape(B, S, B_HEADS * B_HEAD_DIM).astype(h.dtype) @ w_o


def hgrn_lower_bounds(lb_logits):
    p = jax.nn.softmax(lb_logits.astype(jnp.float32), axis=0)
    return jnp.cumsum(p, axis=0) - p[0]


def gla_chunk_scan(q, k, v, log_f):
    B, T, H, K = q.shape
    V = v.shape[-1]
    n = T // C_CHUNK

    def split(t):
        return t.reshape(B, n, C_CHUNK, H, t.shape[-1]).transpose(1, 0, 3, 2, 4)

    causal = jnp.tril(jnp.ones((C_CHUNK, C_CHUNK), dtype=bool))[:, :, None]

    def step(state, blk):
        qb, kb, vb, gb = blk
        G = jnp.cumsum(gb, axis=2)
        o_inter = jnp.einsum('bhtk,bhkv->bhtv', qb * jnp.exp(G), state)
        decay = jnp.exp(jnp.where(causal, G[:, :, :, None, :] - G[:, :, None, :, :], -jnp.inf))
        scores = jnp.einsum('bhtk,bhsk,bhtsk->bhts', qb, kb, decay)
        o_intra = jnp.einsum('bhts,bhsv->bhtv', scores, vb)
        G_end = G[:, :, -1]
        state = (jnp.exp(G_end)[..., None] * state
                 + jnp.einsum('bhsk,bhsv->bhkv', kb * jnp.exp(G_end[:, :, None] - G), vb))
        return state, o_inter + o_intra

    state0 = jnp.zeros((B, H, K, V), jnp.float32)
    _, o = lax.scan(step, state0, (split(q), split(k), split(v), split(log_f)))
    return o.transpose(1, 0, 3, 2, 4).reshape(B, T, H, V)


def hgrn2_bidirectional(h, w_in, lb, gnorm_g, w_o):
    B, S, _ = h.shape
    q, f_fw, f_bw, i, g = jnp.split(
        (h @ w_in).astype(jnp.float32),
        [C_WIDTH, 2 * C_WIDTH, 3 * C_WIDTH, 3 * C_WIDTH + C_VWIDTH], axis=-1)

    def heads(t, d):
        return t.reshape(B, S, C_HEADS, d)

    q = heads(jax.nn.silu(q), C_KEY_DIM) * (C_KEY_DIM ** -0.5)
    v = heads(i, C_VAL_DIM)
    log_lb = jnp.log(lb)
    log_1m_lb = jnp.log1p(-lb)

    def scan_dir(fz, reverse):
        log_f = heads(jnp.logaddexp(log_lb, log_1m_lb + jax.nn.log_sigmoid(fz)), C_KEY_DIM)
        k = -jnp.expm1(log_f)
        flip = (lambda t: t[:, ::-1]) if reverse else (lambda t: t)
        return flip(gla_chunk_scan(flip(q), flip(k), flip(v), flip(log_f)))

    o = scan_dir(f_fw, False) + scan_dir(f_bw, True)
    o = rms_norm(o, gnorm_g) * jax.nn.silu(heads(g, C_VAL_DIM))
    return o.reshape(B, S, C_VWIDTH).astype(h.dtype) @ w_o


def conv_glu_ffn(x, w_in, conv_w, conv_b, w_out):
    S = x.shape[1]
    u = x @ w_in
    pad = CONV_WIDTH // 2
    up = jnp.pad(u, ((0, 0), (pad, pad), (0, 0)))
    acc = conv_b
    for t in range(CONV_WIDTH):
        acc = acc + up[:, t:t + S] * conv_w[t]
    a, b = jnp.split(acc, 2, axis=-1)
    return (jax.nn.gelu(a, approximate=True) * b) @ w_out


def trunk(x, norm_g, a_w_qkv, a_lambda, a_subln_g, a_w_o, b_w_qkv, b_w_o,
          c_w_in, c_lb_logits, c_gnorm_g, c_w_o, f_w_in, f_conv_w, f_conv_b, f_w_out):
    S = x.shape[1]
    pos = jnp.arange(S, dtype=jnp.float32)
    lbs = hgrn_lower_bounds(c_lb_logits)
    for layer in range(DEPTH):
        kind = layer % N_MIXERS
        j = layer // N_MIXERS
        h = rms_norm(x, norm_g[layer, 0])
        if kind == 0:
            h = diff_attention(h, a_w_qkv[j], a_lambda[j], a_subln_g[j], a_w_o[j], pos,
                               diff_lambda_init(layer))
        elif kind == 1:
            h = dilated_mixture_attention(h, b_w_qkv[j], b_w_o[j], pos)
        else:
            h = hgrn2_bidirectional(h, c_w_in[j], lbs[layer], c_gnorm_g[j], c_w_o[j])
        x = x + rms_norm(h, norm_g[layer, 1])
        h = rms_norm(x, norm_g[layer, 2])
        h = conv_glu_ffn(h, f_w_in[layer], f_conv_w[layer], f_conv_b[layer], f_w_out[layer])
        x = x + rms_norm(h, norm_g[layer, 3])
    return x


def setup_inputs(seed: int = 0) -> dict:
    key = jax.random.key(seed)
    ks = jax.random.split(key, 17)

    def nrm(k, shape, scale):
        return jax.random.normal(k, shape, jnp.float32) * scale

    return {
        'x_prompt': nrm(ks[0], (BATCH, SEQ, D_MODEL), 1.0),
        'x_sample': nrm(ks[1], (DEC_BATCH, DEC_SEQ, D_MODEL), 1.0),
        'norm_g': 1.0 + nrm(ks[2], (DEPTH, 4, D_MODEL), 0.02),
        'a_w_qkv': nrm(ks[3], (A_LAYERS, D_MODEL, 2 * A_QK_WIDTH + A_V_WIDTH), D_MODEL ** -0.5),
        'a_lambda': nrm(ks[4], (A_LAYERS, 4, A_HEAD_DIM), 0.1),
        'a_subln_g': 1.0 + nrm(ks[5], (A_LAYERS, A_V_DIM), 0.02),
        'a_w_o': nrm(ks[6], (A_LAYERS, A_V_WIDTH, D_MODEL), A_V_WIDTH ** -0.5),
        'b_w_qkv': nrm(ks[7], (B_LAYERS, D_MODEL, B_GROUPS * 3 * B_HEADS * B_HEAD_DIM), D_MODEL ** -0.5),
        'b_w_o': nrm(ks[8], (B_LAYERS, B_HEADS * B_HEAD_DIM, D_MODEL), (B_HEADS * B_HEAD_DIM) ** -0.5),
        'c_w_in': nrm(ks[9], (C_LAYERS, D_MODEL, 3 * C_WIDTH + 2 * C_VWIDTH), D_MODEL ** -0.5),
        'c_lb_logits': nrm(ks[10], (DEPTH, C_WIDTH), 0.1),
        'c_gnorm_g': 1.0 + nrm(ks[11], (C_LAYERS, C_VAL_DIM), 0.02),
        'c_w_o': nrm(ks[12], (C_LAYERS, C_VWIDTH, D_MODEL), C_VWIDTH ** -0.5),
        'f_w_in': nrm(ks[13], (DEPTH, D_MODEL, 2 * D_FF), D_MODEL ** -0.5),
        'f_conv_w': nrm(ks[14], (DEPTH, CONV_WIDTH, 2 * D_FF), CONV_WIDTH ** -0.5),
        'f_conv_b': nrm(ks[15], (DEPTH, 2 * D_FF), 0.01),
        'f_w_out': nrm(ks[16], (DEPTH, D_FF, D_MODEL), D_FF ** -0.5),
    }


def reference(x_prompt, x_sample, norm_g, a_w_qkv, a_lambda, a_subln_g, a_w_o, b_w_qkv, b_w_o,
              c_w_in, c_lb_logits, c_gnorm_g, c_w_o, f_w_in, f_conv_w, f_conv_b, f_w_out):
    y_prompt = trunk(x_prompt, norm_g, a_w_qkv, a_lambda, a_subln_g, a_w_o, b_w_qkv, b_w_o,
                     c_w_in, c_lb_logits, c_gnorm_g, c_w_o, f_w_in, f_conv_w, f_conv_b, f_w_out)
    y_sample = trunk(x_sample, norm_g, a_w_qkv, a_lambda, a_subln_g, a_w_o, b_w_qkv, b_w_o,
                     c_w_in, c_lb_logits, c_gnorm_g, c_w_o, f_w_in, f_conv_w, f_conv_b, f_w_out)
    return (y_prompt, y_sample)
```

```python
import functools
import math

import numpy as np
import jax
import jax.numpy as jnp
from jax import lax
from jax.experimental import pallas as pl
from jax.experimental.pallas import tpu as pltpu

F32 = jnp.float32
BF16 = jnp.bfloat16

D_MODEL = 1024
DEPTH = 4
N_MIXERS = 3
ROPE_THETA = 500000.0
NORM_EPS = 1e-6

A_HEADS = 8
A_HEAD_DIM = 64
A_V_DIM = 128
A_SUBLN_EPS = 1e-5

B_HEADS = 16
B_HEAD_DIM = 64
B_PATTERNS = ((128, 1), (512, 4), (2048, 16))
B_HALF_W = 64
B_QBLK = 2 * B_HALF_W
B_KBLK = 4 * B_HALF_W

C_HEADS = 8
C_DIM = 128
C_CHUNK = 128
C_LEVELS = (64, 32, 16, 8)
C_DIAG = 8

D_FF = 2816
QKV_GROUP = 3 * D_MODEL
ROPE_ROT = 16
LANES = 128
BF16_ROWS = 16

V7X_VMEM_LIMIT = 56 * 1024 * 1024


def _cparams(n_axes):
    return pltpu.CompilerParams(dimension_semantics=("arbitrary",) * n_axes,
                                vmem_limit_bytes=V7X_VMEM_LIMIT)


def _rms(x, g, eps):
    ms = jnp.mean(x * x, axis=-1, keepdims=True)
    return x * lax.rsqrt(ms + eps) * g


def _logaddexp(a, b):
    return jnp.maximum(a, b) + jnp.log1p(jnp.exp(-jnp.abs(a - b)))


def _rope_tables(seq):
    half = ROPE_ROT // 2
    inv_freq = ROPE_THETA ** (-jnp.arange(half, dtype=F32) / half)
    ang = jnp.arange(seq, dtype=F32)[:, None] * inv_freq[None, :]
    cos, sin = jnp.cos(ang), jnp.sin(ang)
    rest = B_HEAD_DIM - ROPE_ROT
    c = jnp.concatenate([cos, cos, jnp.ones((seq, rest), F32)], axis=1)
    s1 = jnp.concatenate([-sin, jnp.zeros((seq, rest + half), F32)], axis=1)
    s2 = jnp.concatenate([jnp.zeros((seq, half), F32), sin, jnp.zeros((seq, rest), F32)], axis=1)
    rep = LANES // B_HEAD_DIM
    return tuple(jnp.tile(t, (1, rep)) for t in (c, s1, s2))


def _qkv_rope_kernel(x_ref, g_ref, w_ref, c_ref, s1_ref, s2_ref, o_ref, h_scr, *, tn):
    j = pl.program_id(1)

    @pl.when(j == 0)
    def _():
        h_scr[...] = _rms(x_ref[...], g_ref[...], NORM_EPS).astype(BF16)

    acc = jnp.dot(h_scr[...], w_ref[...], preferred_element_type=F32)
    col = (j * tn) % QKV_GROUP

    @pl.when(col < 2 * D_MODEL)
    def _():
        scale = jnp.where(col < D_MODEL, F32(B_HEAD_DIM ** -0.5), F32(1.0))
        c = c_ref[...] * scale
        s1 = s1_ref[...] * scale
        s2 = s2_ref[...] * scale
        for cb in range(tn // LANES):
            a = acc[:, cb * LANES:(cb + 1) * LANES]
            r = a * c + pltpu.roll(a, LANES - ROPE_ROT // 2, 1) * s1 + pltpu.roll(a, ROPE_ROT // 2, 1) * s2
            o_ref[:, cb * LANES:(cb + 1) * LANES] = r.astype(o_ref.dtype)

    @pl.when(col >= 2 * D_MODEL)
    def _():
        o_ref[...] = acc.astype(o_ref.dtype)


def _qkv_rope(x, g, w, tables, seq, *, tm=512, tn=512):
    m, d = x.shape
    n = w.shape[1]
    pos_blocks = seq // tm
    tab_spec = pl.BlockSpec((tm, LANES), lambda i, j: (i % pos_blocks, 0))
    return pl.pallas_call(
        functools.partial(_qkv_rope_kernel, tn=tn),
        grid=(m // tm, n // tn),
        in_specs=[pl.BlockSpec((tm, d), lambda i, j: (i, 0)),
                  pl.BlockSpec((1, d), lambda i, j: (0, 0)),
                  pl.BlockSpec((d, tn), lambda i, j: (0, j)),
                  tab_spec, tab_spec, tab_spec],
        out_specs=pl.BlockSpec((tm, tn), lambda i, j: (i, j)),
        out_shape=jax.ShapeDtypeStruct((m, n), BF16),
        scratch_shapes=[pltpu.VMEM((tm, d), BF16)],
        compiler_params=_cparams(2),
        name="qkv_rope",
    )(x, g.reshape(1, d), w, *tables)


def _hgrn_act_kernel(x_ref, g_ref, w_ref, o_ref, h_scr, *, tiles_per_part):
    j = pl.program_id(1)

    @pl.when(j == 0)
    def _():
        h_scr[...] = _rms(x_ref[...], g_ref[...], NORM_EPS).astype(BF16)

    acc = jnp.dot(h_scr[...], w_ref[...], preferred_element_type=F32)

    @pl.when(j < tiles_per_part)
    def _():
        o_ref[...] = (acc * jax.nn.sigmoid(acc) * F32(C_DIM ** -0.5)).astype(o_ref.dtype)

    @pl.when((j >= tiles_per_part) & (j < 2 * tiles_per_part))
    def _():
        o_ref[...] = acc.astype(o_ref.dtype)

    @pl.when(j >= 2 * tiles_per_part)
    def _():
        o_ref[...] = (acc * jax.nn.sigmoid(acc)).astype(o_ref.dtype)


def _hgrn_act(x, g, w, *, tm=512, tn=512):
    m, d = x.shape
    per = D_MODEL // tn
    wcol = lambda j: jnp.where(j < per, j, j + 2 * per)
    return pl.pallas_call(
        functools.partial(_hgrn_act_kernel, tiles_per_part=per),
        grid=(m // tm, 3 * per),
        in_specs=[pl.BlockSpec((tm, d), lambda i, j: (i, 0)),
                  pl.BlockSpec((1, d), lambda i, j: (0, 0)),
                  pl.BlockSpec((d, tn), lambda i, j: (0, wcol(j)))],
        out_specs=pl.BlockSpec((tm, tn), lambda i, j: (i, j)),
        out_shape=jax.ShapeDtypeStruct((m, 3 * D_MODEL), BF16),
        scratch_shapes=[pltpu.VMEM((tm, d), BF16)],
        compiler_params=_cparams(2),
        name="hgrn_act",
    )(x, g.reshape(1, d), w)


def _hgrn_logf_kernel(x_ref, g_ref, w_ref, lb_ref, o_ref, h_scr, *, layer):
    j = pl.program_id(1)

    @pl.when(j == 0)
    def _():
        h_scr[...] = _rms(x_ref[...], g_ref[...], NORM_EPS).astype(BF16)

    fz = jnp.dot(h_scr[...], w_ref[...], preferred_element_type=F32)
    logits = lb_ref[...]
    e = jnp.exp(logits - jnp.max(logits, axis=0, keepdims=True))
    p = e / jnp.sum(e, axis=0, keepdims=True)
    cum = p[0:1]
    for r in range(1, layer + 1):
        cum = cum + p[r:r + 1]
    lb = cum - p[0:1]
    log_lb = jnp.log(lb)
    log_1m_lb = jnp.log1p(-lb)
    log_sig = jnp.minimum(fz, 0.0) - jnp.log1p(jnp.exp(-jnp.abs(fz)))
    o_ref[...] = _logaddexp(log_lb, log_1m_lb + log_sig)


def _hgrn_logf(x, g, w, lb_logits, layer, *, tm=512, tn=512):
    m, d = x.shape
    per = D_MODEL // tn
    return pl.pallas_call(
        functools.partial(_hgrn_logf_kernel, layer=layer),
        grid=(m // tm, 2 * per),
        in_specs=[pl.BlockSpec((tm, d), lambda i, j: (i, 0)),
                  pl.BlockSpec((1, d), lambda i, j: (0, 0)),
                  pl.BlockSpec((d, tn), lambda i, j: (0, j + per)),
                  pl.BlockSpec((DEPTH, tn), lambda i, j: (0, j % per))],
        out_specs=pl.BlockSpec((tm, tn), lambda i, j: (i, j)),
        out_shape=jax.ShapeDtypeStruct((m, 2 * D_MODEL), F32),
        scratch_shapes=[pltpu.VMEM((tm, d), BF16)],
        compiler_params=_cparams(2),
        name="hgrn_logf",
    )(x, g.reshape(1, d), w, lb_logits)


def _mm_norm_res_kernel(a_ref, w_ref, g_ref, x_ref, o_ref):
    h = jnp.dot(a_ref[...], w_ref[...], preferred_element_type=F32)
    o_ref[...] = x_ref[...] + _rms(h, g_ref[...], NORM_EPS)


def _mm_norm_res(a, w, g, x, *, tm=512):
    m, k = a.shape
    d = w.shape[1]
    return pl.pallas_call(
        _mm_norm_res_kernel,
        grid=(m // tm,),
        in_specs=[pl.BlockSpec((tm, k), lambda i: (i, 0)),
                  pl.BlockSpec((k, d), lambda i: (0, 0)),
                  pl.BlockSpec((1, d), lambda i: (0, 0)),
                  pl.BlockSpec((tm, d), lambda i: (i, 0))],
        out_specs=pl.BlockSpec((tm, d), lambda i: (i, 0)),
        out_shape=jax.ShapeDtypeStruct((m, d), F32),
        compiler_params=_cparams(1),
        name="mm_norm_res",
    )(a, w, g.reshape(1, d), x)


def _ffn_in_kernel(x_ref, xp_ref, xn_ref, g_ref, wa_ref, wb_ref, cwa_ref, cwb_ref, cba_ref, cbb_ref,
                   o_ref, h_scr, *, tm, seq):
    i = pl.program_id(0)
    j = pl.program_id(1)
    halo = BF16_ROWS
    rows = tm + 2 * halo

    @pl.when(j == 0)
    def _():
        g = g_ref[...]
        first = (i * tm) % seq == 0
        last = ((i + 1) * tm) % seq == 0
        hp = _rms(xp_ref[...], g, NORM_EPS)
        hn = _rms(xn_ref[...], g, NORM_EPS)
        h_scr[0:halo] = jnp.where(first, 0.0, hp).astype(BF16)
        h_scr[halo:halo + tm] = _rms(x_ref[...], g, NORM_EPS).astype(BF16)
        h_scr[halo + tm:rows] = jnp.where(last, 0.0, hn).astype(BF16)

    h = h_scr[...]

    def conv(w_ref, cw_ref, cb_ref):
        u = jnp.dot(h, w_ref[...], preferred_element_type=F32)
        cw = cw_ref[...]
        u_prev = pltpu.roll(u, 1, 0)
        u_next = pltpu.roll(u, rows - 1, 0)
        acc = cb_ref[...] + u_prev * cw[0:1] + u * cw[1:2] + u_next * cw[2:3]
        return acc[halo:halo + tm]

    a = conv(wa_ref, cwa_ref, cba_ref)
    b = conv(wb_ref, cwb_ref, cbb_ref)
    o_ref[...] = (jax.nn.gelu(a, approximate=True) * b).astype(o_ref.dtype)


def _ffn_in(x, g, w, conv_w, conv_b, seq, *, tm=512, tn=1408):
    m, d = x.shape
    nb = D_FF // tn
    halo = BF16_ROWS
    hb = tm // halo
    last_blk = m // halo - 1
    return pl.pallas_call(
        functools.partial(_ffn_in_kernel, tm=tm, seq=seq),
        grid=(m // tm, nb),
        in_specs=[pl.BlockSpec((tm, d), lambda i, j: (i, 0)),
                  pl.BlockSpec((halo, d), lambda i, j: (jnp.maximum(i * hb - 1, 0), 0)),
                  pl.BlockSpec((halo, d), lambda i, j: (jnp.minimum((i + 1) * hb, last_blk), 0)),
                  pl.BlockSpec((1, d), lambda i, j: (0, 0)),
                  pl.BlockSpec((d, tn), lambda i, j: (0, j)),
                  pl.BlockSpec((d, tn), lambda i, j: (0, j + nb)),
                  pl.BlockSpec((3, tn), lambda i, j: (0, j)),
                  pl.BlockSpec((3, tn), lambda i, j: (0, j + nb)),
                  pl.BlockSpec((1, tn), lambda i, j: (0, j)),
                  pl.BlockSpec((1, tn), lambda i, j: (0, j + nb))],
        out_specs=pl.BlockSpec((tm, tn), lambda i, j: (i, j)),
        out_shape=jax.ShapeDtypeStruct((m, D_FF), BF16),
        scratch_shapes=[pltpu.VMEM((tm + 2 * halo, d), BF16)],
        compiler_params=_cparams(2),
        name="ffn_in",
    )(x, x, x, g.reshape(1, d), w, w, conv_w, conv_w, conv_b.reshape(1, -1), conv_b.reshape(1, -1))


def _diff_attn_kernel(q_ref, k_ref, v_ref, lam_ref, g_ref, o_ref, q2_scr, m_scr, l_scr, acc_scr,
                      *, tq, lam_init):
    kv = pl.program_id(3)

    @pl.when(kv == 0)
    def _():
        q = q_ref[...]
        lane = lax.broadcasted_iota(jnp.int32, q.shape, 1)
        zero = jnp.zeros_like(q)
        q2_scr[0:tq] = jnp.where(lane < A_HEAD_DIM, q, zero)
        q2_scr[tq:2 * tq] = jnp.where(lane >= A_HEAD_DIM, q, zero)
        m_scr[...] = jnp.full(m_scr.shape, -jnp.inf, F32)
        l_scr[...] = jnp.zeros(l_scr.shape, F32)
        acc_scr[...] = jnp.zeros(acc_scr.shape, F32)

    s = lax.dot_general(q2_scr[...], k_ref[...], (((1,), (1,)), ((), ())), preferred_element_type=F32)
    m_prev = m_scr[...]
    m_new = jnp.maximum(m_prev, jnp.max(s, axis=1, keepdims=True))
    alpha = jnp.exp(m_prev - m_new)
    p = jnp.exp(s - m_new[:, 0:1])
    l_scr[...] = alpha * l_scr[...] + jnp.sum(p, axis=1, keepdims=True)
    acc_scr[...] = alpha * acc_scr[...] + jnp.dot(p.astype(BF16), v_ref[...], preferred_element_type=F32)
    m_scr[...] = m_new

    @pl.when(kv == pl.num_programs(3) - 1)
    def _():
        o = acc_scr[...] / l_scr[...]
        lv = lam_ref[...]
        lam = (jnp.exp(jnp.sum(lv[0:1] * lv[1:2], axis=1, keepdims=True))
               - jnp.exp(jnp.sum(lv[2:3] * lv[3:4], axis=1, keepdims=True)) + F32(lam_init))
        od = o[0:tq] - lam * o[tq:2 * tq]
        y = _rms(od, g_ref[...], A_SUBLN_EPS) * F32(1.0 - lam_init)
        o_ref[...] = y.astype(o_ref.dtype)


def _diff_attn(qkv, lam_vecs, subln_g, lam_init, *, tq=512, tk=1024):
    b, s, _ = qkv.shape
    tq, tk = min(tq, s), min(tk, s)
    blk = lambda rows, off: pl.BlockSpec((None, rows, LANES), off)
    return pl.pallas_call(
        functools.partial(_diff_attn_kernel, tq=tq, lam_init=lam_init),
        grid=(b, A_HEADS, s // tq, s // tk),
        in_specs=[blk(tq, lambda bi, h, i, j: (bi, i, h)),
                  blk(tk, lambda bi, h, i, j: (bi, j, A_HEADS + h)),
                  blk(tk, lambda bi, h, i, j: (bi, j, 2 * A_HEADS + h)),
                  pl.BlockSpec((4, A_HEAD_DIM), lambda bi, h, i, j: (0, 0)),
                  pl.BlockSpec((1, A_V_DIM), lambda bi, h, i, j: (0, 0))],
        out_specs=blk(tq, lambda bi, h, i, j: (bi, i, h)),
        out_shape=jax.ShapeDtypeStruct((b, s, A_HEADS * A_V_DIM), BF16),
        scratch_shapes=[pltpu.VMEM((2 * tq, LANES), BF16),
                        pltpu.VMEM((2 * tq, LANES), F32),
                        pltpu.VMEM((2 * tq, LANES), F32),
                        pltpu.VMEM((2 * tq, A_V_DIM), F32)],
        compiler_params=_cparams(4),
        name="diff_attn",
    )(qkv, qkv, qkv, lam_vecs, subln_g.reshape(1, A_V_DIM))


def _dil_attn_kernel(*refs, tl, length, has_prev, emit_lse):
    q_ref, k_ref, kp_ref, kn_ref, v_ref, vp_ref, vn_ref = refs[:7]
    refs = refs[7:]
    if has_prev:
        oprev_ref, lprev_ref = refs[:2]
        refs = refs[2:]
    o_ref = refs[0]
    refs = refs[1:]
    if emit_lse:
        lse_ref = refs[0]
        refs = refs[1:]
    kbuf, vbuf = refs
    hw = B_HALF_W
    t0 = pl.program_id(2) * tl

    for buf, main, prev, nxt in ((kbuf, k_ref, kp_ref, kn_ref), (vbuf, v_ref, vp_ref, vn_ref)):
        buf[0:hw] = prev[...]
        buf[hw:hw + tl] = main[...]
        buf[hw + tl:2 * hw + tl] = nxt[...]

    a_io = lax.broadcasted_iota(jnp.int32, (B_QBLK, B_KBLK), 0)
    c_io = lax.broadcasted_iota(jnp.int32, (B_QBLK, B_KBLK), 1)
    band = (c_io >= a_io) & (c_io <= a_io + 2 * hw)
    lane = lax.broadcasted_iota(jnp.int32, (B_QBLK, LANES), 1)

    def q_block(jb, carry):
        r0 = pl.multiple_of(jb * B_QBLK, B_QBLK)
        ki = t0 + r0 - hw + c_io
        valid = band & (ki >= 0) & (ki < length)
        lse_tile = jnp.zeros((B_QBLK, LANES), F32)
        for h in range(B_HEADS):
            cols = slice(h * B_HEAD_DIM, (h + 1) * B_HEAD_DIM)
            qh = q_ref[pl.ds(r0, B_QBLK), cols]
            kh = kbuf[pl.ds(r0, B_KBLK), cols]
            vh = vbuf[pl.ds(r0, B_KBLK), cols]
            s = lax.dot_general(qh, kh, (((1,), (1,)), ((), ())), preferred_element_type=F32)
            s = jnp.where(valid, s, -jnp.inf)
            m = jnp.max(s, axis=1, keepdims=True)
            p = jnp.exp(s - m)
            l = jnp.sum(p, axis=1, keepdims=True)
            o = jnp.dot(p.astype(BF16), vh, preferred_element_type=F32) / l
            lse = m + jnp.log(l)
            if has_prev:
                lse_old = lprev_ref[pl.ds(r0, B_QBLK), h:h + 1]
                o_old = oprev_ref[pl.ds(r0, B_QBLK), cols]
                lse_new = _logaddexp(lse_old, lse)
                o = o_old * jnp.exp(lse_old - lse_new) + o * jnp.exp(lse - lse_new)
                lse = lse_new
            o_ref[pl.ds(r0, B_QBLK), cols] = o.astype(o_ref.dtype)
            if emit_lse:
                lse_tile = jnp.where(lane == h, lse, lse_tile)
        if emit_lse:
            lse_ref[pl.ds(r0, B_QBLK), :] = lse_tile
        return carry

    lax.fori_loop(0, tl // B_QBLK, q_block, 0)


def _dil_attn_group(qkv, group, dilation, prev, emit_lse, *, tl=512):
    b, s, n = qkv.shape
    length = s // dilation
    tl = min(tl, length)
    hw = B_HALF_W
    width = B_HEADS * B_HEAD_DIM
    cols_per_tok = n // width
    qkv_v = qkv.reshape(b, length, dilation * n)
    per_tile = tl // hw
    n_hw = length // hw

    def col(part):
        return lambda bi, r, i: (bi, i, r * cols_per_tok + group * 3 + part)

    def halo(part, side):
        if side < 0:
            return lambda bi, r, i: (bi, jnp.maximum(i * per_tile - 1, 0), r * cols_per_tok + group * 3 + part)
        return lambda bi, r, i: (bi, jnp.minimum((i + 1) * per_tile, n_hw - 1), r * cols_per_tok + group * 3 + part)

    main = lambda part: pl.BlockSpec((None, tl, width), col(part))
    edge = lambda part, side: pl.BlockSpec((None, hw, width), halo(part, side))
    out_idx = lambda bi, r, i: (bi, i, r)
    in_specs = [main(0), main(1), edge(1, -1), edge(1, 1), main(2), edge(2, -1), edge(2, 1)]
    args = [qkv_v] * 7
    if prev is not None:
        in_specs += [pl.BlockSpec((None, tl, width), out_idx), pl.BlockSpec((None, tl, LANES), out_idx)]
        args += [prev[0].reshape(b, length, dilation * width), prev[1].reshape(b, length, dilation * LANES)]
    out_specs = [pl.BlockSpec((None, tl, width), out_idx)]
    out_shape = [jax.ShapeDtypeStruct((b, length, dilation * width), F32 if emit_lse else BF16)]
    if emit_lse:
        out_specs.append(pl.BlockSpec((None, tl, LANES), out_idx))
        out_shape.append(jax.ShapeDtypeStruct((b, length, dilation * LANES), F32))
    outs = pl.pallas_call(
        functools.partial(_dil_attn_kernel, tl=tl, length=length, has_prev=prev is not None, emit_lse=emit_lse),
        grid=(b, dilation, length // tl),
        in_specs=in_specs,
        out_specs=out_specs,
        out_shape=out_shape,
        scratch_shapes=[pltpu.VMEM((tl + 2 * hw, width), BF16), pltpu.VMEM((tl + 2 * hw, width), BF16)],
        compiler_params=_cparams(3),
        name=f"dil_attn_g{group}",
    )(*args)
    o = outs[0].reshape(b, s, width)
    if emit_lse:
        return o, outs[1].reshape(b, s, LANES)
    return o


def _dil_attn(qkv):
    acc = None
    last = len(B_PATTERNS) - 1
    for group, (_, dilation) in enumerate(B_PATTERNS):
        acc = _dil_attn_group(qkv, group, dilation, acc, emit_lse=group != last)
    return acc


def _scan_constants(reverse):
    c = C_CHUNK
    t = np.arange(c)[:, None]
    u = np.arange(c)[None, :]
    if not reverse:
        mats = [u <= t, u > t]
    else:
        mats = [u >= t, u < t]
    masks = []
    for hs in C_LEVELS:
        base = (t // (2 * hs)) * (2 * hs)
        mid = base + hs - 1
        same = (u // (2 * hs)) == (t // (2 * hs))
        if not reverse:
            m = np.where(t > mid, (u > mid) & (u <= t), (u > t) & (u <= mid))
            masks.append(same & (t > mid) & (u <= mid))
        else:
            m = np.where(t <= mid, (u >= t) & (u <= mid), (u > mid) & (u < t))
            masks.append(same & (t <= mid) & (u > mid))
        mats.append(m & same)
    stack = np.concatenate([np.asarray(m, np.float32) for m in mats], axis=0)
    return jnp.asarray(stack, BF16), jnp.asarray(np.stack(masks).astype(np.float32))


def _hgrn_scan_kernel(*refs, tt, reverse, final):
    q_ref, v_ref, lf_ref, mstack_ref, masks_ref = refs[:5]
    refs = refs[5:]
    if final:
        ofw_ref, gate_ref, gn_ref = refs[:3]
        refs = refs[3:]
    o_ref, st_scr = refs
    c = C_CHUNK
    n_chunks = tt // c
    nt_contract = (((1,), (1,)), ((), ()))

    @pl.when(pl.program_id(2) == 0)
    def _():
        st_scr[...] = jnp.zeros(st_scr.shape, F32)

    row8 = lax.broadcasted_iota(jnp.int32, (C_DIAG, C_DIM), 0)
    lane_c = lax.broadcasted_iota(jnp.int32, (C_DIAG, c), 1)

    def chunk(ci, carry):
        cidx = (n_chunks - 1 - ci) if reverse else ci
        r0 = pl.multiple_of(cidx * c, c)
        lf = lf_ref[pl.ds(r0, c), :]
        hi = lf.astype(BF16)
        r1 = lf - hi.astype(F32)
        mid = r1.astype(BF16)
        lo = (r1 - mid.astype(F32)).astype(BF16)
        x3 = jnp.concatenate([hi, mid, lo], axis=1)
        d3 = jnp.dot(mstack_ref[...], x3, preferred_element_type=F32)
        d = d3[:, 0:C_DIM] + d3[:, C_DIM:2 * C_DIM] + d3[:, 2 * C_DIM:3 * C_DIM]
        gq = d[0:c]
        gk = d[c:2 * c]

        q = q_ref[pl.ds(r0, c), :].astype(F32)
        v = v_ref[pl.ds(r0, c), :]
        kk = 1.0 - jnp.exp(lf)
        st = st_scr[...]

        o = lax.dot_general((q * jnp.exp(gq)).astype(BF16), st.astype(BF16), nt_contract,
                            preferred_element_type=F32)

        a = jnp.zeros((c, c), F32)
        for li in range(len(C_LEVELS)):
            e = jnp.exp(-jnp.abs(d[(2 + li) * c:(3 + li) * c]))
            pq = (q * e).astype(BF16)
            pk = (kk * e).astype(BF16)
            a = a + lax.dot_general(pq, pk, nt_contract, preferred_element_type=F32) * masks_ref[li]

        blocks = []
        for bi in range(c // C_DIAG):
            rows = slice(bi * C_DIAG, (bi + 1) * C_DIAG)
            gb, qb, kb = gq[rows], q[rows], kk[rows]
            ablk = jnp.zeros((C_DIAG, c), F32)
            for si in range(C_DIAG):
                keep = (row8 <= si) if reverse else (row8 >= si)
                dec = jnp.exp(jnp.where(keep, gb - gb[si:si + 1], -jnp.inf))
                sc = jnp.sum(qb * dec * kb[si:si + 1], axis=1, keepdims=True)
                ablk = jnp.where(lane_c == bi * C_DIAG + si, sc, ablk)
            blocks.append(ablk)
        a = a + jnp.concatenate(blocks, axis=0)

        o = o + jnp.dot(a.astype(BF16), v, preferred_element_type=F32)

        kd = (kk * jnp.exp(gk)).astype(BF16)
        upd = lax.dot_general(v, kd, (((0,), (0,)), ((), ())), preferred_element_type=F32)
        g_tot = gq[0:1] if reverse else gq[c - 1:c]
        st_scr[...] = st * jnp.exp(g_tot) + upd

        if final:
            tot = ofw_ref[pl.ds(r0, c), :] + o
            y = _rms(tot, gn_ref[...], NORM_EPS) * gate_ref[pl.ds(r0, c), :].astype(F32)
            o_ref[pl.ds(r0, c), :] = y.astype(o_ref.dtype)
        else:
            o_ref[pl.ds(r0, c), :] = o
        return carry

    lax.fori_loop(0, n_chunks, chunk, 0)


def _hgrn_scan(act, logf, reverse, o_fw=None, gnorm_g=None, *, tt=512):
    b, s, _ = act.shape
    tt = min(tt, s)
    nt = s // tt
    final = o_fw is not None
    mstack, masks = _scan_constants(reverse)
    tmap = (lambda t: nt - 1 - t) if reverse else (lambda t: t)
    blk = lambda off: pl.BlockSpec((None, tt, C_DIM), lambda bi, h, t: (bi, tmap(t), off + h))
    in_specs = [blk(0), blk(C_HEADS), blk(C_HEADS if reverse else 0),
                pl.BlockSpec(mstack.shape, lambda bi, h, t: (0, 0)),
                pl.BlockSpec(masks.shape, lambda bi, h, t: (0, 0, 0))]
    args = [act, act, logf, mstack, masks]
    if final:
        in_specs += [blk(0), blk(2 * C_HEADS), pl.BlockSpec((1, C_DIM), lambda bi, h, t: (0, 0))]
        args += [o_fw, act, gnorm_g.reshape(1, C_DIM)]
    return pl.pallas_call(
        functools.partial(_hgrn_scan_kernel, tt=tt, reverse=reverse, final=final),
        grid=(b, C_HEADS, nt),
        in_specs=in_specs,
        out_specs=blk(0),
        out_shape=jax.ShapeDtypeStruct((b, s, C_HEADS * C_DIM), BF16 if final else F32),
        scratch_shapes=[pltpu.VMEM((C_DIM, C_DIM), F32)],
        compiler_params=_cparams(3),
        name="hgrn_scan_bw" if reverse else "hgrn_scan_fw",
    )(*args)


def _diff_lambda_init(layer):
    return 0.8 - 0.6 * math.exp(-0.3 * layer)


def _trunk(x3, p):
    b, s, d = x3.shape
    x = x3.reshape(b * s, d)
    tables = _rope_tables(s)
    for layer in range(DEPTH):
        kind = layer % N_MIXERS
        j = layer // N_MIXERS
        g = p["norm_g"][layer]
        if kind == 0:
            qkv = _qkv_rope(x, g[0], p["a_w_qkv"][j], tables, s)
            o = _diff_attn(qkv.reshape(b, s, -1), p["a_lambda"][j], p["a_subln_g"][j], _diff_lambda_init(layer))
            w_o = p["a_w_o"][j]
        elif kind == 1:
            qkv = _qkv_rope(x, g[0], p["b_w_qkv"][j], tables, s)
            o = _dil_attn(qkv.reshape(b, s, -1))
            w_o = p["b_w_o"][j]
        else:
            act = _hgrn_act(x, g[0], p["c_w_in"][j]).reshape(b, s, -1)
            logf = _hgrn_logf(x, g[0], p["c_w_in"][j], p["c_lb_logits"], layer).reshape(b, s, -1)
            o_fw = _hgrn_scan(act, logf, False)
            o = _hgrn_scan(act, logf, True, o_fw, p["c_gnorm_g"][j])
            w_o = p["c_w_o"][j]
        x = _mm_norm_res(o.reshape(b * s, -1), w_o, g[1], x)
        u = _ffn_in(x, g[2], p["f_w_in"][layer], p["f_conv_w"][layer], p["f_conv_b"][layer], s)
        x = _mm_norm_res(u, p["f_w_out"][layer], g[3], x)
    return x.reshape(b, s, d)


def kernel(x_prompt, x_sample, norm_g, a_w_qkv, a_lambda, a_subln_g, a_w_o, b_w_qkv, b_w_o, c_w_in,
           c_lb_logits, c_gnorm_g, c_w_o, f_w_in, f_conv_w, f_conv_b, f_w_out):
    bf = lambda w: w.astype(BF16)
    p = dict(norm_g=norm_g, a_w_qkv=bf(a_w_qkv), a_lambda=a_lambda, a_subln_g=a_subln_g, a_w_o=bf(a_w_o),
             b_w_qkv=bf(b_w_qkv), b_w_o=bf(b_w_o), c_w_in=bf(c_w_in), c_lb_logits=c_lb_logits,
             c_gnorm_g=c_gnorm_g, c_w_o=bf(c_w_o), f_w_in=bf(f_w_in), f_conv_w=f_conv_w,
             f_conv_b=f_conv_b, f_w_out=bf(f_w_out))
    return _trunk(x_prompt, p), _trunk(x_sample, p)
```

```python
import functools
import math

import numpy as np
import jax
import jax.numpy as jnp
from jax import lax
from jax.experimental import pallas as pl
from jax.experimental.pallas import tpu as pltpu

F32 = jnp.float32
BF16 = jnp.bfloat16

D_MODEL = 1024
DEPTH = 4
N_MIXERS = 3
ROPE_THETA = 500000.0
NORM_EPS = 1e-6

A_HEADS = 8
A_HEAD_DIM = 64
A_V_DIM = 128
A_SUBLN_EPS = 1e-5

B_HEADS = 16
B_HEAD_DIM = 64
B_PATTERNS = ((128, 1), (512, 4), (2048, 16))
B_HALF_W = 64
B_QBLK = 2 * B_HALF_W
B_KBLK = 4 * B_HALF_W

C_HEADS = 8
C_DIM = 128
C_CHUNK = 128
C_LEVELS = (64, 32, 16, 8)
C_DIAG = 8

D_FF = 2816
QKV_GROUP = 3 * D_MODEL
ROPE_ROT = 16
LANES = 128
LOG2E = math.log2(math.e)
BF16_ROWS = 16

V7X_VMEM_LIMIT = 56 * 1024 * 1024


def _cparams(n_axes):
    return pltpu.CompilerParams(dimension_semantics=("arbitrary",) * n_axes,
                                vmem_limit_bytes=V7X_VMEM_LIMIT)


def _rms(x, g, eps):
    ms = jnp.mean(x * x, axis=-1, keepdims=True)
    return x * lax.rsqrt(ms + eps) * g


def _logaddexp(a, b):
    return jnp.maximum(a, b) + jnp.log1p(jnp.exp(-jnp.abs(a - b)))


def _rope_tables(seq):
    half = ROPE_ROT // 2
    inv_freq = ROPE_THETA ** (-jnp.arange(half, dtype=F32) / half)
    ang = jnp.arange(seq, dtype=F32)[:, None] * inv_freq[None, :]
    cos, sin = jnp.cos(ang), jnp.sin(ang)
    rest = B_HEAD_DIM - ROPE_ROT
    c = jnp.concatenate([cos, cos, jnp.ones((seq, rest), F32)], axis=1)
    s1 = jnp.concatenate([-sin, jnp.zeros((seq, rest + half), F32)], axis=1)
    s2 = jnp.concatenate([jnp.zeros((seq, half), F32), sin, jnp.zeros((seq, rest), F32)], axis=1)
    rep = LANES // B_HEAD_DIM
    return tuple(jnp.tile(t, (1, rep)) for t in (c, s1, s2))


def _residue_perm(tm, dilation):
    rows = tm // dilation
    rho = np.arange(tm)
    tok = (rho % rows) * dilation + rho // rows
    perm = np.zeros((tm, tm), np.float32)
    perm[rho, tok] = 1.0
    return perm


def _qkv_rope_kernel(*refs, tn, q_scale, dilation):
    if dilation > 1:
        x_ref, g_ref, w_ref, c_ref, s1_ref, s2_ref, perm_ref, o_ref, h_scr = refs
    else:
        x_ref, g_ref, w_ref, c_ref, s1_ref, s2_ref, o_ref, h_scr = refs
    j = pl.program_id(2)
    tm = x_ref.shape[0]
    rows = tm // dilation

    @pl.when(j == 0)
    def _():
        h = _rms(x_ref[...], g_ref[...], NORM_EPS).astype(BF16)
        if dilation > 1:
            h = jnp.dot(perm_ref[...], h, preferred_element_type=F32).astype(BF16)
        h_scr[...] = h

    acc = jnp.dot(h_scr[...], w_ref[...], preferred_element_type=F32)
    col = j * tn

    def put(cb, val):
        for rr in range(dilation):
            o_ref[rr, :, cb * LANES:(cb + 1) * LANES] = val[rr * rows:(rr + 1) * rows].astype(o_ref.dtype)

    @pl.when(col < 2 * D_MODEL)
    def _():
        scale = jnp.where(col < D_MODEL, F32(q_scale), F32(1.0))
        c = c_ref[...] * scale
        s1 = s1_ref[...] * scale
        s2 = s2_ref[...] * scale
        for cb in range(tn // LANES):
            a = acc[:, cb * LANES:(cb + 1) * LANES]
            put(cb, a * c + pltpu.roll(a, LANES - ROPE_ROT // 2, 1) * s1 + pltpu.roll(a, ROPE_ROT // 2, 1) * s2)

    @pl.when(col >= 2 * D_MODEL)
    def _():
        for cb in range(tn // LANES):
            put(cb, acc[:, cb * LANES:(cb + 1) * LANES])


def _qkv_rope(x, g, w, tables, q_scale, group=0, dilation=1, *, tm=512, tn=512):
    b, s, d = x.shape
    rows = tm // dilation
    gcols = QKV_GROUP // tn
    if dilation > 1:
        tables = tuple(t.reshape(s // tm, rows, dilation, LANES).transpose(0, 2, 1, 3).reshape(s, LANES)
                       for t in tables)
    tab_spec = pl.BlockSpec((tm, LANES), lambda bi, i, j: (i, 0))
    in_specs = [pl.BlockSpec((None, tm, d), lambda bi, i, j: (bi, i, 0)),
                pl.BlockSpec((1, d), lambda bi, i, j: (0, 0)),
                pl.BlockSpec((d, tn), lambda bi, i, j: (0, group * gcols + j)),
                tab_spec, tab_spec, tab_spec]
    args = [x, g.reshape(1, d), w, *tables]
    if dilation > 1:
        in_specs.append(pl.BlockSpec((tm, tm), lambda bi, i, j: (0, 0)))
        args.append(jnp.asarray(_residue_perm(tm, dilation), BF16))
    return pl.pallas_call(
        functools.partial(_qkv_rope_kernel, tn=tn, q_scale=q_scale, dilation=dilation),
        grid=(b, s // tm, gcols),
        in_specs=in_specs,
        out_specs=pl.BlockSpec((None, dilation, rows, tn), lambda bi, i, j: (bi, 0, i, j)),
        out_shape=jax.ShapeDtypeStruct((b, dilation, s // dilation, QKV_GROUP), BF16),
        scratch_shapes=[pltpu.VMEM((tm, d), BF16)],
        compiler_params=_cparams(3),
        name=f"qkv_rope_d{dilation}",
    )(*args)


def _hgrn_act_kernel(x_ref, g_ref, w_ref, o_ref, h_scr, *, tiles_per_part):
    j = pl.program_id(1)

    @pl.when(j == 0)
    def _():
        h_scr[...] = _rms(x_ref[...], g_ref[...], NORM_EPS).astype(BF16)

    acc = jnp.dot(h_scr[...], w_ref[...], preferred_element_type=F32)

    @pl.when(j < tiles_per_part)
    def _():
        o_ref[...] = (acc * jax.nn.sigmoid(acc) * F32(C_DIM ** -0.5)).astype(o_ref.dtype)

    @pl.when((j >= tiles_per_part) & (j < 2 * tiles_per_part))
    def _():
        o_ref[...] = acc.astype(o_ref.dtype)

    @pl.when(j >= 2 * tiles_per_part)
    def _():
        o_ref[...] = (acc * jax.nn.sigmoid(acc)).astype(o_ref.dtype)


def _hgrn_act(x, g, w, *, tm=512, tn=512):
    m, d = x.shape
    per = D_MODEL // tn
    wcol = lambda j: jnp.where(j < per, j, j + 2 * per)
    return pl.pallas_call(
        functools.partial(_hgrn_act_kernel, tiles_per_part=per),
        grid=(m // tm, 3 * per),
        in_specs=[pl.BlockSpec((tm, d), lambda i, j: (i, 0)),
                  pl.BlockSpec((1, d), lambda i, j: (0, 0)),
                  pl.BlockSpec((d, tn), lambda i, j: (0, wcol(j)))],
        out_specs=pl.BlockSpec((tm, tn), lambda i, j: (i, j)),
        out_shape=jax.ShapeDtypeStruct((m, 3 * D_MODEL), BF16),
        scratch_shapes=[pltpu.VMEM((tm, d), BF16)],
        compiler_params=_cparams(2),
        name="hgrn_act",
    )(x, g.reshape(1, d), w)


def _hgrn_logf_kernel(x_ref, g_ref, w_ref, lb_ref, o_ref, h_scr, *, layer):
    j = pl.program_id(1)

    @pl.when(j == 0)
    def _():
        h_scr[...] = _rms(x_ref[...], g_ref[...], NORM_EPS).astype(BF16)

    fz = jnp.dot(h_scr[...], w_ref[...], preferred_element_type=F32)
    logits = lb_ref[...]
    e = jnp.exp(logits - jnp.max(logits, axis=0, keepdims=True))
    p = e / jnp.sum(e, axis=0, keepdims=True)
    cum = p[0:1]
    for r in range(1, layer + 1):
        cum = cum + p[r:r + 1]
    lb = cum - p[0:1]
    log_lb = jnp.log(lb)
    log_1m_lb = jnp.log1p(-lb)
    log_sig = jnp.minimum(fz, 0.0) - jnp.log1p(jnp.exp(-jnp.abs(fz)))
    o_ref[...] = _logaddexp(log_lb, log_1m_lb + log_sig)


def _hgrn_logf(x, g, w, lb_logits, layer, *, tm=512, tn=512):
    m, d = x.shape
    per = D_MODEL // tn
    return pl.pallas_call(
        functools.partial(_hgrn_logf_kernel, layer=layer),
        grid=(m // tm, 2 * per),
        in_specs=[pl.BlockSpec((tm, d), lambda i, j: (i, 0)),
                  pl.BlockSpec((1, d), lambda i, j: (0, 0)),
                  pl.BlockSpec((d, tn), lambda i, j: (0, j + per)),
                  pl.BlockSpec((DEPTH, tn), lambda i, j: (0, j % per))],
        out_specs=pl.BlockSpec((tm, tn), lambda i, j: (i, j)),
        out_shape=jax.ShapeDtypeStruct((m, 2 * D_MODEL), F32),
        scratch_shapes=[pltpu.VMEM((tm, d), BF16)],
        compiler_params=_cparams(2),
        name="hgrn_logf",
    )(x, g.reshape(1, d), w, lb_logits)


def _mm_norm_res_kernel(a_ref, w_ref, g_ref, x_ref, o_ref):
    h = jnp.dot(a_ref[...], w_ref[...], preferred_element_type=F32)
    o_ref[...] = x_ref[...] + _rms(h, g_ref[...], NORM_EPS)


def _mm_norm_res(a, w, g, x, *, tm=512):
    m, k = a.shape
    d = w.shape[1]
    return pl.pallas_call(
        _mm_norm_res_kernel,
        grid=(m // tm,),
        in_specs=[pl.BlockSpec((tm, k), lambda i: (i, 0)),
                  pl.BlockSpec((k, d), lambda i: (0, 0)),
                  pl.BlockSpec((1, d), lambda i: (0, 0)),
                  pl.BlockSpec((tm, d), lambda i: (i, 0))],
        out_specs=pl.BlockSpec((tm, d), lambda i: (i, 0)),
        out_shape=jax.ShapeDtypeStruct((m, d), F32),
        compiler_params=_cparams(1),
        name="mm_norm_res",
    )(a, w, g.reshape(1, d), x)


def _ffn_in_kernel(x_ref, xp_ref, xn_ref, g_ref, wa_ref, wb_ref, cwa_ref, cwb_ref, cba_ref, cbb_ref,
                   o_ref, h_scr, *, tm, seq):
    i = pl.program_id(0)
    j = pl.program_id(1)
    halo = BF16_ROWS
    rows = tm + 2 * halo

    @pl.when(j == 0)
    def _():
        g = g_ref[...]
        first = (i * tm) % seq == 0
        last = ((i + 1) * tm) % seq == 0
        hp = _rms(xp_ref[...], g, NORM_EPS)
        hn = _rms(xn_ref[...], g, NORM_EPS)
        h_scr[0:halo] = jnp.where(first, 0.0, hp).astype(BF16)
        h_scr[halo:halo + tm] = _rms(x_ref[...], g, NORM_EPS).astype(BF16)
        h_scr[halo + tm:rows] = jnp.where(last, 0.0, hn).astype(BF16)

    h = h_scr[...]

    def conv(w_ref, cw_ref, cb_ref):
        u = jnp.dot(h, w_ref[...], preferred_element_type=F32)
        cw = cw_ref[...]
        u_prev = pltpu.roll(u, 1, 0)
        u_next = pltpu.roll(u, rows - 1, 0)
        acc = cb_ref[...] + u_prev * cw[0:1] + u * cw[1:2] + u_next * cw[2:3]
        return acc[halo:halo + tm]

    a = conv(wa_ref, cwa_ref, cba_ref)
    b = conv(wb_ref, cwb_ref, cbb_ref)
    o_ref[...] = (jax.nn.gelu(a, approximate=True) * b).astype(o_ref.dtype)


def _ffn_in(x, g, w, conv_w, conv_b, seq, *, tm=512, tn=1408):
    m, d = x.shape
    nb = D_FF // tn
    halo = BF16_ROWS
    hb = tm // halo
    last_blk = m // halo - 1
    return pl.pallas_call(
        functools.partial(_ffn_in_kernel, tm=tm, seq=seq),
        grid=(m // tm, nb),
        in_specs=[pl.BlockSpec((tm, d), lambda i, j: (i, 0)),
                  pl.BlockSpec((halo, d), lambda i, j: (jnp.maximum(i * hb - 1, 0), 0)),
                  pl.BlockSpec((halo, d), lambda i, j: (jnp.minimum((i + 1) * hb, last_blk), 0)),
                  pl.BlockSpec((1, d), lambda i, j: (0, 0)),
                  pl.BlockSpec((d, tn), lambda i, j: (0, j)),
                  pl.BlockSpec((d, tn), lambda i, j: (0, j + nb)),
                  pl.BlockSpec((3, tn), lambda i, j: (0, j)),
                  pl.BlockSpec((3, tn), lambda i, j: (0, j + nb)),
                  pl.BlockSpec((1, tn), lambda i, j: (0, j)),
                  pl.BlockSpec((1, tn), lambda i, j: (0, j + nb))],
        out_specs=pl.BlockSpec((tm, tn), lambda i, j: (i, j)),
        out_shape=jax.ShapeDtypeStruct((m, D_FF), BF16),
        scratch_shapes=[pltpu.VMEM((tm + 2 * halo, d), BF16)],
        compiler_params=_cparams(2),
        name="ffn_in",
    )(x, x, x, g.reshape(1, d), w, w, conv_w, conv_w, conv_b.reshape(1, -1), conv_b.reshape(1, -1))


def _diff_attn_kernel(q_ref, k_ref, v_ref, lam_ref, g_ref, o_ref, q2_scr, vx_scr, m_scr, acc_scr, s_scr,
                      *, tq, tc, lam_init):
    seq = k_ref.shape[0]
    q = q_ref[...]
    lane = lax.broadcasted_iota(jnp.int32, q.shape, 1)
    zero = jnp.zeros_like(q)
    q2_scr[0:tq] = jnp.where(lane < A_HEAD_DIM, q, zero)
    q2_scr[tq:2 * tq] = jnp.where(lane >= A_HEAD_DIM, q, zero)
    m_scr[...] = jnp.full(m_scr.shape, -jnp.inf, F32)
    acc_scr[...] = jnp.zeros(acc_scr.shape, F32)

    @pl.when(pl.program_id(2) == 0)
    def _():
        vx_scr[:, 0:A_V_DIM] = v_ref[...]
        vx_scr[:, A_V_DIM:2 * A_V_DIM] = jnp.ones((seq, A_V_DIM), BF16)

    def scores(c, slot):
        r0 = c * tc if isinstance(c, int) else pl.multiple_of(c * tc, tc)
        s_scr[slot] = lax.dot_general(q2_scr[...], k_ref[pl.ds(r0, tc), :], (((1,), (1,)), ((), ())),
                                      preferred_element_type=F32)

    def softmax_pv(c, slot):
        r0 = c * tc if isinstance(c, int) else pl.multiple_of(c * tc, tc)
        slabs = [s_scr[slot, :, j * LANES:(j + 1) * LANES] for j in range(tc // LANES)]
        mx = slabs[0]
        for sl in slabs[1:]:
            mx = jnp.maximum(mx, sl)
        m_prev = m_scr[...]
        m_new = jnp.maximum(m_prev, jnp.max(mx, axis=1, keepdims=True))
        alpha = jnp.exp2(m_prev - m_new)
        p = jnp.concatenate([jnp.exp2(sl - m_new).astype(BF16) for sl in slabs], axis=1)
        pv = jnp.dot(p, vx_scr[pl.ds(r0, tc), :], preferred_element_type=F32)
        acc_scr[:, 0:A_V_DIM] = alpha * acc_scr[:, 0:A_V_DIM] + pv[:, 0:A_V_DIM]
        acc_scr[:, A_V_DIM:2 * A_V_DIM] = alpha * acc_scr[:, A_V_DIM:2 * A_V_DIM] + pv[:, A_V_DIM:2 * A_V_DIM]
        m_scr[...] = m_new

    n_chunks = seq // tc

    def body(it, carry):
        c = 2 * it
        scores(c + 1, 1)
        softmax_pv(c, 0)
        scores(c + 2, 0)
        softmax_pv(c + 1, 1)
        return carry

    scores(0, 0)
    lax.fori_loop(0, n_chunks // 2 - 1, body, 0)
    scores(n_chunks - 1, 1)
    softmax_pv(n_chunks - 2, 0)
    softmax_pv(n_chunks - 1, 1)

    o = acc_scr[:, 0:A_V_DIM] / acc_scr[:, A_V_DIM:2 * A_V_DIM]
    lv = lam_ref[...]
    lam = (jnp.exp(jnp.sum(lv[0:1] * lv[1:2], axis=1, keepdims=True))
           - jnp.exp(jnp.sum(lv[2:3] * lv[3:4], axis=1, keepdims=True)) + F32(lam_init))
    od = o[0:tq] - lam * o[tq:2 * tq]
    y = _rms(od, g_ref[...], A_SUBLN_EPS) * F32(1.0 - lam_init)
    o_ref[...] = y.astype(o_ref.dtype)


def _diff_attn(qkv, lam_vecs, subln_g, lam_init, *, tq=1024, tc=512):
    b, s, _ = qkv.shape
    tq = min(tq, s)
    blk = lambda rows, off: pl.BlockSpec((None, rows, LANES), off)
    return pl.pallas_call(
        functools.partial(_diff_attn_kernel, tq=tq, tc=tc, lam_init=lam_init),
        grid=(b, A_HEADS, s // tq),
        in_specs=[blk(tq, lambda bi, h, i: (bi, i, h)),
                  blk(s, lambda bi, h, i: (bi, 0, A_HEADS + h)),
                  blk(s, lambda bi, h, i: (bi, 0, 2 * A_HEADS + h)),
                  pl.BlockSpec((4, A_HEAD_DIM), lambda bi, h, i: (0, 0)),
                  pl.BlockSpec((1, A_V_DIM), lambda bi, h, i: (0, 0))],
        out_specs=blk(tq, lambda bi, h, i: (bi, i, h)),
        out_shape=jax.ShapeDtypeStruct((b, s, A_HEADS * A_V_DIM), BF16),
        scratch_shapes=[pltpu.VMEM((2 * tq, LANES), BF16),
                        pltpu.VMEM((s, 2 * A_V_DIM), BF16),
                        pltpu.VMEM((2 * tq, LANES), F32),
                        pltpu.VMEM((2 * tq, 2 * A_V_DIM), F32),
                        pltpu.VMEM((2, 2 * tq, tc), F32)],
        compiler_params=_cparams(3),
        name="diff_attn",
    )(qkv, qkv, qkv, lam_vecs, subln_g.reshape(1, A_V_DIM))


def _dil_attn_kernel(q_ref, k_ref, kp_ref, kn_ref, v_ref, vp_ref, vn_ref, o_ref, lse_ref, kbuf, vbuf,
                     *, tl, length):
    hw = B_HALF_W
    t0 = pl.program_id(2) * tl

    for buf, main, prev, nxt in ((kbuf, k_ref, kp_ref, kn_ref), (vbuf, v_ref, vp_ref, vn_ref)):
        buf[0:hw] = prev[...]
        buf[hw:hw + tl] = main[...]
        buf[hw + tl:2 * hw + tl] = nxt[...]

    a_io = lax.broadcasted_iota(jnp.int32, (B_QBLK, B_KBLK), 0)
    c_io = lax.broadcasted_iota(jnp.int32, (B_QBLK, B_KBLK), 1)
    band = (c_io >= a_io) & (c_io <= a_io + 2 * hw)
    lane = lax.broadcasted_iota(jnp.int32, (B_QBLK, LANES), 1)

    def q_block(jb, carry):
        r0 = pl.multiple_of(jb * B_QBLK, B_QBLK)
        ki = t0 + r0 - hw + c_io
        valid = band & (ki >= 0) & (ki < length)
        lse_tile = jnp.zeros((B_QBLK, LANES), F32)
        for h in range(B_HEADS):
            cols = slice(h * B_HEAD_DIM, (h + 1) * B_HEAD_DIM)
            qh = q_ref[pl.ds(r0, B_QBLK), cols]
            kh = kbuf[pl.ds(r0, B_KBLK), cols]
            vh = vbuf[pl.ds(r0, B_KBLK), cols]
            s = lax.dot_general(qh, kh, (((1,), (1,)), ((), ())), preferred_element_type=F32)
            s = jnp.where(valid, s, -jnp.inf)
            m = jnp.max(s, axis=1, keepdims=True)
            p = jnp.exp(s - m)
            l = jnp.sum(p, axis=1, keepdims=True)
            o = jnp.dot(p.astype(BF16), vh, preferred_element_type=F32) / l
            o_ref[pl.ds(r0, B_QBLK), cols] = o.astype(o_ref.dtype)
            lse_tile = jnp.where(lane == h, m + jnp.log(l), lse_tile)
        lse_ref[pl.ds(r0, B_QBLK), :] = lse_tile
        return carry

    lax.fori_loop(0, tl // B_QBLK, q_block, 0)


def _dil_attn_group(qkv, *, tl=512):
    b, dilation, length, _ = qkv.shape
    tl = min(tl, length)
    hw = B_HALF_W
    width = B_HEADS * B_HEAD_DIM
    per_tile = tl // hw
    n_hw = length // hw
    main = lambda part: pl.BlockSpec((None, None, tl, width), lambda bi, r, i: (bi, r, i, part))
    before = lambda part: pl.BlockSpec((None, None, hw, width),
                                       lambda bi, r, i: (bi, r, jnp.maximum(i * per_tile - 1, 0), part))
    after = lambda part: pl.BlockSpec((None, None, hw, width),
                                      lambda bi, r, i: (bi, r, jnp.minimum((i + 1) * per_tile, n_hw - 1), part))
    return pl.pallas_call(
        functools.partial(_dil_attn_kernel, tl=tl, length=length),
        grid=(b, dilation, length // tl),
        in_specs=[main(0), main(1), before(1), after(1), main(2), before(2), after(2)],
        out_specs=[pl.BlockSpec((None, None, tl, width), lambda bi, r, i: (bi, r, i, 0)),
                   pl.BlockSpec((None, None, tl, LANES), lambda bi, r, i: (bi, r, i, 0))],
        out_shape=[jax.ShapeDtypeStruct((b, dilation, length, width), BF16),
                   jax.ShapeDtypeStruct((b, dilation, length, LANES), F32)],
        scratch_shapes=[pltpu.VMEM((tl + 2 * hw, width), BF16), pltpu.VMEM((tl + 2 * hw, width), BF16)],
        compiler_params=_cparams(3),
        name=f"dil_attn_d{dilation}",
    )(*([qkv] * 7))


def _split3(x):
    hi = x.astype(BF16)
    r1 = x - hi.astype(F32)
    mid = r1.astype(BF16)
    lo = (r1 - mid.astype(F32)).astype(BF16)
    return hi, mid, lo


def _dil_merge_kernel(*refs, n_groups):
    o_refs = refs[:n_groups]
    l_refs = refs[n_groups:2 * n_groups]
    perm_refs = refs[2 * n_groups:3 * n_groups - 1]
    expand_ref, out_ref = refs[3 * n_groups - 1:]
    t = out_ref.shape[0]
    outs, lses = [], []
    for gi in range(n_groups):
        o = o_refs[gi][...].reshape(t, o_refs[gi].shape[-1])
        lse = l_refs[gi][...].reshape(t, LANES)
        if gi == 0:
            outs.append(o.astype(F32))
            lses.append(lse)
        else:
            perm = perm_refs[gi - 1][...]
            outs.append(jnp.dot(perm, o, preferred_element_type=F32))
            lses.append(sum(jnp.dot(perm, piece, preferred_element_type=F32) for piece in _split3(lse)))
    m = functools.reduce(jnp.maximum, lses)
    es = [jnp.exp(l - m) for l in lses]
    den = functools.reduce(lambda a, b: a + b, es)
    acc = jnp.zeros((t, out_ref.shape[1]), F32)
    for gi in range(n_groups):
        w = es[gi] / den
        hi = w.astype(BF16)
        lo = (w - hi.astype(F32)).astype(BF16)
        wide = (jnp.dot(hi, expand_ref[...], preferred_element_type=F32)
                + jnp.dot(lo, expand_ref[...], preferred_element_type=F32))
        acc = acc + wide * outs[gi]
    out_ref[...] = acc.astype(out_ref.dtype)


def _dil_merge(outs, lses, *, t=512):
    b, _, s, width = outs[0].shape
    n_groups = len(outs)
    in_specs, args = [], []
    for arrs, lanes in ((outs, width), (lses, LANES)):
        for a in arrs:
            d = a.shape[1]
            in_specs.append(pl.BlockSpec((None, d, t // d, lanes), lambda bi, i: (bi, 0, i, 0)))
            args.append(a)
    for a in outs[1:]:
        in_specs.append(pl.BlockSpec((t, t), lambda bi, i: (0, 0)))
        args.append(jnp.asarray(_residue_perm(t, a.shape[1]).T, BF16))
    expand = np.zeros((LANES, width), np.float32)
    expand[np.arange(width) // B_HEAD_DIM, np.arange(width)] = 1.0
    in_specs.append(pl.BlockSpec((LANES, width), lambda bi, i: (0, 0)))
    args.append(jnp.asarray(expand, BF16))
    return pl.pallas_call(
        functools.partial(_dil_merge_kernel, n_groups=n_groups),
        grid=(b, s // t),
        in_specs=in_specs,
        out_specs=pl.BlockSpec((None, t, width), lambda bi, i: (bi, i, 0)),
        out_shape=jax.ShapeDtypeStruct((b, s, width), BF16),
        compiler_params=_cparams(2),
        name="dil_merge",
    )(*args)


def _scan_constants(reverse):
    c = C_CHUNK
    t = np.arange(c)[:, None]
    u = np.arange(c)[None, :]
    if not reverse:
        mats = [u <= t, u > t]
    else:
        mats = [u >= t, u < t]
    masks = []
    for hs in C_LEVELS:
        base = (t // (2 * hs)) * (2 * hs)
        mid = base + hs - 1
        same = (u // (2 * hs)) == (t // (2 * hs))
        if not reverse:
            m = np.where(t > mid, (u > mid) & (u <= t), (u > t) & (u <= mid))
            masks.append(same & (t > mid) & (u <= mid))
        else:
            m = np.where(t <= mid, (u >= t) & (u <= mid), (u > mid) & (u < t))
            masks.append(same & (t <= mid) & (u > mid))
        mats.append(m & same)
    stack = np.concatenate([np.asarray(m, np.float32) for m in mats], axis=0)
    return jnp.asarray(stack, BF16), jnp.asarray(np.stack(masks).astype(np.float32))


def _hgrn_scan_kernel(*refs, tt, reverse, final):
    q_ref, v_ref, lf_ref, mstack_ref, masks_ref = refs[:5]
    refs = refs[5:]
    if final:
        ofw_ref, gate_ref, gn_ref = refs[:3]
        refs = refs[3:]
    o_ref, st_scr = refs
    c = C_CHUNK
    n_chunks = tt // c
    nt_contract = (((1,), (1,)), ((), ()))

    @pl.when(pl.program_id(2) == 0)
    def _():
        st_scr[...] = jnp.zeros(st_scr.shape, F32)

    row8 = lax.broadcasted_iota(jnp.int32, (C_DIAG, C_DIM), 0)
    lane_c = lax.broadcasted_iota(jnp.int32, (C_DIAG, c), 1)

    def chunk(ci, carry):
        cidx = (n_chunks - 1 - ci) if reverse else ci
        r0 = pl.multiple_of(cidx * c, c)
        lf = lf_ref[pl.ds(r0, c), :]
        x3 = jnp.concatenate(_split3(lf), axis=1)
        d3 = jnp.dot(mstack_ref[...], x3, preferred_element_type=F32)
        d = d3[:, 0:C_DIM] + d3[:, C_DIM:2 * C_DIM] + d3[:, 2 * C_DIM:3 * C_DIM]
        gq = d[0:c]
        gk = d[c:2 * c]

        q = q_ref[pl.ds(r0, c), :].astype(F32)
        v = v_ref[pl.ds(r0, c), :]
        kk = 1.0 - jnp.exp(lf)
        st = st_scr[...]

        o = lax.dot_general((q * jnp.exp(gq)).astype(BF16), st.astype(BF16), nt_contract,
                            preferred_element_type=F32)

        a = jnp.zeros((c, c), F32)
        for li in range(len(C_LEVELS)):
            e = jnp.exp(-jnp.abs(d[(2 + li) * c:(3 + li) * c]))
            pq = (q * e).astype(BF16)
            pk = (kk * e).astype(BF16)
            a = a + lax.dot_general(pq, pk, nt_contract, preferred_element_type=F32) * masks_ref[li]

        blocks = []
        for bi in range(c // C_DIAG):
            rows = slice(bi * C_DIAG, (bi + 1) * C_DIAG)
            gb, qb, kb = gq[rows], q[rows], kk[rows]
            ablk = jnp.zeros((C_DIAG, c), F32)
            for si in range(C_DIAG):
                keep = (row8 <= si) if reverse else (row8 >= si)
                dec = jnp.exp(jnp.where(keep, gb - gb[si:si + 1], -jnp.inf))
                sc = jnp.sum(qb * dec * kb[si:si + 1], axis=1, keepdims=True)
                ablk = jnp.where(lane_c == bi * C_DIAG + si, sc, ablk)
            blocks.append(ablk)
        a = a + jnp.concatenate(blocks, axis=0)

        o = o + jnp.dot(a.astype(BF16), v, preferred_element_type=F32)

        kd = (kk * jnp.exp(gk)).astype(BF16)
        upd = lax.dot_general(v, kd, (((0,), (0,)), ((), ())), preferred_element_type=F32)
        g_tot = gq[0:1] if reverse else gq[c - 1:c]
        st_scr[...] = st * jnp.exp(g_tot) + upd

        if final:
            tot = ofw_ref[pl.ds(r0, c), :] + o
            y = _rms(tot, gn_ref[...], NORM_EPS) * gate_ref[pl.ds(r0, c), :].astype(F32)
            o_ref[pl.ds(r0, c), :] = y.astype(o_ref.dtype)
        else:
            o_ref[pl.ds(r0, c), :] = o
        return carry

    lax.fori_loop(0, n_chunks, chunk, 0)


def _hgrn_scan(act, logf, reverse, o_fw=None, gnorm_g=None, *, tt=512):
    b, s, _ = act.shape
    tt = min(tt, s)
    nt = s // tt
    final = o_fw is not None
    mstack, masks = _scan_constants(reverse)
    tmap = (lambda t: nt - 1 - t) if reverse else (lambda t: t)
    blk = lambda off: pl.BlockSpec((None, tt, C_DIM), lambda bi, h, t: (bi, tmap(t), off + h))
    in_specs = [blk(0), blk(C_HEADS), blk(C_HEADS if reverse else 0),
                pl.BlockSpec(mstack.shape, lambda bi, h, t: (0, 0)),
                pl.BlockSpec(masks.shape, lambda bi, h, t: (0, 0, 0))]
    args = [act, act, logf, mstack, masks]
    if final:
        in_specs += [blk(0), blk(2 * C_HEADS), pl.BlockSpec((1, C_DIM), lambda bi, h, t: (0, 0))]
        args += [o_fw, act, gnorm_g.reshape(1, C_DIM)]
    return pl.pallas_call(
        functools.partial(_hgrn_scan_kernel, tt=tt, reverse=reverse, final=final),
        grid=(b, C_HEADS, nt),
        in_specs=in_specs,
        out_specs=blk(0),
        out_shape=jax.ShapeDtypeStruct((b, s, C_HEADS * C_DIM), BF16 if final else F32),
        scratch_shapes=[pltpu.VMEM((C_DIM, C_DIM), F32)],
        compiler_params=_cparams(3),
        name="hgrn_scan_bw" if reverse else "hgrn_scan_fw",
    )(*args)


def _diff_lambda_init(layer):
    return 0.8 - 0.6 * math.exp(-0.3 * layer)


def _trunk(x3, p):
    b, s, d = x3.shape
    x = x3.reshape(b * s, d)
    tables = _rope_tables(s)
    for layer in range(DEPTH):
        kind = layer % N_MIXERS
        j = layer // N_MIXERS
        g = p["norm_g"][layer]
        x3 = x.reshape(b, s, d)
        if kind == 0:
            qkv = _qkv_rope(x3, g[0], p["a_w_qkv"][j], tables, A_HEAD_DIM ** -0.5 * LOG2E)
            o = _diff_attn(qkv.reshape(b, s, -1), p["a_lambda"][j], p["a_subln_g"][j], _diff_lambda_init(layer))
            w_o = p["a_w_o"][j]
        elif kind == 1:
            outs, lses = [], []
            for group, (_, dilation) in enumerate(B_PATTERNS):
                qkv = _qkv_rope(x3, g[0], p["b_w_qkv"][j], tables, B_HEAD_DIM ** -0.5, group, dilation)
                o_g, lse_g = _dil_attn_group(qkv)
                outs.append(o_g)
                lses.append(lse_g)
            o = _dil_merge(outs, lses)
            w_o = p["b_w_o"][j]
        else:
            act = _hgrn_act(x, g[0], p["c_w_in"][j]).reshape(b, s, -1)
            logf = _hgrn_logf(x, g[0], p["c_w_in"][j], p["c_lb_logits"], layer).reshape(b, s, -1)
            o_fw = _hgrn_scan(act, logf, False)
            o = _hgrn_scan(act, logf, True, o_fw, p["c_gnorm_g"][j])
            w_o = p["c_w_o"][j]
        x = _mm_norm_res(o.reshape(b * s, -1), w_o, g[1], x)
        u = _ffn_in(x, g[2], p["f_w_in"][layer], p["f_conv_w"][layer], p["f_conv_b"][layer], s)
        x = _mm_norm_res(u, p["f_w_out"][layer], g[3], x)
    return x.reshape(b, s, d)


def kernel(x_prompt, x_sample, norm_g, a_w_qkv, a_lambda, a_subln_g, a_w_o, b_w_qkv, b_w_o, c_w_in,
           c_lb_logits, c_gnorm_g, c_w_o, f_w_in, f_conv_w, f_conv_b, f_w_out):
    bf = lambda w: w.astype(BF16)
    p = dict(norm_g=norm_g, a_w_qkv=bf(a_w_qkv), a_lambda=a_lambda, a_subln_g=a_subln_g, a_w_o=bf(a_w_o),
             b_w_qkv=bf(b_w_qkv), b_w_o=bf(b_w_o), c_w_in=bf(c_w_in), c_lb_logits=c_lb_logits,
             c_gnorm_g=c_gnorm_g, c_w_o=bf(c_w_o), f_w_in=bf(f_w_in), f_conv_w=f_conv_w,
             f_conv_b=f_conv_b, f_w_out=bf(f_w_out))
    return _trunk(x_prompt, p), _trunk(x_sample, p)
```

```python
import functools
import math

import numpy as np
import jax
import jax.numpy as jnp
from jax import lax
from jax.experimental import pallas as pl
from jax.experimental.pallas import tpu as pltpu

F32 = jnp.float32
BF16 = jnp.bfloat16

D_MODEL = 1024
DEPTH = 4
N_MIXERS = 3
ROPE_THETA = 500000.0
NORM_EPS = 1e-6

A_HEADS = 8
A_HEAD_DIM = 64
A_V_DIM = 128
A_SUBLN_EPS = 1e-5

B_HEADS = 16
B_HEAD_DIM = 64
B_PATTERNS = ((128, 1), (512, 4), (2048, 16))
B_HALF_W = 64
B_QBLK = 2 * B_HALF_W
B_KBLK = 4 * B_HALF_W

C_HEADS = 8
C_DIM = 128
C_CHUNK = 128
C_LEVELS = (64, 32, 16, 8)
C_DIAG = 8

D_FF = 2816
QKV_GROUP = 3 * D_MODEL
ROPE_ROT = 16
LANES = 128
LOG2E = math.log2(math.e)
BF16_ROWS = 16

V7X_VMEM_LIMIT = 56 * 1024 * 1024


def _cparams(n_axes):
    return pltpu.CompilerParams(dimension_semantics=("arbitrary",) * n_axes,
                                vmem_limit_bytes=V7X_VMEM_LIMIT)


def _const_spec(shape, index_map):
    return pl.BlockSpec(shape, index_map, pipeline_mode=pl.Buffered(1))


def _rms(x, g, eps):
    ms = jnp.mean(x * x, axis=-1, keepdims=True)
    return x * lax.rsqrt(ms + eps) * g


def _softplus_neg_abs(d):
    return jnp.log(1.0 + jnp.exp(-jnp.abs(d)))


def _logaddexp(a, b):
    return jnp.maximum(a, b) + _softplus_neg_abs(a - b)


def _rope_tables(seq):
    half = ROPE_ROT // 2
    inv_freq = ROPE_THETA ** (-jnp.arange(half, dtype=F32) / half)
    ang = jnp.arange(seq, dtype=F32)[:, None] * inv_freq[None, :]
    cos, sin = jnp.cos(ang), jnp.sin(ang)
    rest = B_HEAD_DIM - ROPE_ROT
    c = jnp.concatenate([cos, cos, jnp.ones((seq, rest), F32)], axis=1)
    s1 = jnp.concatenate([-sin, jnp.zeros((seq, rest + half), F32)], axis=1)
    s2 = jnp.concatenate([jnp.zeros((seq, half), F32), sin, jnp.zeros((seq, rest), F32)], axis=1)
    rep = LANES // B_HEAD_DIM
    return tuple(jnp.tile(t, (1, rep)) for t in (c, s1, s2))


def _residue_perm(tm, dilation):
    rows = tm // dilation
    rho = np.arange(tm)
    tok = (rho % rows) * dilation + rho // rows
    perm = np.zeros((tm, tm), np.float32)
    perm[rho, tok] = 1.0
    return perm


def _qkv_rope_kernel(*refs, tn, q_scale, dilation):
    if dilation > 1:
        x_ref, g_ref, w_ref, c_ref, s1_ref, s2_ref, perm_ref, o_ref = refs
    else:
        x_ref, g_ref, w_ref, c_ref, s1_ref, s2_ref, o_ref = refs
    tm = x_ref.shape[0]
    rows = tm // dilation

    h = _rms(x_ref[...], g_ref[...], NORM_EPS).astype(BF16)
    if dilation > 1:
        h = jnp.dot(perm_ref[...], h, preferred_element_type=F32).astype(BF16)

    def put(col, val):
        for rr in range(dilation):
            o_ref[rr, :, col:col + LANES] = val[rr * rows:(rr + 1) * rows].astype(o_ref.dtype)

    c, s1, s2 = c_ref[...], s1_ref[...], s2_ref[...]
    cq, s1q, s2q = c * F32(q_scale), s1 * F32(q_scale), s2 * F32(q_scale)
    for cn in range(QKV_GROUP // tn):
        acc = jnp.dot(h, w_ref[:, cn * tn:(cn + 1) * tn], preferred_element_type=F32)
        for cb in range(tn // LANES):
            col = cn * tn + cb * LANES
            a = acc[:, cb * LANES:(cb + 1) * LANES]
            if col < 2 * D_MODEL:
                tc, t1, t2 = (cq, s1q, s2q) if col < D_MODEL else (c, s1, s2)
                a = a * tc + pltpu.roll(a, LANES - ROPE_ROT // 2, 1) * t1 + pltpu.roll(a, ROPE_ROT // 2, 1) * t2
            put(col, a)


def _qkv_rope(x, g, w, tables, q_scale, group=0, dilation=1, *, tm=512, tn=512):
    b, s, d = x.shape
    rows = tm // dilation
    if dilation > 1:
        tables = tuple(t.reshape(s // tm, rows, dilation, LANES).transpose(0, 2, 1, 3).reshape(s, LANES)
                       for t in tables)
    tab_spec = pl.BlockSpec((tm, LANES), lambda bi, i: (i, 0))
    in_specs = [pl.BlockSpec((None, tm, d), lambda bi, i: (bi, i, 0)),
                _const_spec((1, d), lambda bi, i: (0, 0)),
                _const_spec((d, QKV_GROUP), lambda bi, i: (0, group)),
                tab_spec, tab_spec, tab_spec]
    args = [x, g.reshape(1, d), w, *tables]
    if dilation > 1:
        in_specs.append(_const_spec((tm, tm), lambda bi, i: (0, 0)))
        args.append(jnp.asarray(_residue_perm(tm, dilation), BF16))
    return pl.pallas_call(
        functools.partial(_qkv_rope_kernel, tn=tn, q_scale=q_scale, dilation=dilation),
        grid=(b, s // tm),
        in_specs=in_specs,
        out_specs=pl.BlockSpec((None, dilation, rows, QKV_GROUP), lambda bi, i: (bi, 0, i, 0)),
        out_shape=jax.ShapeDtypeStruct((b, dilation, s // dilation, QKV_GROUP), BF16),
        compiler_params=_cparams(2),
        name=f"qkv_rope_d{dilation}",
    )(*args)


def _hgrn_in_kernel(x_ref, g_ref, w_ref, lb_ref, act_ref, logf_ref, *, tn, layer):
    h = _rms(x_ref[...], g_ref[...], NORM_EPS).astype(BF16)
    logits = lb_ref[...]
    e = jnp.exp(logits - jnp.max(logits, axis=0, keepdims=True))
    p = e / jnp.sum(e, axis=0, keepdims=True)
    cum = p[0:1]
    for r in range(1, layer + 1):
        cum = cum + p[r:r + 1]
    lb = cum - p[0:1]
    log_lb = jnp.log(lb)
    log_1m_lb = jnp.log1p(-lb)

    for cn in range(5 * D_MODEL // tn):
        col = cn * tn
        part, off = col // D_MODEL, col % D_MODEL
        acc = jnp.dot(h, w_ref[:, col:col + tn], preferred_element_type=F32)
        if part == 0:
            act_ref[:, off:off + tn] = (acc * jax.nn.sigmoid(acc) * F32(C_DIM ** -0.5)).astype(act_ref.dtype)
        elif part in (1, 2):
            log_sig = jnp.minimum(acc, 0.0) - _softplus_neg_abs(acc)
            lo = (part - 1) * D_MODEL + off
            logf_ref[:, lo:lo + tn] = _logaddexp(log_lb[:, off:off + tn], log_1m_lb[:, off:off + tn] + log_sig)
        elif part == 3:
            act_ref[:, D_MODEL + off:D_MODEL + off + tn] = acc.astype(act_ref.dtype)
        else:
            act_ref[:, 2 * D_MODEL + off:2 * D_MODEL + off + tn] = (acc * jax.nn.sigmoid(acc)).astype(act_ref.dtype)


def _hgrn_in(x, g, w, lb_logits, layer, *, tm=512, tn=512):
    m, d = x.shape
    return pl.pallas_call(
        functools.partial(_hgrn_in_kernel, tn=tn, layer=layer),
        grid=(m // tm,),
        in_specs=[pl.BlockSpec((tm, d), lambda i: (i, 0)),
                  _const_spec((1, d), lambda i: (0, 0)),
                  _const_spec((d, 5 * D_MODEL), lambda i: (0, 0)),
                  _const_spec((DEPTH, D_MODEL), lambda i: (0, 0))],
        out_specs=[pl.BlockSpec((tm, 3 * D_MODEL), lambda i: (i, 0)),
                   pl.BlockSpec((tm, 2 * D_MODEL), lambda i: (i, 0))],
        out_shape=[jax.ShapeDtypeStruct((m, 3 * D_MODEL), BF16),
                   jax.ShapeDtypeStruct((m, 2 * D_MODEL), F32)],
        compiler_params=_cparams(1),
        name="hgrn_in",
    )(x, g.reshape(1, d), w, lb_logits)


def _mm_norm_res_kernel(a_ref, w_ref, g_ref, x_ref, o_ref, *, sub):
    for rb in range(a_ref.shape[0] // sub):
        rows = slice(rb * sub, (rb + 1) * sub)
        h = jnp.dot(a_ref[rows, :], w_ref[...], preferred_element_type=F32)
        o_ref[rows, :] = x_ref[rows, :] + _rms(h, g_ref[...], NORM_EPS)


def _mm_norm_res(a, w, g, x, *, tm=1024, sub=256):
    m, k = a.shape
    d = w.shape[1]
    return pl.pallas_call(
        functools.partial(_mm_norm_res_kernel, sub=sub),
        grid=(m // tm,),
        in_specs=[pl.BlockSpec((tm, k), lambda i: (i, 0)),
                  _const_spec((k, d), lambda i: (0, 0)),
                  _const_spec((1, d), lambda i: (0, 0)),
                  pl.BlockSpec((tm, d), lambda i: (i, 0))],
        out_specs=pl.BlockSpec((tm, d), lambda i: (i, 0)),
        out_shape=jax.ShapeDtypeStruct((m, d), F32),
        compiler_params=_cparams(1),
        name="mm_norm_res",
    )(a, w, g.reshape(1, d), x)


def _ffn_in_kernel(x_ref, xp_ref, xn_ref, g_ref, w_ref, cw_ref, cb_ref, o_ref, h_scr, u_scr, *, tm, tn, seq):
    i = pl.program_id(0)
    halo = BF16_ROWS
    rows = tm + 2 * halo

    g = g_ref[...]
    first = (i * tm) % seq == 0
    last = ((i + 1) * tm) % seq == 0
    hp = _rms(xp_ref[...], g, NORM_EPS)
    hn = _rms(xn_ref[...], g, NORM_EPS)
    h_scr[0:halo] = jnp.where(first, 0.0, hp).astype(BF16)
    h_scr[halo:halo + tm] = _rms(x_ref[...], g, NORM_EPS).astype(BF16)
    h_scr[halo + tm:rows] = jnp.where(last, 0.0, hn).astype(BF16)
    h = h_scr[...]

    def conv(col, u_scr):
        u_scr[...] = jnp.dot(h, w_ref[:, col:col + tn], preferred_element_type=F32)
        cw = cw_ref[:, col:col + tn]
        return (cb_ref[:, col:col + tn] + u_scr[halo - 1:halo - 1 + tm, :] * cw[0:1]
                + u_scr[halo:halo + tm, :] * cw[1:2] + u_scr[halo + 1:halo + 1 + tm, :] * cw[2:3])

    for cn in range(D_FF // tn):
        a = conv(cn * tn, u_scr.at[(2 * cn) % 4])
        b = conv(D_FF + cn * tn, u_scr.at[(2 * cn + 1) % 4])
        o_ref[:, cn * tn:(cn + 1) * tn] = (jax.nn.gelu(a, approximate=True) * b).astype(o_ref.dtype)


def _ffn_in(x, g, w, conv_w, conv_b, seq, *, tm=512, tn=256):
    m, d = x.shape
    halo = BF16_ROWS
    hb = tm // halo
    last_blk = m // halo - 1
    return pl.pallas_call(
        functools.partial(_ffn_in_kernel, tm=tm, tn=tn, seq=seq),
        grid=(m // tm,),
        in_specs=[pl.BlockSpec((tm, d), lambda i: (i, 0)),
                  pl.BlockSpec((halo, d), lambda i: (jnp.maximum(i * hb - 1, 0), 0)),
                  pl.BlockSpec((halo, d), lambda i: (jnp.minimum((i + 1) * hb, last_blk), 0)),
                  _const_spec((1, d), lambda i: (0, 0)),
                  _const_spec((d, 2 * D_FF), lambda i: (0, 0)),
                  _const_spec((3, 2 * D_FF), lambda i: (0, 0)),
                  _const_spec((1, 2 * D_FF), lambda i: (0, 0))],
        out_specs=pl.BlockSpec((tm, D_FF), lambda i: (i, 0)),
        out_shape=jax.ShapeDtypeStruct((m, D_FF), BF16),
        scratch_shapes=[pltpu.VMEM((tm + 2 * halo, d), BF16), pltpu.VMEM((4, tm + 2 * halo, tn), F32)],
        compiler_params=_cparams(1),
        name="ffn_in",
    )(x, x, x, g.reshape(1, d), w, conv_w, conv_b.reshape(1, -1))


def _diff_attn_kernel(q_ref, k_ref, v_ref, lam_ref, g_ref, o_ref, q2_scr, vx_scr, m_scr, acc_scr, s_scr,
                      *, tq, tc, lam_init):
    seq = k_ref.shape[0]
    q = q_ref[...]
    lane = lax.broadcasted_iota(jnp.int32, q.shape, 1)
    zero = jnp.zeros_like(q)
    q2_scr[0:tq] = jnp.where(lane < A_HEAD_DIM, q, zero)
    q2_scr[tq:2 * tq] = jnp.where(lane >= A_HEAD_DIM, q, zero)
    m_scr[...] = jnp.full(m_scr.shape, -jnp.inf, F32)
    acc_scr[...] = jnp.zeros(acc_scr.shape, F32)

    @pl.when(pl.program_id(2) == 0)
    def _():
        vx_scr[:, 0:A_V_DIM] = v_ref[...]
        vx_scr[:, A_V_DIM:2 * A_V_DIM] = jnp.ones((seq, A_V_DIM), BF16)

    def scores(c, slot):
        r0 = c * tc if isinstance(c, int) else pl.multiple_of(c * tc, tc)
        s_scr[slot] = lax.dot_general(q2_scr[...], k_ref[pl.ds(r0, tc), :], (((1,), (1,)), ((), ())),
                                      preferred_element_type=F32)

    def softmax_pv(c, slot):
        r0 = c * tc if isinstance(c, int) else pl.multiple_of(c * tc, tc)
        slabs = [s_scr[slot, :, j * LANES:(j + 1) * LANES] for j in range(tc // LANES)]
        mx = slabs[0]
        for sl in slabs[1:]:
            mx = jnp.maximum(mx, sl)
        m_prev = m_scr[...]
        m_new = jnp.maximum(m_prev, jnp.max(mx, axis=1, keepdims=True))
        alpha = jnp.exp2(m_prev - m_new)
        p = jnp.concatenate([jnp.exp2(sl - m_new).astype(BF16) for sl in slabs], axis=1)
        pv = jnp.dot(p, vx_scr[pl.ds(r0, tc), :], preferred_element_type=F32)
        acc_scr[:, 0:A_V_DIM] = alpha * acc_scr[:, 0:A_V_DIM] + pv[:, 0:A_V_DIM]
        acc_scr[:, A_V_DIM:2 * A_V_DIM] = alpha * acc_scr[:, A_V_DIM:2 * A_V_DIM] + pv[:, A_V_DIM:2 * A_V_DIM]
        m_scr[...] = m_new

    n_chunks = seq // tc

    def body(it, carry):
        c = 2 * it
        scores(c + 1, 1)
        softmax_pv(c, 0)
        scores(c + 2, 0)
        softmax_pv(c + 1, 1)
        return carry

    scores(0, 0)
    lax.fori_loop(0, n_chunks // 2 - 1, body, 0)
    scores(n_chunks - 1, 1)
    softmax_pv(n_chunks - 2, 0)
    softmax_pv(n_chunks - 1, 1)

    o = acc_scr[:, 0:A_V_DIM] / acc_scr[:, A_V_DIM:2 * A_V_DIM]
    lv = lam_ref[...]
    lam = (jnp.exp(jnp.sum(lv[0:1] * lv[1:2], axis=1, keepdims=True))
           - jnp.exp(jnp.sum(lv[2:3] * lv[3:4], axis=1, keepdims=True)) + F32(lam_init))
    od = o[0:tq] - lam * o[tq:2 * tq]
    y = _rms(od, g_ref[...], A_SUBLN_EPS) * F32(1.0 - lam_init)
    o_ref[...] = y.astype(o_ref.dtype)


def _diff_attn(qkv, lam_vecs, subln_g, lam_init, *, tq=1024, tc=512):
    b, s, _ = qkv.shape
    tq = min(tq, s)
    blk = lambda rows, off: pl.BlockSpec((None, rows, LANES), off)
    return pl.pallas_call(
        functools.partial(_diff_attn_kernel, tq=tq, tc=tc, lam_init=lam_init),
        grid=(b, A_HEADS, s // tq),
        in_specs=[blk(tq, lambda bi, h, i: (bi, i, h)),
                  blk(s, lambda bi, h, i: (bi, 0, A_HEADS + h)),
                  blk(s, lambda bi, h, i: (bi, 0, 2 * A_HEADS + h)),
                  pl.BlockSpec((4, A_HEAD_DIM), lambda bi, h, i: (0, 0)),
                  pl.BlockSpec((1, A_V_DIM), lambda bi, h, i: (0, 0))],
        out_specs=blk(tq, lambda bi, h, i: (bi, i, h)),
        out_shape=jax.ShapeDtypeStruct((b, s, A_HEADS * A_V_DIM), BF16),
        scratch_shapes=[pltpu.VMEM((2 * tq, LANES), BF16),
                        pltpu.VMEM((s, 2 * A_V_DIM), BF16),
                        pltpu.VMEM((2 * tq, LANES), F32),
                        pltpu.VMEM((2 * tq, 2 * A_V_DIM), F32),
                        pltpu.VMEM((2, 2 * tq, tc), F32)],
        compiler_params=_cparams(3),
        name="diff_attn",
    )(qkv, qkv, qkv, lam_vecs, subln_g.reshape(1, A_V_DIM))


def _dil_attn_kernel(q_ref, k_ref, kp_ref, kn_ref, v_ref, vp_ref, vn_ref, o_ref, lse_ref, kbuf, vbuf,
                     *, tl, length):
    hw = B_HALF_W
    t0 = pl.program_id(2) * tl

    for buf, main, prev, nxt in ((kbuf, k_ref, kp_ref, kn_ref), (vbuf, v_ref, vp_ref, vn_ref)):
        buf[0:hw] = prev[...]
        buf[hw:hw + tl] = main[...]
        buf[hw + tl:2 * hw + tl] = nxt[...]

    a_io = lax.broadcasted_iota(jnp.int32, (B_QBLK, B_KBLK), 0)
    c_io = lax.broadcasted_iota(jnp.int32, (B_QBLK, B_KBLK), 1)
    band = (c_io >= a_io) & (c_io <= a_io + 2 * hw)
    lane = lax.broadcasted_iota(jnp.int32, (B_QBLK, LANES), 1)

    def q_block(jb, carry):
        r0 = pl.multiple_of(jb * B_QBLK, B_QBLK)
        ki = t0 + r0 - hw + c_io
        valid = band & (ki >= 0) & (ki < length)
        lse_tile = jnp.zeros((B_QBLK, LANES), F32)
        for h in range(B_HEADS):
            cols = slice(h * B_HEAD_DIM, (h + 1) * B_HEAD_DIM)
            qh = q_ref[pl.ds(r0, B_QBLK), cols]
            kh = kbuf[pl.ds(r0, B_KBLK), cols]
            vh = vbuf[pl.ds(r0, B_KBLK), cols]
            s = lax.dot_general(qh, kh, (((1,), (1,)), ((), ())), preferred_element_type=F32)
            s = jnp.where(valid, s, -jnp.inf)
            m = jnp.max(s, axis=1, keepdims=True)
            p = jnp.exp(s - m)
            l = jnp.sum(p, axis=1, keepdims=True)
            o = jnp.dot(p.astype(BF16), vh, preferred_element_type=F32) / l
            o_ref[pl.ds(r0, B_QBLK), cols] = o.astype(o_ref.dtype)
            lse_tile = jnp.where(lane == h, m + jnp.log(l), lse_tile)
        lse_ref[pl.ds(r0, B_QBLK), :] = lse_tile
        return carry

    lax.fori_loop(0, tl // B_QBLK, q_block, 0)


def _dil_attn_group(qkv, *, tl=512):
    b, dilation, length, _ = qkv.shape
    tl = min(tl, length)
    hw = B_HALF_W
    width = B_HEADS * B_HEAD_DIM
    per_tile = tl // hw
    n_hw = length // hw
    main = lambda part: pl.BlockSpec((None, None, tl, width), lambda bi, r, i: (bi, r, i, part))
    before = lambda part: pl.BlockSpec((None, None, hw, width),
                                       lambda bi, r, i: (bi, r, jnp.maximum(i * per_tile - 1, 0), part))
    after = lambda part: pl.BlockSpec((None, None, hw, width),
                                      lambda bi, r, i: (bi, r, jnp.minimum((i + 1) * per_tile, n_hw - 1), part))
    return pl.pallas_call(
        functools.partial(_dil_attn_kernel, tl=tl, length=length),
        grid=(b, dilation, length // tl),
        in_specs=[main(0), main(1), before(1), after(1), main(2), before(2), after(2)],
        out_specs=[pl.BlockSpec((None, None, tl, width), lambda bi, r, i: (bi, r, i, 0)),
                   pl.BlockSpec((None, None, tl, LANES), lambda bi, r, i: (bi, r, i, 0))],
        out_shape=[jax.ShapeDtypeStruct((b, dilation, length, width), BF16),
                   jax.ShapeDtypeStruct((b, dilation, length, LANES), F32)],
        scratch_shapes=[pltpu.VMEM((tl + 2 * hw, width), BF16), pltpu.VMEM((tl + 2 * hw, width), BF16)],
        compiler_params=_cparams(3),
        name=f"dil_attn_d{dilation}",
    )(*([qkv] * 7))


def _split3(x):
    hi = x.astype(BF16)
    r1 = x - hi.astype(F32)
    mid = r1.astype(BF16)
    lo = (r1 - mid.astype(F32)).astype(BF16)
    return hi, mid, lo


def _dil_merge_kernel(*refs, n_groups):
    o_refs = refs[:n_groups]
    l_refs = refs[n_groups:2 * n_groups]
    perm_refs = refs[2 * n_groups:3 * n_groups - 1]
    expand_ref, out_ref = refs[3 * n_groups - 1:]
    t = out_ref.shape[0]
    outs, lses = [], []
    for gi in range(n_groups):
        o = o_refs[gi][...].reshape(t, o_refs[gi].shape[-1])
        lse = l_refs[gi][...].reshape(t, LANES)
        if gi == 0:
            outs.append(o.astype(F32))
            lses.append(lse)
        else:
            perm = perm_refs[gi - 1][...]
            outs.append(jnp.dot(perm, o, preferred_element_type=F32))
            lses.append(sum(jnp.dot(perm, piece, preferred_element_type=F32) for piece in _split3(lse)))
    m = functools.reduce(jnp.maximum, lses)
    es = [jnp.exp(l - m) for l in lses]
    den = functools.reduce(lambda a, b: a + b, es)
    acc = jnp.zeros((t, out_ref.shape[1]), F32)
    for gi in range(n_groups):
        w = es[gi] / den
        hi = w.astype(BF16)
        lo = (w - hi.astype(F32)).astype(BF16)
        wide = (jnp.dot(hi, expand_ref[...], preferred_element_type=F32)
                + jnp.dot(lo, expand_ref[...], preferred_element_type=F32))
        acc = acc + wide * outs[gi]
    out_ref[...] = acc.astype(out_ref.dtype)


def _dil_merge(outs, lses, *, t=512):
    b, _, s, width = outs[0].shape
    n_groups = len(outs)
    in_specs, args = [], []
    for arrs, lanes in ((outs, width), (lses, LANES)):
        for a in arrs:
            d = a.shape[1]
            in_specs.append(pl.BlockSpec((None, d, t // d, lanes), lambda bi, i: (bi, 0, i, 0)))
            args.append(a)
    for a in outs[1:]:
        in_specs.append(pl.BlockSpec((t, t), lambda bi, i: (0, 0)))
        args.append(jnp.asarray(_residue_perm(t, a.shape[1]).T, BF16))
    expand = np.zeros((LANES, width), np.float32)
    expand[np.arange(width) // B_HEAD_DIM, np.arange(width)] = 1.0
    in_specs.append(pl.BlockSpec((LANES, width), lambda bi, i: (0, 0)))
    args.append(jnp.asarray(expand, BF16))
    return pl.pallas_call(
        functools.partial(_dil_merge_kernel, n_groups=n_groups),
        grid=(b, s // t),
        in_specs=in_specs,
        out_specs=pl.BlockSpec((None, t, width), lambda bi, i: (bi, i, 0)),
        out_shape=jax.ShapeDtypeStruct((b, s, width), BF16),
        compiler_params=_cparams(2),
        name="dil_merge",
    )(*args)


def _scan_constants(reverse):
    c = C_CHUNK
    t = np.arange(c)[:, None]
    u = np.arange(c)[None, :]
    if not reverse:
        mats = [u <= t, u > t]
    else:
        mats = [u >= t, u < t]
    masks = []
    for hs in C_LEVELS:
        base = (t // (2 * hs)) * (2 * hs)
        mid = base + hs - 1
        same = (u // (2 * hs)) == (t // (2 * hs))
        if not reverse:
            m = np.where(t > mid, (u > mid) & (u <= t), (u > t) & (u <= mid))
            masks.append(same & (t > mid) & (u <= mid))
        else:
            m = np.where(t <= mid, (u >= t) & (u <= mid), (u > mid) & (u < t))
            masks.append(same & (t <= mid) & (u > mid))
        mats.append(m & same)
    stack = np.concatenate([np.asarray(m, np.float32) for m in mats], axis=0)
    return jnp.asarray(stack, BF16), jnp.asarray(np.stack(masks).astype(np.float32))


def _hgrn_scan_kernel(*refs, tt, reverse, final):
    q_ref, v_ref, lf_ref, mstack_ref, masks_ref = refs[:5]
    refs = refs[5:]
    if final:
        ofw_ref, gate_ref, gn_ref = refs[:3]
        refs = refs[3:]
    o_ref, st_scr = refs
    c = C_CHUNK
    n_chunks = tt // c
    nt_contract = (((1,), (1,)), ((), ()))

    @pl.when(pl.program_id(2) == 0)
    def _():
        st_scr[...] = jnp.zeros(st_scr.shape, F32)

    row8 = lax.broadcasted_iota(jnp.int32, (C_DIAG, C_DIM), 0)
    lane_c = lax.broadcasted_iota(jnp.int32, (C_DIAG, c), 1)

    def chunk(ci, carry):
        cidx = (n_chunks - 1 - ci) if reverse else ci
        r0 = pl.multiple_of(cidx * c, c)
        lf = lf_ref[pl.ds(r0, c), :]
        x3 = jnp.concatenate(_split3(lf), axis=1)
        d3 = jnp.dot(mstack_ref[...], x3, preferred_element_type=F32)
        d = d3[:, 0:C_DIM] + d3[:, C_DIM:2 * C_DIM] + d3[:, 2 * C_DIM:3 * C_DIM]
        gq = d[0:c]
        gk = d[c:2 * c]

        q = q_ref[pl.ds(r0, c), :].astype(F32)
        v = v_ref[pl.ds(r0, c), :]
        kk = 1.0 - jnp.exp(lf)
        st = st_scr[...]

        o = lax.dot_general((q * jnp.exp(gq)).astype(BF16), st.astype(BF16), nt_contract,
                            preferred_element_type=F32)

        a = jnp.zeros((c, c), F32)
        for li in range(len(C_LEVELS)):
            e = jnp.exp(-jnp.abs(d[(2 + li) * c:(3 + li) * c]))
            pq = (q * e).astype(BF16)
            pk = (kk * e).astype(BF16)
            a = a + lax.dot_general(pq, pk, nt_contract, preferred_element_type=F32) * masks_ref[li]

        blocks = []
        for bi in range(c // C_DIAG):
            rows = slice(bi * C_DIAG, (bi + 1) * C_DIAG)
            gb, qb, kb = gq[rows], q[rows], kk[rows]
            ablk = jnp.zeros((C_DIAG, c), F32)
            for si in range(C_DIAG):
                keep = (row8 <= si) if reverse else (row8 >= si)
                dec = jnp.exp(jnp.where(keep, gb - gb[si:si + 1], -jnp.inf))
                sc = jnp.sum(qb * dec * kb[si:si + 1], axis=1, keepdims=True)
                ablk = jnp.where(lane_c == bi * C_DIAG + si, sc, ablk)
            blocks.append(ablk)
        a = a + jnp.concatenate(blocks, axis=0)

        o = o + jnp.dot(a.astype(BF16), v, preferred_element_type=F32)

        kd = (kk * jnp.exp(gk)).astype(BF16)
        upd = lax.dot_general(v, kd, (((0,), (0,)), ((), ())), preferred_element_type=F32)
        g_tot = gq[0:1] if reverse else gq[c - 1:c]
        st_scr[...] = st * jnp.exp(g_tot) + upd

        if final:
            tot = ofw_ref[pl.ds(r0, c), :] + o
            y = _rms(tot, gn_ref[...], NORM_EPS) * gate_ref[pl.ds(r0, c), :].astype(F32)
            o_ref[pl.ds(r0, c), :] = y.astype(o_ref.dtype)
        else:
            o_ref[pl.ds(r0, c), :] = o
        return carry

    lax.fori_loop(0, n_chunks, chunk, 0)


def _hgrn_scan(act, logf, reverse, o_fw=None, gnorm_g=None, *, tt=512):
    b, s, _ = act.shape
    tt = min(tt, s)
    nt = s // tt
    final = o_fw is not None
    mstack, masks = _scan_constants(reverse)
    tmap = (lambda t: nt - 1 - t) if reverse else (lambda t: t)
    blk = lambda off: pl.BlockSpec((None, tt, C_DIM), lambda bi, h, t: (bi, tmap(t), off + h))
    in_specs = [blk(0), blk(C_HEADS), blk(C_HEADS if reverse else 0),
                pl.BlockSpec(mstack.shape, lambda bi, h, t: (0, 0)),
                pl.BlockSpec(masks.shape, lambda bi, h, t: (0, 0, 0))]
    args = [act, act, logf, mstack, masks]
    if final:
        in_specs += [blk(0), blk(2 * C_HEADS), pl.BlockSpec((1, C_DIM), lambda bi, h, t: (0, 0))]
        args += [o_fw, act, gnorm_g.reshape(1, C_DIM)]
    return pl.pallas_call(
        functools.partial(_hgrn_scan_kernel, tt=tt, reverse=reverse, final=final),
        grid=(b, C_HEADS, nt),
        in_specs=in_specs,
        out_specs=blk(0),
        out_shape=jax.ShapeDtypeStruct((b, s, C_HEADS * C_DIM), BF16 if final else F32),
        scratch_shapes=[pltpu.VMEM((C_DIM, C_DIM), F32)],
        compiler_params=_cparams(3),
        name="hgrn_scan_bw" if reverse else "hgrn_scan_fw",
    )(*args)


def _diff_lambda_init(layer):
    return 0.8 - 0.6 * math.exp(-0.3 * layer)


def _trunk(x3, p):
    b, s, d = x3.shape
    x = x3.reshape(b * s, d)
    tables = _rope_tables(s)
    for layer in range(DEPTH):
        kind = layer % N_MIXERS
        j = layer // N_MIXERS
        g = p["norm_g"][layer]
        x3 = x.reshape(b, s, d)
        if kind == 0:
            qkv = _qkv_rope(x3, g[0], p["a_w_qkv"][j], tables, A_HEAD_DIM ** -0.5 * LOG2E)
            o = _diff_attn(qkv.reshape(b, s, -1), p["a_lambda"][j], p["a_subln_g"][j], _diff_lambda_init(layer))
            w_o = p["a_w_o"][j]
        elif kind == 1:
            outs, lses = [], []
            for group, (_, dilation) in enumerate(B_PATTERNS):
                qkv = _qkv_rope(x3, g[0], p["b_w_qkv"][j], tables, B_HEAD_DIM ** -0.5, group, dilation)
                o_g, lse_g = _dil_attn_group(qkv)
                outs.append(o_g)
                lses.append(lse_g)
            o = _dil_merge(outs, lses)
            w_o = p["b_w_o"][j]
        else:
            act, logf = _hgrn_in(x, g[0], p["c_w_in"][j], p["c_lb_logits"], layer)
            act, logf = act.reshape(b, s, -1), logf.reshape(b, s, -1)
            o_fw = _hgrn_scan(act, logf, False)
            o = _hgrn_scan(act, logf, True, o_fw, p["c_gnorm_g"][j])
            w_o = p["c_w_o"][j]
        x = _mm_norm_res(o.reshape(b * s, -1), w_o, g[1], x)
        u = _ffn_in(x, g[2], p["f_w_in"][layer], p["f_conv_w"][layer], p["f_conv_b"][layer], s)
        x = _mm_norm_res(u, p["f_w_out"][layer], g[3], x)
    return x.reshape(b, s, d)


def kernel(x_prompt, x_sample, norm_g, a_w_qkv, a_lambda, a_subln_g, a_w_o, b_w_qkv, b_w_o, c_w_in,
           c_lb_logits, c_gnorm_g, c_w_o, f_w_in, f_conv_w, f_conv_b, f_w_out):
    bf = lambda w: w.astype(BF16)
    p = dict(norm_g=norm_g, a_w_qkv=bf(a_w_qkv), a_lambda=a_lambda, a_subln_g=a_subln_g, a_w_o=bf(a_w_o),
             b_w_qkv=bf(b_w_qkv), b_w_o=bf(b_w_o), c_w_in=bf(c_w_in), c_lb_logits=c_lb_logits,
             c_gnorm_g=c_gnorm_g, c_w_o=bf(c_w_o), f_w_in=bf(f_w_in), f_conv_w=f_conv_w,
             f_conv_b=f_conv_b, f_w_out=bf(f_w_out))
    return _trunk(x_prompt, p), _trunk(x_sample, p)
```

```python
import functools
import math

import numpy as np
import jax
import jax.numpy as jnp
from jax import lax
from jax.experimental import pallas as pl
from jax.experimental.pallas import tpu as pltpu

F32 = jnp.float32
BF16 = jnp.bfloat16

D_MODEL = 1024
DEPTH = 4
N_MIXERS = 3
ROPE_THETA = 500000.0
NORM_EPS = 1e-6

A_HEADS = 8
A_HEAD_DIM = 64
A_V_DIM = 128
A_SUBLN_EPS = 1e-5

B_HEADS = 16
B_HEAD_DIM = 64
B_PATTERNS = ((128, 1), (512, 4), (2048, 16))
B_HALF_W = 64
B_QBLK = 2 * B_HALF_W
B_KBLK = 4 * B_HALF_W

C_HEADS = 8
C_DIM = 128
C_CHUNK = 128
C_LEVELS = (64, 32, 16, 8)
C_DIAG = 8

D_FF = 2816
QKV_GROUP = 3 * D_MODEL
ROPE_ROT = 16
LANES = 128
LOG2E = math.log2(math.e)
BF16_ROWS = 16

V7X_VMEM_LIMIT = 56 * 1024 * 1024


def _cparams(n_axes):
    return pltpu.CompilerParams(dimension_semantics=("arbitrary",) * n_axes,
                                vmem_limit_bytes=V7X_VMEM_LIMIT)


def _const_spec(shape, index_map):
    return pl.BlockSpec(shape, index_map, pipeline_mode=pl.Buffered(1))


def _rms(x, g, eps):
    ms = jnp.mean(x * x, axis=-1, keepdims=True)
    return x * lax.rsqrt(ms + eps) * g


def _softplus_neg_abs(d):
    return jnp.log(1.0 + jnp.exp(-jnp.abs(d)))


def _logaddexp(a, b):
    return jnp.maximum(a, b) + _softplus_neg_abs(a - b)


def _rope_tables(seq):
    half = ROPE_ROT // 2
    inv_freq = ROPE_THETA ** (-jnp.arange(half, dtype=F32) / half)
    ang = jnp.arange(seq, dtype=F32)[:, None] * inv_freq[None, :]
    cos, sin = jnp.cos(ang), jnp.sin(ang)
    rest = B_HEAD_DIM - ROPE_ROT
    c = jnp.concatenate([cos, cos, jnp.ones((seq, rest), F32)], axis=1)
    s1 = jnp.concatenate([-sin, jnp.zeros((seq, rest + half), F32)], axis=1)
    s2 = jnp.concatenate([jnp.zeros((seq, half), F32), sin, jnp.zeros((seq, rest), F32)], axis=1)
    rep = LANES // B_HEAD_DIM
    return tuple(jnp.tile(t, (1, rep)) for t in (c, s1, s2))


def _residue_perm(tm, dilation):
    rows = tm // dilation
    rho = np.arange(tm)
    tok = (rho % rows) * dilation + rho // rows
    perm = np.zeros((tm, tm), np.float32)
    perm[rho, tok] = 1.0
    return perm


def _qkv_rope_kernel(*refs, tn, q_scale, dilation):
    if dilation > 1:
        x_ref, g_ref, w_ref, c_ref, s1_ref, s2_ref, perm_ref, o_ref = refs
    else:
        x_ref, g_ref, w_ref, c_ref, s1_ref, s2_ref, o_ref = refs
    tm = x_ref.shape[0]
    rows = tm // dilation

    h = _rms(x_ref[...], g_ref[...], NORM_EPS).astype(BF16)
    if dilation > 1:
        h = jnp.dot(perm_ref[...], h, preferred_element_type=F32).astype(BF16)

    def put(col, val):
        for rr in range(dilation):
            o_ref[rr, :, col:col + LANES] = val[rr * rows:(rr + 1) * rows].astype(o_ref.dtype)

    c, s1, s2 = c_ref[...], s1_ref[...], s2_ref[...]
    cq, s1q, s2q = c * F32(q_scale), s1 * F32(q_scale), s2 * F32(q_scale)
    for cn in range(QKV_GROUP // tn):
        acc = jnp.dot(h, w_ref[:, cn * tn:(cn + 1) * tn], preferred_element_type=F32)
        for cb in range(tn // LANES):
            col = cn * tn + cb * LANES
            a = acc[:, cb * LANES:(cb + 1) * LANES]
            if col < 2 * D_MODEL:
                tc, t1, t2 = (cq, s1q, s2q) if col < D_MODEL else (c, s1, s2)
                a = a * tc + pltpu.roll(a, LANES - ROPE_ROT // 2, 1) * t1 + pltpu.roll(a, ROPE_ROT // 2, 1) * t2
            put(col, a)


def _qkv_rope(x, g, w, tables, q_scale, group=0, dilation=1, *, tm=512, tn=512):
    b, s, d = x.shape
    rows = tm // dilation
    if dilation > 1:
        tables = tuple(t.reshape(s // tm, rows, dilation, LANES).transpose(0, 2, 1, 3).reshape(s, LANES)
                       for t in tables)
    tab_spec = pl.BlockSpec((tm, LANES), lambda bi, i: (i, 0))
    in_specs = [pl.BlockSpec((None, tm, d), lambda bi, i: (bi, i, 0)),
                _const_spec((1, d), lambda bi, i: (0, 0)),
                _const_spec((d, QKV_GROUP), lambda bi, i: (0, group)),
                tab_spec, tab_spec, tab_spec]
    args = [x, g.reshape(1, d), w, *tables]
    if dilation > 1:
        in_specs.append(_const_spec((tm, tm), lambda bi, i: (0, 0)))
        args.append(jnp.asarray(_residue_perm(tm, dilation), BF16))
    return pl.pallas_call(
        functools.partial(_qkv_rope_kernel, tn=tn, q_scale=q_scale, dilation=dilation),
        grid=(b, s // tm),
        in_specs=in_specs,
        out_specs=pl.BlockSpec((None, dilation, rows, QKV_GROUP), lambda bi, i: (bi, 0, i, 0)),
        out_shape=jax.ShapeDtypeStruct((b, dilation, s // dilation, QKV_GROUP), BF16),
        compiler_params=_cparams(2),
        name=f"qkv_rope_d{dilation}",
    )(*args)


def _hgrn_in_kernel(x_ref, g_ref, w_ref, lb_ref, act_ref, logf_ref, *, tn, layer):
    h = _rms(x_ref[...], g_ref[...], NORM_EPS).astype(BF16)
    logits = lb_ref[...]
    e = jnp.exp(logits - jnp.max(logits, axis=0, keepdims=True))
    p = e / jnp.sum(e, axis=0, keepdims=True)
    cum = p[0:1]
    for r in range(1, layer + 1):
        cum = cum + p[r:r + 1]
    lb = cum - p[0:1]
    log_lb = jnp.log(lb)
    log_1m_lb = jnp.log1p(-lb)

    for cn in range(5 * D_MODEL // tn):
        col = cn * tn
        part, off = col // D_MODEL, col % D_MODEL
        acc = jnp.dot(h, w_ref[:, col:col + tn], preferred_element_type=F32)
        if part == 0:
            act_ref[:, off:off + tn] = (acc * jax.nn.sigmoid(acc) * F32(C_DIM ** -0.5)).astype(act_ref.dtype)
        elif part in (1, 2):
            log_sig = jnp.minimum(acc, 0.0) - _softplus_neg_abs(acc)
            lo = (part - 1) * D_MODEL + off
            logf_ref[:, lo:lo + tn] = _logaddexp(log_lb[:, off:off + tn], log_1m_lb[:, off:off + tn] + log_sig)
        elif part == 3:
            act_ref[:, D_MODEL + off:D_MODEL + off + tn] = acc.astype(act_ref.dtype)
        else:
            act_ref[:, 2 * D_MODEL + off:2 * D_MODEL + off + tn] = (acc * jax.nn.sigmoid(acc)).astype(act_ref.dtype)


def _hgrn_in(x, g, w, lb_logits, layer, *, tm=512, tn=512):
    m, d = x.shape
    return pl.pallas_call(
        functools.partial(_hgrn_in_kernel, tn=tn, layer=layer),
        grid=(m // tm,),
        in_specs=[pl.BlockSpec((tm, d), lambda i: (i, 0)),
                  _const_spec((1, d), lambda i: (0, 0)),
                  _const_spec((d, 5 * D_MODEL), lambda i: (0, 0)),
                  _const_spec((DEPTH, D_MODEL), lambda i: (0, 0))],
        out_specs=[pl.BlockSpec((tm, 3 * D_MODEL), lambda i: (i, 0)),
                   pl.BlockSpec((tm, 2 * D_MODEL), lambda i: (i, 0))],
        out_shape=[jax.ShapeDtypeStruct((m, 3 * D_MODEL), BF16),
                   jax.ShapeDtypeStruct((m, 2 * D_MODEL), F32)],
        compiler_params=_cparams(1),
        name="hgrn_in",
    )(x, g.reshape(1, d), w, lb_logits)


def _mm_norm_res_kernel(a_ref, w_ref, g_ref, x_ref, o_ref, *, sub):
    for rb in range(a_ref.shape[0] // sub):
        rows = slice(rb * sub, (rb + 1) * sub)
        h = jnp.dot(a_ref[rows, :], w_ref[...], preferred_element_type=F32)
        o_ref[rows, :] = x_ref[rows, :] + _rms(h, g_ref[...], NORM_EPS)


def _mm_norm_res(a, w, g, x, *, tm=1024, sub=256):
    m, k = a.shape
    d = w.shape[1]
    return pl.pallas_call(
        functools.partial(_mm_norm_res_kernel, sub=sub),
        grid=(m // tm,),
        in_specs=[pl.BlockSpec((tm, k), lambda i: (i, 0)),
                  _const_spec((k, d), lambda i: (0, 0)),
                  _const_spec((1, d), lambda i: (0, 0)),
                  pl.BlockSpec((tm, d), lambda i: (i, 0))],
        out_specs=pl.BlockSpec((tm, d), lambda i: (i, 0)),
        out_shape=jax.ShapeDtypeStruct((m, d), F32),
        compiler_params=_cparams(1),
        name="mm_norm_res",
    )(a, w, g.reshape(1, d), x)


def _ffn_in_kernel(x_ref, xp_ref, xn_ref, g_ref, w_ref, cw_ref, cb_ref, o_ref, h_scr, u_scr, *, tm, tn, seq):
    i = pl.program_id(0)
    halo = BF16_ROWS
    rows = tm + 2 * halo

    g = g_ref[...]
    first = (i * tm) % seq == 0
    last = ((i + 1) * tm) % seq == 0
    hp = _rms(xp_ref[...], g, NORM_EPS)
    hn = _rms(xn_ref[...], g, NORM_EPS)
    h_scr[0:halo] = jnp.where(first, 0.0, hp).astype(BF16)
    h_scr[halo:halo + tm] = _rms(x_ref[...], g, NORM_EPS).astype(BF16)
    h_scr[halo + tm:rows] = jnp.where(last, 0.0, hn).astype(BF16)

    def lanes(cn, base=0):
        col = cn * tn if isinstance(cn, int) else pl.multiple_of(cn * tn, tn)
        return pl.ds(base + col, tn)

    def matmuls(cn, slot):
        u_scr[2 * slot] = jnp.dot(h_scr[...], w_ref[:, lanes(cn)], preferred_element_type=F32)
        u_scr[2 * slot + 1] = jnp.dot(h_scr[...], w_ref[:, lanes(cn, D_FF)], preferred_element_type=F32)

    def conv(u, cols):
        cw = cw_ref[:, cols]
        u_prev = pltpu.roll(u, 1, 0)
        u_next = pltpu.roll(u, rows - 1, 0)
        acc = cb_ref[:, cols] + u_prev * cw[0:1] + u * cw[1:2] + u_next * cw[2:3]
        return acc[halo:halo + tm]

    def epilogue(cn, slot):
        a = conv(u_scr[2 * slot], lanes(cn))
        b = conv(u_scr[2 * slot + 1], lanes(cn, D_FF))
        o_ref[:, lanes(cn)] = (jax.nn.gelu(a, approximate=True) * b).astype(o_ref.dtype)

    n_chunks = D_FF // tn
    assert n_chunks % 2 == 1

    def body(it, carry):
        cn = 2 * it
        matmuls(cn + 1, 1)
        epilogue(cn, 0)
        matmuls(cn + 2, 0)
        epilogue(cn + 1, 1)
        return carry

    matmuls(0, 0)
    lax.fori_loop(0, n_chunks // 2, body, 0)
    epilogue(n_chunks - 1, 0)


def _ffn_in(x, g, w, conv_w, conv_b, seq, *, tm=512, tn=256):
    m, d = x.shape
    halo = BF16_ROWS
    hb = tm // halo
    last_blk = m // halo - 1
    return pl.pallas_call(
        functools.partial(_ffn_in_kernel, tm=tm, tn=tn, seq=seq),
        grid=(m // tm,),
        in_specs=[pl.BlockSpec((tm, d), lambda i: (i, 0)),
                  pl.BlockSpec((halo, d), lambda i: (jnp.maximum(i * hb - 1, 0), 0)),
                  pl.BlockSpec((halo, d), lambda i: (jnp.minimum((i + 1) * hb, last_blk), 0)),
                  _const_spec((1, d), lambda i: (0, 0)),
                  _const_spec((d, 2 * D_FF), lambda i: (0, 0)),
                  _const_spec((3, 2 * D_FF), lambda i: (0, 0)),
                  _const_spec((1, 2 * D_FF), lambda i: (0, 0))],
        out_specs=pl.BlockSpec((tm, D_FF), lambda i: (i, 0)),
        out_shape=jax.ShapeDtypeStruct((m, D_FF), BF16),
        scratch_shapes=[pltpu.VMEM((tm + 2 * halo, d), BF16), pltpu.VMEM((4, tm + 2 * halo, tn), F32)],
        compiler_params=_cparams(1),
        name="ffn_in",
    )(x, x, x, g.reshape(1, d), w, conv_w, conv_b.reshape(1, -1))


def _diff_attn_kernel(q_ref, k_ref, v_ref, lam_ref, g_ref, o_ref, q2_scr, vx_scr, m_scr, acc_scr, s_scr,
                      *, tq, tc, lam_init):
    seq = k_ref.shape[0]
    q = q_ref[...]
    lane = lax.broadcasted_iota(jnp.int32, q.shape, 1)
    zero = jnp.zeros_like(q)
    q2_scr[0:tq] = jnp.where(lane < A_HEAD_DIM, q, zero)
    q2_scr[tq:2 * tq] = jnp.where(lane >= A_HEAD_DIM, q, zero)
    m_scr[...] = jnp.full(m_scr.shape, -jnp.inf, F32)
    acc_scr[...] = jnp.zeros(acc_scr.shape, F32)

    @pl.when(pl.program_id(2) == 0)
    def _():
        vx_scr[:, 0:A_V_DIM] = v_ref[...]
        vx_scr[:, A_V_DIM:2 * A_V_DIM] = jnp.ones((seq, A_V_DIM), BF16)

    def scores(c, slot):
        r0 = c * tc if isinstance(c, int) else pl.multiple_of(c * tc, tc)
        s_scr[slot] = lax.dot_general(q2_scr[...], k_ref[pl.ds(r0, tc), :], (((1,), (1,)), ((), ())),
                                      preferred_element_type=F32)

    def softmax_pv(c, slot):
        r0 = c * tc if isinstance(c, int) else pl.multiple_of(c * tc, tc)
        slabs = [s_scr[slot, :, j * LANES:(j + 1) * LANES] for j in range(tc // LANES)]
        mx = slabs[0]
        for sl in slabs[1:]:
            mx = jnp.maximum(mx, sl)
        m_prev = m_scr[...]
        m_new = jnp.maximum(m_prev, jnp.max(mx, axis=1, keepdims=True))
        alpha = jnp.exp2(m_prev - m_new)
        p = jnp.concatenate([jnp.exp2(sl - m_new).astype(BF16) for sl in slabs], axis=1)
        pv = jnp.dot(p, vx_scr[pl.ds(r0, tc), :], preferred_element_type=F32)
        acc_scr[:, 0:A_V_DIM] = alpha * acc_scr[:, 0:A_V_DIM] + pv[:, 0:A_V_DIM]
        acc_scr[:, A_V_DIM:2 * A_V_DIM] = alpha * acc_scr[:, A_V_DIM:2 * A_V_DIM] + pv[:, A_V_DIM:2 * A_V_DIM]
        m_scr[...] = m_new

    n_chunks = seq // tc

    def body(it, carry):
        c = 2 * it
        scores(c + 1, 1)
        softmax_pv(c, 0)
        scores(c + 2, 0)
        softmax_pv(c + 1, 1)
        return carry

    scores(0, 0)
    lax.fori_loop(0, n_chunks // 2 - 1, body, 0)
    scores(n_chunks - 1, 1)
    softmax_pv(n_chunks - 2, 0)
    softmax_pv(n_chunks - 1, 1)

    o = acc_scr[:, 0:A_V_DIM] / acc_scr[:, A_V_DIM:2 * A_V_DIM]
    lv = lam_ref[...]
    lam = (jnp.exp(jnp.sum(lv[0:1] * lv[1:2], axis=1, keepdims=True))
           - jnp.exp(jnp.sum(lv[2:3] * lv[3:4], axis=1, keepdims=True)) + F32(lam_init))
    od = o[0:tq] - lam * o[tq:2 * tq]
    y = _rms(od, g_ref[...], A_SUBLN_EPS) * F32(1.0 - lam_init)
    o_ref[...] = y.astype(o_ref.dtype)


def _diff_attn(qkv, lam_vecs, subln_g, lam_init, *, tq=1024, tc=512):
    b, s, _ = qkv.shape
    tq = min(tq, s)
    blk = lambda rows, off: pl.BlockSpec((None, rows, LANES), off)
    return pl.pallas_call(
        functools.partial(_diff_attn_kernel, tq=tq, tc=tc, lam_init=lam_init),
        grid=(b, A_HEADS, s // tq),
        in_specs=[blk(tq, lambda bi, h, i: (bi, i, h)),
                  blk(s, lambda bi, h, i: (bi, 0, A_HEADS + h)),
                  blk(s, lambda bi, h, i: (bi, 0, 2 * A_HEADS + h)),
                  pl.BlockSpec((4, A_HEAD_DIM), lambda bi, h, i: (0, 0)),
                  pl.BlockSpec((1, A_V_DIM), lambda bi, h, i: (0, 0))],
        out_specs=blk(tq, lambda bi, h, i: (bi, i, h)),
        out_shape=jax.ShapeDtypeStruct((b, s, A_HEADS * A_V_DIM), BF16),
        scratch_shapes=[pltpu.VMEM((2 * tq, LANES), BF16),
                        pltpu.VMEM((s, 2 * A_V_DIM), BF16),
                        pltpu.VMEM((2 * tq, LANES), F32),
                        pltpu.VMEM((2 * tq, 2 * A_V_DIM), F32),
                        pltpu.VMEM((2, 2 * tq, tc), F32)],
        compiler_params=_cparams(3),
        name="diff_attn",
    )(qkv, qkv, qkv, lam_vecs, subln_g.reshape(1, A_V_DIM))


def _dil_attn_kernel(q_ref, k_ref, kp_ref, kn_ref, v_ref, vp_ref, vn_ref, o_ref, lse_ref, kbuf, vbuf,
                     *, tl, length):
    hw = B_HALF_W
    t0 = pl.program_id(2) * tl

    for buf, main, prev, nxt in ((kbuf, k_ref, kp_ref, kn_ref), (vbuf, v_ref, vp_ref, vn_ref)):
        buf[0:hw] = prev[...]
        buf[hw:hw + tl] = main[...]
        buf[hw + tl:2 * hw + tl] = nxt[...]

    a_io = lax.broadcasted_iota(jnp.int32, (B_QBLK, B_KBLK), 0)
    c_io = lax.broadcasted_iota(jnp.int32, (B_QBLK, B_KBLK), 1)
    band = (c_io >= a_io) & (c_io <= a_io + 2 * hw)
    lane = lax.broadcasted_iota(jnp.int32, (B_QBLK, LANES), 1)

    def q_block(jb, carry):
        r0 = pl.multiple_of(jb * B_QBLK, B_QBLK)
        ki = t0 + r0 - hw + c_io
        valid = band & (ki >= 0) & (ki < length)
        lse_tile = jnp.zeros((B_QBLK, LANES), F32)
        for h in range(B_HEADS):
            cols = slice(h * B_HEAD_DIM, (h + 1) * B_HEAD_DIM)
            qh = q_ref[pl.ds(r0, B_QBLK), cols]
            kh = kbuf[pl.ds(r0, B_KBLK), cols]
            vh = vbuf[pl.ds(r0, B_KBLK), cols]
            s = lax.dot_general(qh, kh, (((1,), (1,)), ((), ())), preferred_element_type=F32)
            s = jnp.where(valid, s, -jnp.inf)
            m = jnp.max(s, axis=1, keepdims=True)
            p = jnp.exp(s - m)
            l = jnp.sum(p, axis=1, keepdims=True)
            o = jnp.dot(p.astype(BF16), vh, preferred_element_type=F32) / l
            o_ref[pl.ds(r0, B_QBLK), cols] = o.astype(o_ref.dtype)
            lse_tile = jnp.where(lane == h, m + jnp.log(l), lse_tile)
        lse_ref[pl.ds(r0, B_QBLK), :] = lse_tile
        return carry

    lax.fori_loop(0, tl // B_QBLK, q_block, 0)


def _dil_attn_group(qkv, *, tl=512):
    b, dilation, length, _ = qkv.shape
    tl = min(tl, length)
    hw = B_HALF_W
    width = B_HEADS * B_HEAD_DIM
    per_tile = tl // hw
    n_hw = length // hw
    main = lambda part: pl.BlockSpec((None, None, tl, width), lambda bi, r, i: (bi, r, i, part))
    before = lambda part: pl.BlockSpec((None, None, hw, width),
                                       lambda bi, r, i: (bi, r, jnp.maximum(i * per_tile - 1, 0), part))
    after = lambda part: pl.BlockSpec((None, None, hw, width),
                                      lambda bi, r, i: (bi, r, jnp.minimum((i + 1) * per_tile, n_hw - 1), part))
    return pl.pallas_call(
        functools.partial(_dil_attn_kernel, tl=tl, length=length),
        grid=(b, dilation, length // tl),
        in_specs=[main(0), main(1), before(1), after(1), main(2), before(2), after(2)],
        out_specs=[pl.BlockSpec((None, None, tl, width), lambda bi, r, i: (bi, r, i, 0)),
                   pl.BlockSpec((None, None, tl, LANES), lambda bi, r, i: (bi, r, i, 0))],
        out_shape=[jax.ShapeDtypeStruct((b, dilation, length, width), BF16),
                   jax.ShapeDtypeStruct((b, dilation, length, LANES), F32)],
        scratch_shapes=[pltpu.VMEM((tl + 2 * hw, width), BF16), pltpu.VMEM((tl + 2 * hw, width), BF16)],
        compiler_params=_cparams(3),
        name=f"dil_attn_d{dilation}",
    )(*([qkv] * 7))


def _split3(x):
    hi = x.astype(BF16)
    r1 = x - hi.astype(F32)
    mid = r1.astype(BF16)
    lo = (r1 - mid.astype(F32)).astype(BF16)
    return hi, mid, lo


def _dil_merge_kernel(*refs, n_groups):
    o_refs = refs[:n_groups]
    l_refs = refs[n_groups:2 * n_groups]
    perm_refs = refs[2 * n_groups:3 * n_groups - 1]
    expand_ref, out_ref = refs[3 * n_groups - 1:]
    t = out_ref.shape[0]
    outs, lses = [], []
    for gi in range(n_groups):
        o = o_refs[gi][...].reshape(t, o_refs[gi].shape[-1])
        lse = l_refs[gi][...].reshape(t, LANES)
        if gi == 0:
            outs.append(o.astype(F32))
            lses.append(lse)
        else:
            perm = perm_refs[gi - 1][...]
            outs.append(jnp.dot(perm, o, preferred_element_type=F32))
            lses.append(sum(jnp.dot(perm, piece, preferred_element_type=F32) for piece in _split3(lse)))
    m = functools.reduce(jnp.maximum, lses)
    es = [jnp.exp(l - m) for l in lses]
    den = functools.reduce(lambda a, b: a + b, es)
    acc = jnp.zeros((t, out_ref.shape[1]), F32)
    for gi in range(n_groups):
        w = es[gi] / den
        hi = w.astype(BF16)
        lo = (w - hi.astype(F32)).astype(BF16)
        wide = (jnp.dot(hi, expand_ref[...], preferred_element_type=F32)
                + jnp.dot(lo, expand_ref[...], preferred_element_type=F32))
        acc = acc + wide * outs[gi]
    out_ref[...] = acc.astype(out_ref.dtype)


def _dil_merge(outs, lses, *, t=512):
    b, _, s, width = outs[0].shape
    n_groups = len(outs)
    in_specs, args = [], []
    for arrs, lanes in ((outs, width), (lses, LANES)):
        for a in arrs:
            d = a.shape[1]
            in_specs.append(pl.BlockSpec((None, d, t // d, lanes), lambda bi, i: (bi, 0, i, 0)))
            args.append(a)
    for a in outs[1:]:
        in_specs.append(pl.BlockSpec((t, t), lambda bi, i: (0, 0)))
        args.append(jnp.asarray(_residue_perm(t, a.shape[1]).T, BF16))
    expand = np.zeros((LANES, width), np.float32)
    expand[np.arange(width) // B_HEAD_DIM, np.arange(width)] = 1.0
    in_specs.append(pl.BlockSpec((LANES, width), lambda bi, i: (0, 0)))
    args.append(jnp.asarray(expand, BF16))
    return pl.pallas_call(
        functools.partial(_dil_merge_kernel, n_groups=n_groups),
        grid=(b, s // t),
        in_specs=in_specs,
        out_specs=pl.BlockSpec((None, t, width), lambda bi, i: (bi, i, 0)),
        out_shape=jax.ShapeDtypeStruct((b, s, width), BF16),
        compiler_params=_cparams(2),
        name="dil_merge",
    )(*args)


def _scan_constants(reverse):
    c = C_CHUNK
    t = np.arange(c)[:, None]
    u = np.arange(c)[None, :]
    if not reverse:
        mats = [u <= t, u > t]
    else:
        mats = [u >= t, u < t]
    masks = []
    for hs in C_LEVELS:
        base = (t // (2 * hs)) * (2 * hs)
        mid = base + hs - 1
        same = (u // (2 * hs)) == (t // (2 * hs))
        if not reverse:
            m = np.where(t > mid, (u > mid) & (u <= t), (u > t) & (u <= mid))
            masks.append(same & (t > mid) & (u <= mid))
        else:
            m = np.where(t <= mid, (u >= t) & (u <= mid), (u > mid) & (u < t))
            masks.append(same & (t <= mid) & (u > mid))
        mats.append(m & same)
    stack = np.concatenate([np.asarray(m, np.float32) for m in mats], axis=0)
    return jnp.asarray(stack, BF16), jnp.asarray(np.stack(masks).astype(np.float32))


def _hgrn_scan_kernel(*refs, tt, hp, reverse, final):
    q_ref, v_ref, lf_ref, mstack_ref, masks_ref = refs[:5]
    refs = refs[5:]
    if final:
        ofw_ref, gate_ref, gn_ref = refs[:3]
        refs = refs[3:]
    o_ref, st_scr = refs
    c = C_CHUNK
    n_chunks = tt // c
    nt_contract = (((1,), (1,)), ((), ()))

    @pl.when(pl.program_id(2) == 0)
    def _():
        st_scr[...] = jnp.zeros(st_scr.shape, F32)

    row8 = lax.broadcasted_iota(jnp.int32, (C_DIAG, C_DIM), 0)
    lane_c = lax.broadcasted_iota(jnp.int32, (C_DIAG, c), 1)

    def head_chunk(r0, hi):
        cols = slice(hi * C_DIM, (hi + 1) * C_DIM)
        lf = lf_ref[pl.ds(r0, c), cols]
        top = lf.astype(BF16)
        low = (lf - top.astype(F32)).astype(BF16)
        d2 = jnp.dot(mstack_ref[...], jnp.concatenate([top, low], axis=1), preferred_element_type=F32)
        d = d2[:, 0:C_DIM] + d2[:, C_DIM:2 * C_DIM]
        gq = d[0:c]
        gk = d[c:2 * c]

        q = q_ref[pl.ds(r0, c), cols].astype(F32)
        v = v_ref[pl.ds(r0, c), cols]
        kk = 1.0 - jnp.exp(lf)
        st = st_scr[hi]

        o = lax.dot_general((q * jnp.exp(gq)).astype(BF16), st.astype(BF16), nt_contract,
                            preferred_element_type=F32)

        a = jnp.zeros((c, c), F32)
        for li in range(len(C_LEVELS)):
            e = jnp.exp(-jnp.abs(d[(2 + li) * c:(3 + li) * c]))
            pq = (q * e).astype(BF16)
            pk = (kk * e).astype(BF16)
            a = a + lax.dot_general(pq, pk, nt_contract, preferred_element_type=F32) * masks_ref[li]

        blocks = []
        for bi in range(c // C_DIAG):
            rows = slice(bi * C_DIAG, (bi + 1) * C_DIAG)
            gb, qb, kb = gq[rows], q[rows], kk[rows]
            ablk = jnp.zeros((C_DIAG, c), F32)
            for si in range(C_DIAG):
                keep = (row8 <= si) if reverse else (row8 >= si)
                dec = jnp.exp(jnp.where(keep, gb - gb[si:si + 1], -jnp.inf))
                sc = jnp.sum(qb * dec * kb[si:si + 1], axis=1, keepdims=True)
                ablk = jnp.where(lane_c == bi * C_DIAG + si, sc, ablk)
            blocks.append(ablk)
        a = a + jnp.concatenate(blocks, axis=0)

        o = o + jnp.dot(a.astype(BF16), v, preferred_element_type=F32)

        kd = (kk * jnp.exp(gk)).astype(BF16)
        upd = lax.dot_general(v, kd, (((0,), (0,)), ((), ())), preferred_element_type=F32)
        g_tot = gq[0:1] if reverse else gq[c - 1:c]
        st_scr[hi] = st * jnp.exp(g_tot) + upd

        if final:
            tot = ofw_ref[pl.ds(r0, c), cols] + o
            y = _rms(tot, gn_ref[...], NORM_EPS) * gate_ref[pl.ds(r0, c), cols].astype(F32)
            o_ref[pl.ds(r0, c), cols] = y.astype(o_ref.dtype)
        else:
            o_ref[pl.ds(r0, c), cols] = o

    def chunk(ci, carry):
        cidx = (n_chunks - 1 - ci) if reverse else ci
        r0 = pl.multiple_of(cidx * c, c)
        for hi in range(hp):
            head_chunk(r0, hi)
        return carry

    lax.fori_loop(0, n_chunks, chunk, 0)


def _hgrn_scan(act, logf, reverse, o_fw=None, gnorm_g=None, *, tt=512, hp=8):
    b, s, _ = act.shape
    tt = min(tt, s)
    nt = s // tt
    final = o_fw is not None
    mstack, masks = _scan_constants(reverse)
    groups = C_HEADS // hp
    tmap = (lambda t: nt - 1 - t) if reverse else (lambda t: t)
    blk = lambda part: pl.BlockSpec((None, tt, hp * C_DIM), lambda bi, hg, t: (bi, tmap(t), part * groups + hg))
    in_specs = [blk(0), blk(1), blk(1 if reverse else 0),
                _const_spec(mstack.shape, lambda bi, hg, t: (0, 0)),
                _const_spec(masks.shape, lambda bi, hg, t: (0, 0, 0))]
    args = [act, act, logf, mstack, masks]
    if final:
        in_specs += [blk(0), blk(2), _const_spec((1, C_DIM), lambda bi, hg, t: (0, 0))]
        args += [o_fw, act, gnorm_g.reshape(1, C_DIM)]
    return pl.pallas_call(
        functools.partial(_hgrn_scan_kernel, tt=tt, hp=hp, reverse=reverse, final=final),
        grid=(b, groups, nt),
        in_specs=in_specs,
        out_specs=blk(0),
        out_shape=jax.ShapeDtypeStruct((b, s, C_HEADS * C_DIM), BF16 if final else F32),
        scratch_shapes=[pltpu.VMEM((hp, C_DIM, C_DIM), F32)],
        compiler_params=_cparams(3),
        name="hgrn_scan_bw" if reverse else "hgrn_scan_fw",
    )(*args)


def _diff_lambda_init(layer):
    return 0.8 - 0.6 * math.exp(-0.3 * layer)


def _trunk(x3, p):
    b, s, d = x3.shape
    x = x3.reshape(b * s, d)
    tables = _rope_tables(s)
    for layer in range(DEPTH):
        kind = layer % N_MIXERS
        j = layer // N_MIXERS
        g = p["norm_g"][layer]
        x3 = x.reshape(b, s, d)
        if kind == 0:
            qkv = _qkv_rope(x3, g[0], p["a_w_qkv"][j], tables, A_HEAD_DIM ** -0.5 * LOG2E)
            o = _diff_attn(qkv.reshape(b, s, -1), p["a_lambda"][j], p["a_subln_g"][j], _diff_lambda_init(layer))
            w_o = p["a_w_o"][j]
        elif kind == 1:
            outs, lses = [], []
            for group, (_, dilation) in enumerate(B_PATTERNS):
                qkv = _qkv_rope(x3, g[0], p["b_w_qkv"][j], tables, B_HEAD_DIM ** -0.5, group, dilation)
                o_g, lse_g = _dil_attn_group(qkv)
                outs.append(o_g)
                lses.append(lse_g)
            o = _dil_merge(outs, lses)
            w_o = p["b_w_o"][j]
        else:
            act, logf = _hgrn_in(x, g[0], p["c_w_in"][j], p["c_lb_logits"], layer)
            act, logf = act.reshape(b, s, -1), logf.reshape(b, s, -1)
            o_fw = _hgrn_scan(act, logf, False)
            o = _hgrn_scan(act, logf, True, o_fw, p["c_gnorm_g"][j])
            w_o = p["c_w_o"][j]
        x = _mm_norm_res(o.reshape(b * s, -1), w_o, g[1], x)
        u = _ffn_in(x, g[2], p["f_w_in"][layer], p["f_conv_w"][layer], p["f_conv_b"][layer], s)
        x = _mm_norm_res(u, p["f_w_out"][layer], g[3], x)
    return x.reshape(b, s, d)


def kernel(x_prompt, x_sample, norm_g, a_w_qkv, a_lambda, a_subln_g, a_w_o, b_w_qkv, b_w_o, c_w_in,
           c_lb_logits, c_gnorm_g, c_w_o, f_w_in, f_conv_w, f_conv_b, f_w_out):
    bf = lambda w: w.astype(BF16)
    p = dict(norm_g=norm_g, a_w_qkv=bf(a_w_qkv), a_lambda=a_lambda, a_subln_g=a_subln_g, a_w_o=bf(a_w_o),
             b_w_qkv=bf(b_w_qkv), b_w_o=bf(b_w_o), c_w_in=bf(c_w_in), c_lb_logits=c_lb_logits,
             c_gnorm_g=c_gnorm_g, c_w_o=bf(c_w_o), f_w_in=bf(f_w_in), f_conv_w=f_conv_w,
             f_conv_b=f_conv_b, f_w_out=bf(f_w_out))
    return _trunk(x_prompt, p), _trunk(x_sample, p)
```

```python
import functools
import math

import numpy as np
import jax
import jax.numpy as jnp
from jax import lax
from jax.experimental import pallas as pl
from jax.experimental.pallas import tpu as pltpu

F32 = jnp.float32
BF16 = jnp.bfloat16

D_MODEL = 1024
DEPTH = 4
N_MIXERS = 3
ROPE_THETA = 500000.0
NORM_EPS = 1e-6

A_HEADS = 8
A_HEAD_DIM = 64
A_V_DIM = 128
A_SUBLN_EPS = 1e-5

B_HEADS = 16
B_HEAD_DIM = 64
B_PATTERNS = ((128, 1), (512, 4), (2048, 16))
B_HALF_W = 64
B_QBLK = 2 * B_HALF_W
B_KBLK = 4 * B_HALF_W

C_HEADS = 8
C_DIM = 128
C_CHUNK = 128
C_LEVELS = (64, 32, 16, 8)
C_DIAG = 8

D_FF = 2816
QKV_GROUP = 3 * D_MODEL
ROPE_ROT = 16
LANES = 128
LOG2E = math.log2(math.e)
BF16_ROWS = 16
SUBLANES = 8

V7X_VMEM_LIMIT = 56 * 1024 * 1024


def _cparams(n_axes):
    return pltpu.CompilerParams(dimension_semantics=("arbitrary",) * n_axes,
                                vmem_limit_bytes=V7X_VMEM_LIMIT)


def _const_spec(shape, index_map):
    return pl.BlockSpec(shape, index_map, pipeline_mode=pl.Buffered(1))


def _rms(x, g, eps):
    ms = jnp.mean(x * x, axis=-1, keepdims=True)
    return x * lax.rsqrt(ms + eps) * g


def _softplus_neg_abs(d):
    return jnp.log(1.0 + jnp.exp(-jnp.abs(d)))


def _logaddexp(a, b):
    return jnp.maximum(a, b) + _softplus_neg_abs(a - b)


def _rope_tables(seq):
    half = ROPE_ROT // 2
    inv_freq = ROPE_THETA ** (-jnp.arange(half, dtype=F32) / half)
    ang = jnp.arange(seq, dtype=F32)[:, None] * inv_freq[None, :]
    cos, sin = jnp.cos(ang), jnp.sin(ang)
    rest = B_HEAD_DIM - ROPE_ROT
    c = jnp.concatenate([cos, cos, jnp.ones((seq, rest), F32)], axis=1)
    s1 = jnp.concatenate([-sin, jnp.zeros((seq, rest + half), F32)], axis=1)
    s2 = jnp.concatenate([jnp.zeros((seq, half), F32), sin, jnp.zeros((seq, rest), F32)], axis=1)
    rep = LANES // B_HEAD_DIM
    return tuple(jnp.tile(t, (1, rep)) for t in (c, s1, s2))


def _residue_perm(tm, dilation):
    rows = tm // dilation
    rho = np.arange(tm)
    tok = (rho % rows) * dilation + rho // rows
    perm = np.zeros((tm, tm), np.float32)
    perm[rho, tok] = 1.0
    return perm


def _qkv_rope_kernel(*refs, tn, q_scale, dilation):
    if dilation > 1:
        x_ref, g_ref, w_ref, c_ref, s1_ref, s2_ref, perm_ref, o_ref = refs
    else:
        x_ref, g_ref, w_ref, c_ref, s1_ref, s2_ref, o_ref = refs
    tm = x_ref.shape[0]
    rows = tm // dilation

    h = _rms(x_ref[...], g_ref[...], NORM_EPS).astype(BF16)
    if dilation > 1:
        h = jnp.dot(perm_ref[...], h, preferred_element_type=F32).astype(BF16)

    def put(col, val):
        for rr in range(dilation):
            o_ref[rr, :, col:col + LANES] = val[rr * rows:(rr + 1) * rows].astype(o_ref.dtype)

    c, s1, s2 = c_ref[...], s1_ref[...], s2_ref[...]
    cq, s1q, s2q = c * F32(q_scale), s1 * F32(q_scale), s2 * F32(q_scale)
    for cn in range(QKV_GROUP // tn):
        acc = jnp.dot(h, w_ref[:, cn * tn:(cn + 1) * tn], preferred_element_type=F32)
        for cb in range(tn // LANES):
            col = cn * tn + cb * LANES
            a = acc[:, cb * LANES:(cb + 1) * LANES]
            if col < 2 * D_MODEL:
                tc, t1, t2 = (cq, s1q, s2q) if col < D_MODEL else (c, s1, s2)
                a = a * tc + pltpu.roll(a, LANES - ROPE_ROT // 2, 1) * t1 + pltpu.roll(a, ROPE_ROT // 2, 1) * t2
            put(col, a)


def _qkv_rope(x, g, w, tables, q_scale, group=0, dilation=1, *, tm=512, tn=512):
    b, s, d = x.shape
    rows = tm // dilation
    if dilation > 1:
        tables = tuple(t.reshape(s // tm, rows, dilation, LANES).transpose(0, 2, 1, 3).reshape(s, LANES)
                       for t in tables)
    tab_spec = pl.BlockSpec((tm, LANES), lambda bi, i: (i, 0))
    in_specs = [pl.BlockSpec((None, tm, d), lambda bi, i: (bi, i, 0)),
                _const_spec((1, d), lambda bi, i: (0, 0)),
                _const_spec((d, QKV_GROUP), lambda bi, i: (0, group)),
                tab_spec, tab_spec, tab_spec]
    args = [x, g.reshape(1, d), w, *tables]
    if dilation > 1:
        in_specs.append(_const_spec((tm, tm), lambda bi, i: (0, 0)))
        args.append(jnp.asarray(_residue_perm(tm, dilation), BF16))
    return pl.pallas_call(
        functools.partial(_qkv_rope_kernel, tn=tn, q_scale=q_scale, dilation=dilation),
        grid=(b, s // tm),
        in_specs=in_specs,
        out_specs=pl.BlockSpec((None, dilation, rows, QKV_GROUP), lambda bi, i: (bi, 0, i, 0)),
        out_shape=jax.ShapeDtypeStruct((b, dilation, s // dilation, QKV_GROUP), BF16),
        compiler_params=_cparams(2),
        name=f"qkv_rope_d{dilation}",
    )(*args)


def _hgrn_in_kernel(x_ref, g_ref, w_ref, lb_ref, act_ref, logf_ref, *, tn, layer):
    h = _rms(x_ref[...], g_ref[...], NORM_EPS).astype(BF16)
    logits = lb_ref[...]
    e = jnp.exp(logits - jnp.max(logits, axis=0, keepdims=True))
    p = e / jnp.sum(e, axis=0, keepdims=True)
    cum = p[0:1]
    for r in range(1, layer + 1):
        cum = cum + p[r:r + 1]
    lb = cum - p[0:1]
    log_lb = jnp.log(lb)
    log_1m_lb = jnp.log1p(-lb)

    for cn in range(5 * D_MODEL // tn):
        col = cn * tn
        part, off = col // D_MODEL, col % D_MODEL
        acc = jnp.dot(h, w_ref[:, col:col + tn], preferred_element_type=F32)
        if part == 0:
            act_ref[:, off:off + tn] = (acc * jax.nn.sigmoid(acc) * F32(C_DIM ** -0.5)).astype(act_ref.dtype)
        elif part in (1, 2):
            log_sig = jnp.minimum(acc, 0.0) - _softplus_neg_abs(acc)
            lo = (part - 1) * D_MODEL + off
            logf_ref[:, lo:lo + tn] = _logaddexp(log_lb[:, off:off + tn], log_1m_lb[:, off:off + tn] + log_sig)
        elif part == 3:
            act_ref[:, D_MODEL + off:D_MODEL + off + tn] = acc.astype(act_ref.dtype)
        else:
            act_ref[:, 2 * D_MODEL + off:2 * D_MODEL + off + tn] = (acc * jax.nn.sigmoid(acc)).astype(act_ref.dtype)


def _hgrn_in(x, g, w, lb_logits, layer, *, tm=512, tn=512):
    m, d = x.shape
    return pl.pallas_call(
        functools.partial(_hgrn_in_kernel, tn=tn, layer=layer),
        grid=(m // tm,),
        in_specs=[pl.BlockSpec((tm, d), lambda i: (i, 0)),
                  _const_spec((1, d), lambda i: (0, 0)),
                  _const_spec((d, 5 * D_MODEL), lambda i: (0, 0)),
                  _const_spec((DEPTH, D_MODEL), lambda i: (0, 0))],
        out_specs=[pl.BlockSpec((tm, 3 * D_MODEL), lambda i: (i, 0)),
                   pl.BlockSpec((tm, 2 * D_MODEL), lambda i: (i, 0))],
        out_shape=[jax.ShapeDtypeStruct((m, 3 * D_MODEL), BF16),
                   jax.ShapeDtypeStruct((m, 2 * D_MODEL), F32)],
        compiler_params=_cparams(1),
        name="hgrn_in",
    )(x, g.reshape(1, d), w, lb_logits)


def _mm_norm_res_kernel(a_ref, w_ref, g_ref, x_ref, o_ref, *, sub):
    for rb in range(a_ref.shape[0] // sub):
        rows = slice(rb * sub, (rb + 1) * sub)
        h = jnp.dot(a_ref[rows, :], w_ref[...], preferred_element_type=F32)
        o_ref[rows, :] = x_ref[rows, :] + _rms(h, g_ref[...], NORM_EPS)


def _mm_norm_res(a, w, g, x, *, tm=1024, sub=256):
    m, k = a.shape
    d = w.shape[1]
    return pl.pallas_call(
        functools.partial(_mm_norm_res_kernel, sub=sub),
        grid=(m // tm,),
        in_specs=[pl.BlockSpec((tm, k), lambda i: (i, 0)),
                  _const_spec((k, d), lambda i: (0, 0)),
                  _const_spec((1, d), lambda i: (0, 0)),
                  pl.BlockSpec((tm, d), lambda i: (i, 0))],
        out_specs=pl.BlockSpec((tm, d), lambda i: (i, 0)),
        out_shape=jax.ShapeDtypeStruct((m, d), F32),
        compiler_params=_cparams(1),
        name="mm_norm_res",
    )(a, w, g.reshape(1, d), x)


def _ffn_in_kernel(x_ref, xp_ref, xn_ref, g_ref, w_ref, cw_ref, cb_ref, o_ref, h_scr, *, tm, tn, seq):
    i = pl.program_id(0)
    halo = BF16_ROWS
    rows = tm + 2 * halo

    g = g_ref[...]
    first = (i * tm) % seq == 0
    last = ((i + 1) * tm) % seq == 0
    hp = _rms(xp_ref[...], g, NORM_EPS)
    hn = _rms(xn_ref[...], g, NORM_EPS)
    h_scr[0:halo] = jnp.where(first, 0.0, hp).astype(BF16)
    h_scr[halo:halo + tm] = _rms(x_ref[...], g, NORM_EPS).astype(BF16)
    h_scr[halo + tm:rows] = jnp.where(last, 0.0, hn).astype(BF16)

    h = h_scr[...]

    groups = rows // SUBLANES
    sub = lax.broadcasted_iota(jnp.int32, (groups, SUBLANES, tn), 1)

    def conv(col, scale=None):
        u = jnp.dot(h, w_ref[:, col:col + tn], preferred_element_type=F32).reshape(groups, SUBLANES, tn)
        cw = cw_ref[:, col:col + tn]
        cb = cb_ref[:, col:col + tn]
        if scale is not None:
            cw, cb = cw * scale, cb * scale
        r_dn = pltpu.roll(u, 1, 1)
        r_up = pltpu.roll(u, SUBLANES - 1, 1)
        u_prev = jnp.where(sub == 0, jnp.concatenate([r_dn[-1:], r_dn[:-1]], axis=0), r_dn)
        u_next = jnp.where(sub == SUBLANES - 1, jnp.concatenate([r_up[1:], r_up[:1]], axis=0), r_up)
        acc = cb + u_prev * cw[0:1] + u * cw[1:2] + u_next * cw[2:3]
        return acc.reshape(rows, tn)[halo:halo + tm]

    k0 = math.sqrt(2.0 / math.pi)
    for cn in range(D_FF // tn):
        a = conv(cn * tn)
        half_b = conv(D_FF + cn * tn, F32(0.5))
        t = jnp.tanh(a * (F32(k0) + F32(k0 * 0.044715) * (a * a)))
        o_ref[:, cn * tn:(cn + 1) * tn] = ((a * half_b) * (1.0 + t)).astype(o_ref.dtype)


def _ffn_in(x, g, w, conv_w, conv_b, seq, *, tm=512, tn=256):
    m, d = x.shape
    halo = BF16_ROWS
    hb = tm // halo
    last_blk = m // halo - 1
    return pl.pallas_call(
        functools.partial(_ffn_in_kernel, tm=tm, tn=tn, seq=seq),
        grid=(m // tm,),
        in_specs=[pl.BlockSpec((tm, d), lambda i: (i, 0)),
                  pl.BlockSpec((halo, d), lambda i: (jnp.maximum(i * hb - 1, 0), 0)),
                  pl.BlockSpec((halo, d), lambda i: (jnp.minimum((i + 1) * hb, last_blk), 0)),
                  _const_spec((1, d), lambda i: (0, 0)),
                  _const_spec((d, 2 * D_FF), lambda i: (0, 0)),
                  _const_spec((3, 2 * D_FF), lambda i: (0, 0)),
                  _const_spec((1, 2 * D_FF), lambda i: (0, 0))],
        out_specs=pl.BlockSpec((tm, D_FF), lambda i: (i, 0)),
        out_shape=jax.ShapeDtypeStruct((m, D_FF), BF16),
        scratch_shapes=[pltpu.VMEM((tm + 2 * halo, d), BF16)],
        compiler_params=_cparams(1),
        name="ffn_in",
    )(x, x, x, g.reshape(1, d), w, conv_w, conv_b.reshape(1, -1))


def _diff_attn_kernel(q_ref, k_ref, v_ref, lam_ref, g_ref, o_ref, q2_scr, vx_scr, m_scr, acc_scr, s_scr,
                      *, tq, tc, lam_init):
    seq = k_ref.shape[0]
    q = q_ref[...]
    lane = lax.broadcasted_iota(jnp.int32, q.shape, 1)
    zero = jnp.zeros_like(q)
    q2_scr[0:tq] = jnp.where(lane < A_HEAD_DIM, q, zero)
    q2_scr[tq:2 * tq] = jnp.where(lane >= A_HEAD_DIM, q, zero)
    m_scr[...] = jnp.full(m_scr.shape, -jnp.inf, F32)
    acc_scr[...] = jnp.zeros(acc_scr.shape, F32)

    @pl.when(pl.program_id(2) == 0)
    def _():
        vx_scr[:, 0:A_V_DIM] = v_ref[...]
        vx_scr[:, A_V_DIM:2 * A_V_DIM] = jnp.ones((seq, A_V_DIM), BF16)

    def scores(c, slot):
        r0 = c * tc if isinstance(c, int) else pl.multiple_of(c * tc, tc)
        s_scr[slot] = lax.dot_general(q2_scr[...], k_ref[pl.ds(r0, tc), :], (((1,), (1,)), ((), ())),
                                      preferred_element_type=F32)

    def softmax_pv(c, slot):
        r0 = c * tc if isinstance(c, int) else pl.multiple_of(c * tc, tc)
        slabs = [s_scr[slot, :, j * LANES:(j + 1) * LANES] for j in range(tc // LANES)]
        mx = slabs[0]
        for sl in slabs[1:]:
            mx = jnp.maximum(mx, sl)
        m_prev = m_scr[...]
        m_new = jnp.maximum(m_prev, jnp.max(mx, axis=1, keepdims=True))
        alpha = jnp.exp2(m_prev - m_new)
        p = jnp.concatenate([jnp.exp2(sl - m_new).astype(BF16) for sl in slabs], axis=1)
        pv = jnp.dot(p, vx_scr[pl.ds(r0, tc), :], preferred_element_type=F32)
        acc_scr[:, 0:A_V_DIM] = alpha * acc_scr[:, 0:A_V_DIM] + pv[:, 0:A_V_DIM]
        acc_scr[:, A_V_DIM:2 * A_V_DIM] = alpha * acc_scr[:, A_V_DIM:2 * A_V_DIM] + pv[:, A_V_DIM:2 * A_V_DIM]
        m_scr[...] = m_new

    n_chunks = seq // tc

    def body(it, carry):
        c = 2 * it
        scores(c + 1, 1)
        softmax_pv(c, 0)
        scores(c + 2, 0)
        softmax_pv(c + 1, 1)
        return carry

    scores(0, 0)
    lax.fori_loop(0, n_chunks // 2 - 1, body, 0)
    scores(n_chunks - 1, 1)
    softmax_pv(n_chunks - 2, 0)
    softmax_pv(n_chunks - 1, 1)

    o = acc_scr[:, 0:A_V_DIM] / acc_scr[:, A_V_DIM:2 * A_V_DIM]
    lv = lam_ref[...]
    lam = (jnp.exp(jnp.sum(lv[0:1] * lv[1:2], axis=1, keepdims=True))
           - jnp.exp(jnp.sum(lv[2:3] * lv[3:4], axis=1, keepdims=True)) + F32(lam_init))
    od = o[0:tq] - lam * o[tq:2 * tq]
    y = _rms(od, g_ref[...], A_SUBLN_EPS) * F32(1.0 - lam_init)
    o_ref[...] = y.astype(o_ref.dtype)


def _diff_attn(qkv, lam_vecs, subln_g, lam_init, *, tq=1024, tc=512):
    b, s, _ = qkv.shape
    tq = min(tq, s)
    blk = lambda rows, off: pl.BlockSpec((None, rows, LANES), off)
    return pl.pallas_call(
        functools.partial(_diff_attn_kernel, tq=tq, tc=tc, lam_init=lam_init),
        grid=(b, A_HEADS, s // tq),
        in_specs=[blk(tq, lambda bi, h, i: (bi, i, h)),
                  blk(s, lambda bi, h, i: (bi, 0, A_HEADS + h)),
                  blk(s, lambda bi, h, i: (bi, 0, 2 * A_HEADS + h)),
                  pl.BlockSpec((4, A_HEAD_DIM), lambda bi, h, i: (0, 0)),
                  pl.BlockSpec((1, A_V_DIM), lambda bi, h, i: (0, 0))],
        out_specs=blk(tq, lambda bi, h, i: (bi, i, h)),
        out_shape=jax.ShapeDtypeStruct((b, s, A_HEADS * A_V_DIM), BF16),
        scratch_shapes=[pltpu.VMEM((2 * tq, LANES), BF16),
                        pltpu.VMEM((s, 2 * A_V_DIM), BF16),
                        pltpu.VMEM((2 * tq, LANES), F32),
                        pltpu.VMEM((2 * tq, 2 * A_V_DIM), F32),
                        pltpu.VMEM((2, 2 * tq, tc), F32)],
        compiler_params=_cparams(3),
        name="diff_attn",
    )(qkv, qkv, qkv, lam_vecs, subln_g.reshape(1, A_V_DIM))


def _dil_attn_kernel(q_ref, k_ref, kp_ref, kn_ref, v_ref, vp_ref, vn_ref, o_ref, lse_ref, kbuf, vbuf,
                     *, tl, length):
    hw = B_HALF_W
    t0 = pl.program_id(2) * tl

    for buf, main, prev, nxt in ((kbuf, k_ref, kp_ref, kn_ref), (vbuf, v_ref, vp_ref, vn_ref)):
        buf[0:hw] = prev[...]
        buf[hw:hw + tl] = main[...]
        buf[hw + tl:2 * hw + tl] = nxt[...]

    a_io = lax.broadcasted_iota(jnp.int32, (B_QBLK, B_KBLK), 0)
    c_io = lax.broadcasted_iota(jnp.int32, (B_QBLK, B_KBLK), 1)
    band = (c_io >= a_io) & (c_io <= a_io + 2 * hw)
    lane = lax.broadcasted_iota(jnp.int32, (B_QBLK, LANES), 1)
    lane_q = lane

    def q_block(jb, carry):
        r0 = pl.multiple_of(jb * B_QBLK, B_QBLK)
        ki = t0 + r0 - hw + c_io
        valid = band & (ki >= 0) & (ki < length)
        valid2 = jnp.concatenate([valid, valid], axis=0)
        lse_tile = jnp.zeros((B_QBLK, LANES), F32)
        for hp in range(B_HEADS // 2):
            cols = slice(hp * LANES, (hp + 1) * LANES)
            q = q_ref[pl.ds(r0, B_QBLK), cols]
            zero = jnp.zeros_like(q)
            q2 = jnp.concatenate([jnp.where(lane_q < B_HEAD_DIM, q, zero),
                                  jnp.where(lane_q >= B_HEAD_DIM, q, zero)], axis=0)
            s = lax.dot_general(q2, kbuf[pl.ds(r0, B_KBLK), cols], (((1,), (1,)), ((), ())),
                                preferred_element_type=F32)
            s = jnp.where(valid2, s, -jnp.inf)
            m = jnp.max(s, axis=1, keepdims=True)
            p = jnp.exp(s - m)
            l = jnp.sum(p, axis=1, keepdims=True)
            pv = jnp.dot(p.astype(BF16), vbuf[pl.ds(r0, B_KBLK), cols], preferred_element_type=F32) / l
            o_ref[pl.ds(r0, B_QBLK), cols] = jnp.where(lane < B_HEAD_DIM, pv[0:B_QBLK],
                                                       pv[B_QBLK:2 * B_QBLK]).astype(o_ref.dtype)
            lse = m + jnp.log(l)
            lse_tile = jnp.where(lane == 2 * hp, lse[0:B_QBLK], lse_tile)
            lse_tile = jnp.where(lane == 2 * hp + 1, lse[B_QBLK:2 * B_QBLK], lse_tile)
        lse_ref[pl.ds(r0, B_QBLK), :] = lse_tile
        return carry

    lax.fori_loop(0, tl // B_QBLK, q_block, 0)


def _dil_attn_group(qkv, *, tl=512):
    b, dilation, length, _ = qkv.shape
    tl = min(tl, length)
    hw = B_HALF_W
    width = B_HEADS * B_HEAD_DIM
    per_tile = tl // hw
    n_hw = length // hw
    main = lambda part: pl.BlockSpec((None, None, tl, width), lambda bi, r, i: (bi, r, i, part))
    before = lambda part: pl.BlockSpec((None, None, hw, width),
                                       lambda bi, r, i: (bi, r, jnp.maximum(i * per_tile - 1, 0), part))
    after = lambda part: pl.BlockSpec((None, None, hw, width),
                                      lambda bi, r, i: (bi, r, jnp.minimum((i + 1) * per_tile, n_hw - 1), part))
    return pl.pallas_call(
        functools.partial(_dil_attn_kernel, tl=tl, length=length),
        grid=(b, dilation, length // tl),
        in_specs=[main(0), main(1), before(1), after(1), main(2), before(2), after(2)],
        out_specs=[pl.BlockSpec((None, None, tl, width), lambda bi, r, i: (bi, r, i, 0)),
                   pl.BlockSpec((None, None, tl, LANES), lambda bi, r, i: (bi, r, i, 0))],
        out_shape=[jax.ShapeDtypeStruct((b, dilation, length, width), BF16),
                   jax.ShapeDtypeStruct((b, dilation, length, LANES), F32)],
        scratch_shapes=[pltpu.VMEM((tl + 2 * hw, width), BF16), pltpu.VMEM((tl + 2 * hw, width), BF16)],
        compiler_params=_cparams(3),
        name=f"dil_attn_d{dilation}",
    )(*([qkv] * 7))


def _split3(x):
    hi = x.astype(BF16)
    r1 = x - hi.astype(F32)
    mid = r1.astype(BF16)
    lo = (r1 - mid.astype(F32)).astype(BF16)
    return hi, mid, lo


def _dil_merge_kernel(*refs, n_groups):
    o_refs = refs[:n_groups]
    l_refs = refs[n_groups:2 * n_groups]
    perm_refs = refs[2 * n_groups:3 * n_groups - 1]
    expand_ref, out_ref = refs[3 * n_groups - 1:]
    t = out_ref.shape[0]
    outs, lses = [], []
    for gi in range(n_groups):
        o = o_refs[gi][...].reshape(t, o_refs[gi].shape[-1])
        lse = l_refs[gi][...].reshape(t, LANES)
        if gi == 0:
            outs.append(o.astype(F32))
            lses.append(lse)
        else:
            perm = perm_refs[gi - 1][...]
            outs.append(jnp.dot(perm, o, preferred_element_type=F32))
            lses.append(sum(jnp.dot(perm, piece, preferred_element_type=F32) for piece in _split3(lse)))
    m = functools.reduce(jnp.maximum, lses)
    es = [jnp.exp(l - m) for l in lses]
    den = functools.reduce(lambda a, b: a + b, es)
    acc = jnp.zeros((t, out_ref.shape[1]), F32)
    for gi in range(n_groups):
        w = es[gi] / den
        hi = w.astype(BF16)
        lo = (w - hi.astype(F32)).astype(BF16)
        wide = (jnp.dot(hi, expand_ref[...], preferred_element_type=F32)
                + jnp.dot(lo, expand_ref[...], preferred_element_type=F32))
        acc = acc + wide * outs[gi]
    out_ref[...] = acc.astype(out_ref.dtype)


def _dil_merge(outs, lses, *, t=512):
    b, _, s, width = outs[0].shape
    n_groups = len(outs)
    in_specs, args = [], []
    for arrs, lanes in ((outs, width), (lses, LANES)):
        for a in arrs:
            d = a.shape[1]
            in_specs.append(pl.BlockSpec((None, d, t // d, lanes), lambda bi, i: (bi, 0, i, 0)))
            args.append(a)
    for a in outs[1:]:
        in_specs.append(pl.BlockSpec((t, t), lambda bi, i: (0, 0)))
        args.append(jnp.asarray(_residue_perm(t, a.shape[1]).T, BF16))
    expand = np.zeros((LANES, width), np.float32)
    expand[np.arange(width) // B_HEAD_DIM, np.arange(width)] = 1.0
    in_specs.append(pl.BlockSpec((LANES, width), lambda bi, i: (0, 0)))
    args.append(jnp.asarray(expand, BF16))
    return pl.pallas_call(
        functools.partial(_dil_merge_kernel, n_groups=n_groups),
        grid=(b, s // t),
        in_specs=in_specs,
        out_specs=pl.BlockSpec((None, t, width), lambda bi, i: (bi, i, 0)),
        out_shape=jax.ShapeDtypeStruct((b, s, width), BF16),
        compiler_params=_cparams(2),
        name="dil_merge",
    )(*args)


def _scan_constants(reverse):
    c = C_CHUNK
    t = np.arange(c)[:, None]
    u = np.arange(c)[None, :]
    if not reverse:
        mats = [u <= t, u > t]
    else:
        mats = [u >= t, u < t]
    masks = []
    for hs in C_LEVELS:
        base = (t // (2 * hs)) * (2 * hs)
        mid = base + hs - 1
        same = (u // (2 * hs)) == (t // (2 * hs))
        if not reverse:
            m = np.where(t > mid, (u > mid) & (u <= t), (u > t) & (u <= mid))
            masks.append(same & (t > mid) & (u <= mid))
        else:
            m = np.where(t <= mid, (u >= t) & (u <= mid), (u > mid) & (u < t))
            masks.append(same & (t <= mid) & (u > mid))
        mats.append(m & same)
    stack = np.concatenate([np.asarray(m, np.float32) for m in mats], axis=0)
    return jnp.asarray(stack, BF16), jnp.asarray(np.stack(masks).astype(np.float32))


def _hgrn_scan_kernel(*refs, tt, hp, reverse, final):
    q_ref, v_ref, lf_ref, mstack_ref, masks_ref = refs[:5]
    refs = refs[5:]
    if final:
        ofw_ref, gate_ref, gn_ref = refs[:3]
        refs = refs[3:]
    o_ref, st_scr = refs
    c = C_CHUNK
    n_chunks = tt // c
    nt_contract = (((1,), (1,)), ((), ()))

    @pl.when(pl.program_id(2) == 0)
    def _():
        st_scr[...] = jnp.zeros(st_scr.shape, F32)

    row8 = lax.broadcasted_iota(jnp.int32, (C_DIAG, C_DIM), 0)
    lane_c = lax.broadcasted_iota(jnp.int32, (C_DIAG, c), 1)

    def head_chunk(r0, hi):
        cols = slice(hi * C_DIM, (hi + 1) * C_DIM)
        lf = lf_ref[pl.ds(r0, c), cols]
        top = lf.astype(BF16)
        low = (lf - top.astype(F32)).astype(BF16)
        d2 = jnp.dot(mstack_ref[...], jnp.concatenate([top, low], axis=1), preferred_element_type=F32)
        d = d2[:, 0:C_DIM] + d2[:, C_DIM:2 * C_DIM]
        gq = d[0:c]
        gk = d[c:2 * c]

        q = q_ref[pl.ds(r0, c), cols].astype(F32)
        v = v_ref[pl.ds(r0, c), cols]
        kk = 1.0 - jnp.exp(lf)
        st = st_scr[hi]

        o = lax.dot_general((q * jnp.exp(gq)).astype(BF16), st.astype(BF16), nt_contract,
                            preferred_element_type=F32)

        a = jnp.zeros((c, c), F32)
        for li in range(len(C_LEVELS)):
            e = jnp.exp(-jnp.abs(d[(2 + li) * c:(3 + li) * c]))
            pq = (q * e).astype(BF16)
            pk = (kk * e).astype(BF16)
            a = a + lax.dot_general(pq, pk, nt_contract, preferred_element_type=F32) * masks_ref[li]

        blocks = []
        for bi in range(c // C_DIAG):
            rows = slice(bi * C_DIAG, (bi + 1) * C_DIAG)
            gb, qb, kb = gq[rows], q[rows], kk[rows]
            ablk = jnp.zeros((C_DIAG, c), F32)
            for si in range(C_DIAG):
                keep = (row8 <= si) if reverse else (row8 >= si)
                dec = jnp.exp(jnp.where(keep, gb - gb[si:si + 1], -jnp.inf))
                sc = jnp.sum(qb * dec * kb[si:si + 1], axis=1, keepdims=True)
                ablk = jnp.where(lane_c == bi * C_DIAG + si, sc, ablk)
            blocks.append(ablk)
        a = a + jnp.concatenate(blocks, axis=0)

        o = o + jnp.dot(a.astype(BF16), v, preferred_element_type=F32)

        kd = (kk * jnp.exp(gk)).astype(BF16)
        upd = lax.dot_general(v, kd, (((0,), (0,)), ((), ())), preferred_element_type=F32)
        g_tot = gq[0:1] if reverse else gq[c - 1:c]
        st_scr[hi] = st * jnp.exp(g_tot) + upd

        if final:
            tot = ofw_ref[pl.ds(r0, c), cols] + o
            y = _rms(tot, gn_ref[...], NORM_EPS) * gate_ref[pl.ds(r0, c), cols].astype(F32)
            o_ref[pl.ds(r0, c), cols] = y.astype(o_ref.dtype)
        else:
            o_ref[pl.ds(r0, c), cols] = o

    def chunk(ci, carry):
        cidx = (n_chunks - 1 - ci) if reverse else ci
        r0 = pl.multiple_of(cidx * c, c)
        for hi in range(hp):
            head_chunk(r0, hi)
        return carry

    lax.fori_loop(0, n_chunks, chunk, 0)


def _hgrn_scan(act, logf, reverse, o_fw=None, gnorm_g=None, *, tt=512, hp=8):
    b, s, _ = act.shape
    tt = min(tt, s)
    nt = s // tt
    final = o_fw is not None
    mstack, masks = _scan_constants(reverse)
    groups = C_HEADS // hp
    tmap = (lambda t: nt - 1 - t) if reverse else (lambda t: t)
    blk = lambda part: pl.BlockSpec((None, tt, hp * C_DIM), lambda bi, hg, t: (bi, tmap(t), part * groups + hg))
    in_specs = [blk(0), blk(1), blk(1 if reverse else 0),
                _const_spec(mstack.shape, lambda bi, hg, t: (0, 0)),
                _const_spec(masks.shape, lambda bi, hg, t: (0, 0, 0))]
    args = [act, act, logf, mstack, masks]
    if final:
        in_specs += [blk(0), blk(2), _const_spec((1, C_DIM), lambda bi, hg, t: (0, 0))]
        args += [o_fw, act, gnorm_g.reshape(1, C_DIM)]
    return pl.pallas_call(
        functools.partial(_hgrn_scan_kernel, tt=tt, hp=hp, reverse=reverse, final=final),
        grid=(b, groups, nt),
        in_specs=in_specs,
        out_specs=blk(0),
        out_shape=jax.ShapeDtypeStruct((b, s, C_HEADS * C_DIM), BF16 if final else F32),
        scratch_shapes=[pltpu.VMEM((hp, C_DIM, C_DIM), F32)],
        compiler_params=_cparams(3),
        name="hgrn_scan_bw" if reverse else "hgrn_scan_fw",
    )(*args)


def _diff_lambda_init(layer):
    return 0.8 - 0.6 * math.exp(-0.3 * layer)


def _trunk(x3, p):
    b, s, d = x3.shape
    x = x3.reshape(b * s, d)
    tables = _rope_tables(s)
    for layer in range(DEPTH):
        kind = layer % N_MIXERS
        j = layer // N_MIXERS
        g = p["norm_g"][layer]
        x3 = x.reshape(b, s, d)
        if kind == 0:
            qkv = _qkv_rope(x3, g[0], p["a_w_qkv"][j], tables, A_HEAD_DIM ** -0.5 * LOG2E)
            o = _diff_attn(qkv.reshape(b, s, -1), p["a_lambda"][j], p["a_subln_g"][j], _diff_lambda_init(layer))
            w_o = p["a_w_o"][j]
        elif kind == 1:
            outs, lses = [], []
            for group, (_, dilation) in enumerate(B_PATTERNS):
                qkv = _qkv_rope(x3, g[0], p["b_w_qkv"][j], tables, B_HEAD_DIM ** -0.5, group, dilation)
                o_g, lse_g = _dil_attn_group(qkv)
                outs.append(o_g)
                lses.append(lse_g)
            o = _dil_merge(outs, lses)
            w_o = p["b_w_o"][j]
        else:
            act, logf = _hgrn_in(x, g[0], p["c_w_in"][j], p["c_lb_logits"], layer)
            act, logf = act.reshape(b, s, -1), logf.reshape(b, s, -1)
            o_fw = _hgrn_scan(act, logf, False)
            o = _hgrn_scan(act, logf, True, o_fw, p["c_gnorm_g"][j])
            w_o = p["c_w_o"][j]
        x = _mm_norm_res(o.reshape(b * s, -1), w_o, g[1], x)
        u = _ffn_in(x, g[2], p["f_w_in"][layer], p["f_conv_w"][layer], p["f_conv_b"][layer], s)
        x = _mm_norm_res(u, p["f_w_out"][layer], g[3], x)
    return x.reshape(b, s, d)


def kernel(x_prompt, x_sample, norm_g, a_w_qkv, a_lambda, a_subln_g, a_w_o, b_w_qkv, b_w_o, c_w_in,
           c_lb_logits, c_gnorm_g, c_w_o, f_w_in, f_conv_w, f_conv_b, f_w_out):
    bf = lambda w: w.astype(BF16)
    p = dict(norm_g=norm_g, a_w_qkv=bf(a_w_qkv), a_lambda=a_lambda, a_subln_g=a_subln_g, a_w_o=bf(a_w_o),
             b_w_qkv=bf(b_w_qkv), b_w_o=bf(b_w_o), c_w_in=bf(c_w_in), c_lb_logits=c_lb_logits,
             c_gnorm_g=c_gnorm_g, c_w_o=bf(c_w_o), f_w_in=bf(f_w_in), f_conv_w=f_conv_w,
             f_conv_b=f_conv_b, f_w_out=bf(f_w_out))
    return _trunk(x_prompt, p), _trunk(x_sample, p)
```

```python
import functools
import math

import numpy as np
import jax
import jax.numpy as jnp
from jax import lax
from jax.experimental import pallas as pl
from jax.experimental.pallas import tpu as pltpu

F32 = jnp.float32
BF16 = jnp.bfloat16

D_MODEL = 1024
DEPTH = 4
N_MIXERS = 3
ROPE_THETA = 500000.0
NORM_EPS = 1e-6

A_HEADS = 8
A_HEAD_DIM = 64
A_V_DIM = 128
A_SUBLN_EPS = 1e-5

B_HEADS = 16
B_HEAD_DIM = 64
B_PATTERNS = ((128, 1), (512, 4), (2048, 16))
B_HALF_W = 64
B_QBLK = 2 * B_HALF_W
B_KBLK = 4 * B_HALF_W

C_HEADS = 8
C_DIM = 128
C_CHUNK = 128
C_LEVELS = (64, 32, 16, 8)
C_DIAG = 8

D_FF = 2816
QKV_GROUP = 3 * D_MODEL
ROPE_ROT = 16
LANES = 128
LOG2E = math.log2(math.e)
BF16_ROWS = 16
SUBLANES = 8

V7X_VMEM_LIMIT = 56 * 1024 * 1024


def _cparams(n_axes):
    return pltpu.CompilerParams(dimension_semantics=("arbitrary",) * n_axes,
                                vmem_limit_bytes=V7X_VMEM_LIMIT)


def _const_spec(shape, index_map):
    return pl.BlockSpec(shape, index_map, pipeline_mode=pl.Buffered(1))


def _rms(x, g, eps):
    ms = jnp.mean(x * x, axis=-1, keepdims=True)
    return x * lax.rsqrt(ms + eps) * g


def _softplus_neg_abs(d):
    return jnp.log(1.0 + jnp.exp(-jnp.abs(d)))


def _logaddexp(a, b):
    return jnp.maximum(a, b) + _softplus_neg_abs(a - b)


def _rope_tables(seq):
    half = ROPE_ROT // 2
    inv_freq = ROPE_THETA ** (-jnp.arange(half, dtype=F32) / half)
    ang = jnp.arange(seq, dtype=F32)[:, None] * inv_freq[None, :]
    cos, sin = jnp.cos(ang), jnp.sin(ang)
    rest = B_HEAD_DIM - ROPE_ROT
    c = jnp.concatenate([cos, cos, jnp.ones((seq, rest), F32)], axis=1)
    s1 = jnp.concatenate([-sin, jnp.zeros((seq, rest + half), F32)], axis=1)
    s2 = jnp.concatenate([jnp.zeros((seq, half), F32), sin, jnp.zeros((seq, rest), F32)], axis=1)
    rep = LANES // B_HEAD_DIM
    return tuple(jnp.tile(t, (1, rep)) for t in (c, s1, s2))


def _residue_perm(tm, dilation):
    rows = tm // dilation
    rho = np.arange(tm)
    tok = (rho % rows) * dilation + rho // rows
    perm = np.zeros((tm, tm), np.float32)
    perm[rho, tok] = 1.0
    return perm


def _qkv_rope_kernel(*refs, tn, q_scale, dilation):
    if dilation > 1:
        x_ref, g_ref, w_ref, c_ref, s1_ref, s2_ref, perm_ref, o_ref = refs
    else:
        x_ref, g_ref, w_ref, c_ref, s1_ref, s2_ref, o_ref = refs
    tm = x_ref.shape[0]
    rows = tm // dilation

    h = _rms(x_ref[...], g_ref[...], NORM_EPS).astype(BF16)
    if dilation > 1:
        h = jnp.dot(perm_ref[...], h, preferred_element_type=F32).astype(BF16)

    def put(col, val):
        for rr in range(dilation):
            o_ref[rr, :, col:col + LANES] = val[rr * rows:(rr + 1) * rows].astype(o_ref.dtype)

    c, s1, s2 = c_ref[...], s1_ref[...], s2_ref[...]
    cq, s1q, s2q = c * F32(q_scale), s1 * F32(q_scale), s2 * F32(q_scale)
    for cn in range(QKV_GROUP // tn):
        acc = jnp.dot(h, w_ref[:, cn * tn:(cn + 1) * tn], preferred_element_type=F32)
        for cb in range(tn // LANES):
            col = cn * tn + cb * LANES
            a = acc[:, cb * LANES:(cb + 1) * LANES]
            if col < 2 * D_MODEL:
                tc, t1, t2 = (cq, s1q, s2q) if col < D_MODEL else (c, s1, s2)
                a = a * tc + pltpu.roll(a, LANES - ROPE_ROT // 2, 1) * t1 + pltpu.roll(a, ROPE_ROT // 2, 1) * t2
            put(col, a)


def _qkv_rope(x, g, w, tables, q_scale, group=0, dilation=1, *, tm=512, tn=512):
    b, s, d = x.shape
    rows = tm // dilation
    if dilation > 1:
        tables = tuple(t.reshape(s // tm, rows, dilation, LANES).transpose(0, 2, 1, 3).reshape(s, LANES)
                       for t in tables)
    tab_spec = pl.BlockSpec((tm, LANES), lambda bi, i: (i, 0))
    in_specs = [pl.BlockSpec((None, tm, d), lambda bi, i: (bi, i, 0)),
                _const_spec((1, d), lambda bi, i: (0, 0)),
                _const_spec((d, QKV_GROUP), lambda bi, i: (0, group)),
                tab_spec, tab_spec, tab_spec]
    args = [x, g.reshape(1, d), w, *tables]
    if dilation > 1:
        in_specs.append(_const_spec((tm, tm), lambda bi, i: (0, 0)))
        args.append(jnp.asarray(_residue_perm(tm, dilation), BF16))
    return pl.pallas_call(
        functools.partial(_qkv_rope_kernel, tn=tn, q_scale=q_scale, dilation=dilation),
        grid=(b, s // tm),
        in_specs=in_specs,
        out_specs=pl.BlockSpec((None, dilation, rows, QKV_GROUP), lambda bi, i: (bi, 0, i, 0)),
        out_shape=jax.ShapeDtypeStruct((b, dilation, s // dilation, QKV_GROUP), BF16),
        compiler_params=_cparams(2),
        name=f"qkv_rope_d{dilation}",
    )(*args)


def _hgrn_in_kernel(x_ref, g_ref, w_ref, lb_ref, act_ref, logf_ref, *, tn, layer):
    h = _rms(x_ref[...], g_ref[...], NORM_EPS).astype(BF16)
    logits = lb_ref[...]
    e = jnp.exp(logits - jnp.max(logits, axis=0, keepdims=True))
    p = e / jnp.sum(e, axis=0, keepdims=True)
    cum = p[0:1]
    for r in range(1, layer + 1):
        cum = cum + p[r:r + 1]
    lb = cum - p[0:1]
    log_lb = jnp.log(lb)
    log_1m_lb = jnp.log1p(-lb)

    for cn in range(5 * D_MODEL // tn):
        col = cn * tn
        part, off = col // D_MODEL, col % D_MODEL
        acc = jnp.dot(h, w_ref[:, col:col + tn], preferred_element_type=F32)
        if part == 0:
            act_ref[:, off:off + tn] = (acc * jax.nn.sigmoid(acc) * F32(C_DIM ** -0.5)).astype(act_ref.dtype)
        elif part in (1, 2):
            log_sig = jnp.minimum(acc, 0.0) - _softplus_neg_abs(acc)
            lo = (part - 1) * D_MODEL + off
            logf_ref[:, lo:lo + tn] = _logaddexp(log_lb[:, off:off + tn], log_1m_lb[:, off:off + tn] + log_sig)
        elif part == 3:
            act_ref[:, D_MODEL + off:D_MODEL + off + tn] = acc.astype(act_ref.dtype)
        else:
            act_ref[:, 2 * D_MODEL + off:2 * D_MODEL + off + tn] = (acc * jax.nn.sigmoid(acc)).astype(act_ref.dtype)


def _hgrn_in(x, g, w, lb_logits, layer, *, tm=512, tn=512):
    m, d = x.shape
    return pl.pallas_call(
        functools.partial(_hgrn_in_kernel, tn=tn, layer=layer),
        grid=(m // tm,),
        in_specs=[pl.BlockSpec((tm, d), lambda i: (i, 0)),
                  _const_spec((1, d), lambda i: (0, 0)),
                  _const_spec((d, 5 * D_MODEL), lambda i: (0, 0)),
                  _const_spec((DEPTH, D_MODEL), lambda i: (0, 0))],
        out_specs=[pl.BlockSpec((tm, 3 * D_MODEL), lambda i: (i, 0)),
                   pl.BlockSpec((tm, 2 * D_MODEL), lambda i: (i, 0))],
        out_shape=[jax.ShapeDtypeStruct((m, 3 * D_MODEL), BF16),
                   jax.ShapeDtypeStruct((m, 2 * D_MODEL), F32)],
        compiler_params=_cparams(1),
        name="hgrn_in",
    )(x, g.reshape(1, d), w, lb_logits)


def _mm_norm_res_kernel(a_ref, w_ref, g_ref, x_ref, o_ref, *, sub):
    for rb in range(a_ref.shape[0] // sub):
        rows = slice(rb * sub, (rb + 1) * sub)
        h = jnp.dot(a_ref[rows, :], w_ref[...], preferred_element_type=F32)
        o_ref[rows, :] = x_ref[rows, :] + _rms(h, g_ref[...], NORM_EPS)


def _mm_norm_res(a, w, g, x, *, tm=1024, sub=256):
    m, k = a.shape
    d = w.shape[1]
    return pl.pallas_call(
        functools.partial(_mm_norm_res_kernel, sub=sub),
        grid=(m // tm,),
        in_specs=[pl.BlockSpec((tm, k), lambda i: (i, 0)),
                  _const_spec((k, d), lambda i: (0, 0)),
                  _const_spec((1, d), lambda i: (0, 0)),
                  pl.BlockSpec((tm, d), lambda i: (i, 0))],
        out_specs=pl.BlockSpec((tm, d), lambda i: (i, 0)),
        out_shape=jax.ShapeDtypeStruct((m, d), F32),
        compiler_params=_cparams(1),
        name="mm_norm_res",
    )(a, w, g.reshape(1, d), x)


def _ffn_in_kernel(x_ref, xp_ref, xn_ref, g_ref, w_ref, cw_ref, cb_ref, o_ref, h_scr, u_scr, *, tm, tn, seq):
    i = pl.program_id(0)
    n_tiles = pl.num_programs(0) - 1
    halo = BF16_ROWS
    rows = tm + 2 * halo
    n_chunks = D_FF // tn
    slot_new = i % 2
    slot_old = 1 - slot_new

    @pl.when(i == 0)
    def _():
        u_scr[...] = jnp.zeros(u_scr.shape, F32)

    t = jnp.minimum(i, n_tiles - 1)
    g = g_ref[...]
    first = (t * tm) % seq == 0
    last = ((t + 1) * tm) % seq == 0
    hp = _rms(xp_ref[...], g, NORM_EPS)
    hn = _rms(xn_ref[...], g, NORM_EPS)
    h_scr[0:halo] = jnp.where(first, 0.0, hp).astype(BF16)
    h_scr[halo:halo + tm] = _rms(x_ref[...], g, NORM_EPS).astype(BF16)
    h_scr[halo + tm:rows] = jnp.where(last, 0.0, hn).astype(BF16)
    def matmul(cn):
        u_scr[slot_new, cn] = jnp.dot(h_scr[...], w_ref[:, cn * tn:(cn + 1) * tn], preferred_element_type=F32)

    def conv(cn, scale=None):
        cw = cw_ref[:, cn * tn:(cn + 1) * tn]
        cb = cb_ref[:, cn * tn:(cn + 1) * tn]
        if scale is not None:
            cw, cb = cw * scale, cb * scale
        return (cb + u_scr[slot_old, cn, halo - 1:halo - 1 + tm, :] * cw[0:1]
                + u_scr[slot_old, cn, halo:halo + tm, :] * cw[1:2]
                + u_scr[slot_old, cn, halo + 1:halo + 1 + tm, :] * cw[2:3])

    k0 = math.sqrt(2.0 / math.pi)
    for cn in range(n_chunks):
        matmul(2 * cn)
        matmul(2 * cn + 1)
        a = conv(cn)
        half_b = conv(n_chunks + cn, F32(0.5))
        th = jnp.tanh(a * (F32(k0) + F32(k0 * 0.044715) * (a * a)))
        o_ref[:, cn * tn:(cn + 1) * tn] = ((a * half_b) * (1.0 + th)).astype(o_ref.dtype)


def _ffn_in(x, g, w, conv_w, conv_b, seq, *, tm=512, tn=256):
    m, d = x.shape
    halo = BF16_ROWS
    hb = tm // halo
    last_blk = m // halo - 1
    n_tiles = m // tm
    cur = lambda i: jnp.minimum(i, n_tiles - 1)
    return pl.pallas_call(
        functools.partial(_ffn_in_kernel, tm=tm, tn=tn, seq=seq),
        grid=(n_tiles + 1,),
        in_specs=[pl.BlockSpec((tm, d), lambda i: (cur(i), 0)),
                  pl.BlockSpec((halo, d), lambda i: (jnp.maximum(cur(i) * hb - 1, 0), 0)),
                  pl.BlockSpec((halo, d), lambda i: (jnp.minimum((cur(i) + 1) * hb, last_blk), 0)),
                  _const_spec((1, d), lambda i: (0, 0)),
                  _const_spec((d, 2 * D_FF), lambda i: (0, 0)),
                  _const_spec((3, 2 * D_FF), lambda i: (0, 0)),
                  _const_spec((1, 2 * D_FF), lambda i: (0, 0))],
        out_specs=pl.BlockSpec((tm, D_FF), lambda i: (jnp.maximum(i - 1, 0), 0)),
        out_shape=jax.ShapeDtypeStruct((m, D_FF), BF16),
        scratch_shapes=[pltpu.VMEM((tm + 2 * halo, d), BF16),
                        pltpu.VMEM((2, 2 * D_FF // tn, tm + 2 * halo, tn), F32)],
        compiler_params=_cparams(1),
        name="ffn_in",
    )(x, x, x, g.reshape(1, d), w, conv_w, conv_b.reshape(1, -1))


def _diff_attn_kernel(q_ref, k_ref, v_ref, lam_ref, g_ref, o_ref, q2_scr, vx_scr, m_scr, acc_scr, s_scr,
                      *, tq, tc, unroll, lam_init):
    seq = k_ref.shape[0]
    q = q_ref[...]
    lane = lax.broadcasted_iota(jnp.int32, q.shape, 1)
    zero = jnp.zeros_like(q)
    q2_scr[0:tq] = jnp.where(lane < A_HEAD_DIM, q, zero)
    q2_scr[tq:2 * tq] = jnp.where(lane >= A_HEAD_DIM, q, zero)
    m_scr[...] = jnp.full(m_scr.shape, -jnp.inf, F32)
    acc_scr[...] = jnp.zeros(acc_scr.shape, F32)

    @pl.when(pl.program_id(2) == 0)
    def _():
        vx_scr[:, 0:A_V_DIM] = v_ref[...]
        vx_scr[:, A_V_DIM:2 * A_V_DIM] = jnp.ones((seq, A_V_DIM), BF16)

    def scores(c, slot):
        r0 = c * tc if isinstance(c, int) else pl.multiple_of(c * tc, tc)
        s_scr[slot] = lax.dot_general(q2_scr[...], k_ref[pl.ds(r0, tc), :], (((1,), (1,)), ((), ())),
                                      preferred_element_type=F32)

    def softmax_pv(c, slot):
        r0 = c * tc if isinstance(c, int) else pl.multiple_of(c * tc, tc)
        slabs = [s_scr[slot, :, j * LANES:(j + 1) * LANES] for j in range(tc // LANES)]
        mx = slabs[0]
        for sl in slabs[1:]:
            mx = jnp.maximum(mx, sl)
        m_prev = m_scr[...]
        m_new = jnp.maximum(m_prev, jnp.max(mx, axis=1, keepdims=True))
        alpha = jnp.exp2(m_prev - m_new)
        p = jnp.concatenate([jnp.exp2(sl - m_new).astype(BF16) for sl in slabs], axis=1)
        pv = jnp.dot(p, vx_scr[pl.ds(r0, tc), :], preferred_element_type=F32)
        acc_scr[:, 0:A_V_DIM] = alpha * acc_scr[:, 0:A_V_DIM] + pv[:, 0:A_V_DIM]
        acc_scr[:, A_V_DIM:2 * A_V_DIM] = alpha * acc_scr[:, A_V_DIM:2 * A_V_DIM] + pv[:, A_V_DIM:2 * A_V_DIM]
        m_scr[...] = m_new

    n_chunks = seq // tc

    def body(it, carry):
        for u in range(unroll):
            c = 2 * (unroll * it + u)
            scores(c + 1, 1)
            softmax_pv(c, 0)
            scores(jnp.minimum(c + 2, n_chunks - 1), 0)
            softmax_pv(c + 1, 1)
        return carry

    scores(0, 0)
    lax.fori_loop(0, n_chunks // (2 * unroll), body, 0)

    o = acc_scr[:, 0:A_V_DIM] / acc_scr[:, A_V_DIM:2 * A_V_DIM]
    lv = lam_ref[...]
    lam = (jnp.exp(jnp.sum(lv[0:1] * lv[1:2], axis=1, keepdims=True))
           - jnp.exp(jnp.sum(lv[2:3] * lv[3:4], axis=1, keepdims=True)) + F32(lam_init))
    od = o[0:tq] - lam * o[tq:2 * tq]
    y = _rms(od, g_ref[...], A_SUBLN_EPS) * F32(1.0 - lam_init)
    o_ref[...] = y.astype(o_ref.dtype)


def _diff_attn(qkv, lam_vecs, subln_g, lam_init, *, tq=1024, tc=512, unroll=2):
    b, s, _ = qkv.shape
    tq = min(tq, s)
    blk = lambda rows, off: pl.BlockSpec((None, rows, LANES), off)
    return pl.pallas_call(
        functools.partial(_diff_attn_kernel, tq=tq, tc=tc, unroll=unroll, lam_init=lam_init),
        grid=(b, A_HEADS, s // tq),
        in_specs=[blk(tq, lambda bi, h, i: (bi, i, h)),
                  blk(s, lambda bi, h, i: (bi, 0, A_HEADS + h)),
                  blk(s, lambda bi, h, i: (bi, 0, 2 * A_HEADS + h)),
                  pl.BlockSpec((4, A_HEAD_DIM), lambda bi, h, i: (0, 0)),
                  pl.BlockSpec((1, A_V_DIM), lambda bi, h, i: (0, 0))],
        out_specs=blk(tq, lambda bi, h, i: (bi, i, h)),
        out_shape=jax.ShapeDtypeStruct((b, s, A_HEADS * A_V_DIM), BF16),
        scratch_shapes=[pltpu.VMEM((2 * tq, LANES), BF16),
                        pltpu.VMEM((s, 2 * A_V_DIM), BF16),
                        pltpu.VMEM((2 * tq, LANES), F32),
                        pltpu.VMEM((2 * tq, 2 * A_V_DIM), F32),
                        pltpu.VMEM((2, 2 * tq, tc), F32)],
        compiler_params=_cparams(3),
        name="diff_attn",
    )(qkv, qkv, qkv, lam_vecs, subln_g.reshape(1, A_V_DIM))


def _dil_attn_kernel(q_ref, k_ref, kp_ref, kn_ref, v_ref, vp_ref, vn_ref, o_ref, lse_ref, kbuf, vbuf,
                     *, tl, length):
    hw = B_HALF_W
    t0 = pl.program_id(2) * tl

    for buf, main, prev, nxt in ((kbuf, k_ref, kp_ref, kn_ref), (vbuf, v_ref, vp_ref, vn_ref)):
        buf[0:hw] = prev[...]
        buf[hw:hw + tl] = main[...]
        buf[hw + tl:2 * hw + tl] = nxt[...]

    a_io = lax.broadcasted_iota(jnp.int32, (B_QBLK, B_KBLK), 0)
    c_io = lax.broadcasted_iota(jnp.int32, (B_QBLK, B_KBLK), 1)
    band = (c_io >= a_io) & (c_io <= a_io + 2 * hw)
    lane = lax.broadcasted_iota(jnp.int32, (B_QBLK, LANES), 1)
    lane_q = lane

    def q_block(jb, carry):
        r0 = pl.multiple_of(jb * B_QBLK, B_QBLK)
        ki = t0 + r0 - hw + c_io
        valid = band & (ki >= 0) & (ki < length)
        valid2 = jnp.concatenate([valid, valid], axis=0)
        lse_tile = jnp.zeros((B_QBLK, LANES), F32)
        for hp in range(B_HEADS // 2):
            cols = slice(hp * LANES, (hp + 1) * LANES)
            q = q_ref[pl.ds(r0, B_QBLK), cols]
            zero = jnp.zeros_like(q)
            q2 = jnp.concatenate([jnp.where(lane_q < B_HEAD_DIM, q, zero),
                                  jnp.where(lane_q >= B_HEAD_DIM, q, zero)], axis=0)
            s = lax.dot_general(q2, kbuf[pl.ds(r0, B_KBLK), cols], (((1,), (1,)), ((), ())),
                                preferred_element_type=F32)
            s = jnp.where(valid2, s, -jnp.inf)
            m = jnp.max(s, axis=1, keepdims=True)
            p = jnp.exp(s - m)
            l = jnp.sum(p, axis=1, keepdims=True)
            pv = jnp.dot(p.astype(BF16), vbuf[pl.ds(r0, B_KBLK), cols], preferred_element_type=F32) / l
            o_ref[pl.ds(r0, B_QBLK), cols] = jnp.where(lane < B_HEAD_DIM, pv[0:B_QBLK],
                                                       pv[B_QBLK:2 * B_QBLK]).astype(o_ref.dtype)
            lse = m + jnp.log(l)
            lse_tile = jnp.where(lane == 2 * hp, lse[0:B_QBLK], lse_tile)
            lse_tile = jnp.where(lane == 2 * hp + 1, lse[B_QBLK:2 * B_QBLK], lse_tile)
        lse_ref[pl.ds(r0, B_QBLK), :] = lse_tile
        return carry

    lax.fori_loop(0, tl // B_QBLK, q_block, 0)


def _dil_attn_group(qkv, *, tl=512):
    b, dilation, length, _ = qkv.shape
    tl = min(tl, length)
    hw = B_HALF_W
    width = B_HEADS * B_HEAD_DIM
    per_tile = tl // hw
    n_hw = length // hw
    main = lambda part: pl.BlockSpec((None, None, tl, width), lambda bi, r, i: (bi, r, i, part))
    before = lambda part: pl.BlockSpec((None, None, hw, width),
                                       lambda bi, r, i: (bi, r, jnp.maximum(i * per_tile - 1, 0), part))
    after = lambda part: pl.BlockSpec((None, None, hw, width),
                                      lambda bi, r, i: (bi, r, jnp.minimum((i + 1) * per_tile, n_hw - 1), part))
    return pl.pallas_call(
        functools.partial(_dil_attn_kernel, tl=tl, length=length),
        grid=(b, dilation, length // tl),
        in_specs=[main(0), main(1), before(1), after(1), main(2), before(2), after(2)],
        out_specs=[pl.BlockSpec((None, None, tl, width), lambda bi, r, i: (bi, r, i, 0)),
                   pl.BlockSpec((None, None, tl, LANES), lambda bi, r, i: (bi, r, i, 0))],
        out_shape=[jax.ShapeDtypeStruct((b, dilation, length, width), BF16),
                   jax.ShapeDtypeStruct((b, dilation, length, LANES), F32)],
        scratch_shapes=[pltpu.VMEM((tl + 2 * hw, width), BF16), pltpu.VMEM((tl + 2 * hw, width), BF16)],
        compiler_params=_cparams(3),
        name=f"dil_attn_d{dilation}",
    )(*([qkv] * 7))


def _split3(x):
    hi = x.astype(BF16)
    r1 = x - hi.astype(F32)
    mid = r1.astype(BF16)
    lo = (r1 - mid.astype(F32)).astype(BF16)
    return hi, mid, lo


def _dil_merge_kernel(*refs, n_groups):
    o_refs = refs[:n_groups]
    l_refs = refs[n_groups:2 * n_groups]
    perm_refs = refs[2 * n_groups:3 * n_groups - 1]
    expand_ref, out_ref = refs[3 * n_groups - 1:]
    t = out_ref.shape[0]
    outs, lses = [], []
    for gi in range(n_groups):
        o = o_refs[gi][...].reshape(t, o_refs[gi].shape[-1])
        lse = l_refs[gi][...].reshape(t, LANES)
        if gi == 0:
            outs.append(o.astype(F32))
            lses.append(lse)
        else:
            perm = perm_refs[gi - 1][...]
            outs.append(jnp.dot(perm, o, preferred_element_type=F32))
            lses.append(sum(jnp.dot(perm, piece, preferred_element_type=F32) for piece in _split3(lse)))
    m = functools.reduce(jnp.maximum, lses)
    es = [jnp.exp(l - m) for l in lses]
    den = functools.reduce(lambda a, b: a + b, es)
    acc = jnp.zeros((t, out_ref.shape[1]), F32)
    for gi in range(n_groups):
        w = es[gi] / den
        hi = w.astype(BF16)
        lo = (w - hi.astype(F32)).astype(BF16)
        wide = (jnp.dot(hi, expand_ref[...], preferred_element_type=F32)
                + jnp.dot(lo, expand_ref[...], preferred_element_type=F32))
        acc = acc + wide * outs[gi]
    out_ref[...] = acc.astype(out_ref.dtype)


def _dil_merge(outs, lses, *, t=512):
    b, _, s, width = outs[0].shape
    n_groups = len(outs)
    in_specs, args = [], []
    for arrs, lanes in ((outs, width), (lses, LANES)):
        for a in arrs:
            d = a.shape[1]
            in_specs.append(pl.BlockSpec((None, d, t // d, lanes), lambda bi, i: (bi, 0, i, 0)))
            args.append(a)
    for a in outs[1:]:
        in_specs.append(pl.BlockSpec((t, t), lambda bi, i: (0, 0)))
        args.append(jnp.asarray(_residue_perm(t, a.shape[1]).T, BF16))
    expand = np.zeros((LANES, width), np.float32)
    expand[np.arange(width) // B_HEAD_DIM, np.arange(width)] = 1.0
    in_specs.append(pl.BlockSpec((LANES, width), lambda bi, i: (0, 0)))
    args.append(jnp.asarray(expand, BF16))
    return pl.pallas_call(
        functools.partial(_dil_merge_kernel, n_groups=n_groups),
        grid=(b, s // t),
        in_specs=in_specs,
        out_specs=pl.BlockSpec((None, t, width), lambda bi, i: (bi, i, 0)),
        out_shape=jax.ShapeDtypeStruct((b, s, width), BF16),
        compiler_params=_cparams(2),
        name="dil_merge",
    )(*args)


def _scan_constants(reverse):
    c = C_CHUNK
    t = np.arange(c)[:, None]
    u = np.arange(c)[None, :]
    if not reverse:
        mats = [u <= t, u > t]
    else:
        mats = [u >= t, u < t]
    masks = []
    for hs in C_LEVELS:
        base = (t // (2 * hs)) * (2 * hs)
        mid = base + hs - 1
        same = (u // (2 * hs)) == (t // (2 * hs))
        if not reverse:
            m = np.where(t > mid, (u > mid) & (u <= t), (u > t) & (u <= mid))
            masks.append(same & (t > mid) & (u <= mid))
        else:
            m = np.where(t <= mid, (u >= t) & (u <= mid), (u > mid) & (u < t))
            masks.append(same & (t <= mid) & (u > mid))
        mats.append(m & same)
    stack = np.concatenate([np.asarray(m, np.float32) for m in mats], axis=0)
    return jnp.asarray(stack, BF16), jnp.asarray(np.stack(masks).astype(np.float32))


def _hgrn_scan_kernel(*refs, tt, hp, reverse, final):
    q_ref, v_ref, lf_ref, mstack_ref, masks_ref = refs[:5]
    refs = refs[5:]
    if final:
        ofw_ref, gate_ref, gn_ref = refs[:3]
        refs = refs[3:]
    o_ref, st_scr = refs
    c = C_CHUNK
    n_chunks = tt // c
    nt_contract = (((1,), (1,)), ((), ()))

    @pl.when(pl.program_id(2) == 0)
    def _():
        st_scr[...] = jnp.zeros(st_scr.shape, F32)

    row8 = lax.broadcasted_iota(jnp.int32, (C_DIAG, C_DIM), 0)
    lane_c = lax.broadcasted_iota(jnp.int32, (C_DIAG, c), 1)

    def head_chunk(r0, hi):
        cols = slice(hi * C_DIM, (hi + 1) * C_DIM)
        lf = lf_ref[pl.ds(r0, c), cols]
        top = lf.astype(BF16)
        low = (lf - top.astype(F32)).astype(BF16)
        d2 = jnp.dot(mstack_ref[...], jnp.concatenate([top, low], axis=1), preferred_element_type=F32)
        d = d2[:, 0:C_DIM] + d2[:, C_DIM:2 * C_DIM]
        gq = d[0:c]
        gk = d[c:2 * c]

        q = q_ref[pl.ds(r0, c), cols].astype(F32)
        v = v_ref[pl.ds(r0, c), cols]
        kk = 1.0 - jnp.exp(lf)
        st = st_scr[hi]

        o = lax.dot_general((q * jnp.exp(gq)).astype(BF16), st.astype(BF16), nt_contract,
                            preferred_element_type=F32)

        a = jnp.zeros((c, c), F32)
        for li in range(len(C_LEVELS)):
            e = jnp.exp(-jnp.abs(d[(2 + li) * c:(3 + li) * c]))
            pq = (q * e).astype(BF16)
            pk = (kk * e).astype(BF16)
            a = a + lax.dot_general(pq, pk, nt_contract, preferred_element_type=F32) * masks_ref[li]

        blocks = []
        for bi in range(c // C_DIAG):
            rows = slice(bi * C_DIAG, (bi + 1) * C_DIAG)
            gb, qb, kb = gq[rows], q[rows], kk[rows]
            ablk = jnp.zeros((C_DIAG, c), F32)
            for si in range(C_DIAG):
                keep = (row8 <= si) if reverse else (row8 >= si)
                dec = jnp.exp(jnp.where(keep, gb - gb[si:si + 1], -jnp.inf))
                sc = jnp.sum(qb * dec * kb[si:si + 1], axis=1, keepdims=True)
                ablk = jnp.where(lane_c == bi * C_DIAG + si, sc, ablk)
            blocks.append(ablk)
        a = a + jnp.concatenate(blocks, axis=0)

        o = o + jnp.dot(a.astype(BF16), v, preferred_element_type=F32)

        kd = (kk * jnp.exp(gk)).astype(BF16)
        upd = lax.dot_general(v, kd, (((0,), (0,)), ((), ())), preferred_element_type=F32)
        g_tot = gq[0:1] if reverse else gq[c - 1:c]
        st_scr[hi] = st * jnp.exp(g_tot) + upd

        if final:
            tot = ofw_ref[pl.ds(r0, c), cols] + o
            y = _rms(tot, gn_ref[...], NORM_EPS) * gate_ref[pl.ds(r0, c), cols].astype(F32)
            o_ref[pl.ds(r0, c), cols] = y.astype(o_ref.dtype)
        else:
            o_ref[pl.ds(r0, c), cols] = o

    def chunk(ci, carry):
        cidx = (n_chunks - 1 - ci) if reverse else ci
        r0 = pl.multiple_of(cidx * c, c)
        for hi in range(hp):
            head_chunk(r0, hi)
        return carry

    lax.fori_loop(0, n_chunks, chunk, 0)


def _hgrn_scan(act, logf, reverse, o_fw=None, gnorm_g=None, *, tt=512, hp=8):
    b, s, _ = act.shape
    tt = min(tt, s)
    nt = s // tt
    final = o_fw is not None
    mstack, masks = _scan_constants(reverse)
    groups = C_HEADS // hp
    tmap = (lambda t: nt - 1 - t) if reverse else (lambda t: t)
    blk = lambda part: pl.BlockSpec((None, tt, hp * C_DIM), lambda bi, hg, t: (bi, tmap(t), part * groups + hg))
    in_specs = [blk(0), blk(1), blk(1 if reverse else 0),
                _const_spec(mstack.shape, lambda bi, hg, t: (0, 0)),
                _const_spec(masks.shape, lambda bi, hg, t: (0, 0, 0))]
    args = [act, act, logf, mstack, masks]
    if final:
        in_specs += [blk(0), blk(2), _const_spec((1, C_DIM), lambda bi, hg, t: (0, 0))]
        args += [o_fw, act, gnorm_g.reshape(1, C_DIM)]
    return pl.pallas_call(
        functools.partial(_hgrn_scan_kernel, tt=tt, hp=hp, reverse=reverse, final=final),
        grid=(b, groups, nt),
        in_specs=in_specs,
        out_specs=blk(0),
        out_shape=jax.ShapeDtypeStruct((b, s, C_HEADS * C_DIM), BF16 if final else F32),
        scratch_shapes=[pltpu.VMEM((hp, C_DIM, C_DIM), F32)],
        compiler_params=_cparams(3),
        name="hgrn_scan_bw" if reverse else "hgrn_scan_fw",
    )(*args)


def _diff_lambda_init(layer):
    return 0.8 - 0.6 * math.exp(-0.3 * layer)


def _trunk(x3, p):
    b, s, d = x3.shape
    x = x3.reshape(b * s, d)
    tables = _rope_tables(s)
    for layer in range(DEPTH):
        kind = layer % N_MIXERS
        j = layer // N_MIXERS
        g = p["norm_g"][layer]
        x3 = x.reshape(b, s, d)
        if kind == 0:
            qkv = _qkv_rope(x3, g[0], p["a_w_qkv"][j], tables, A_HEAD_DIM ** -0.5 * LOG2E)
            o = _diff_attn(qkv.reshape(b, s, -1), p["a_lambda"][j], p["a_subln_g"][j], _diff_lambda_init(layer))
            w_o = p["a_w_o"][j]
        elif kind == 1:
            outs, lses = [], []
            for group, (_, dilation) in enumerate(B_PATTERNS):
                qkv = _qkv_rope(x3, g[0], p["b_w_qkv"][j], tables, B_HEAD_DIM ** -0.5, group, dilation)
                o_g, lse_g = _dil_attn_group(qkv)
                outs.append(o_g)
                lses.append(lse_g)
            o = _dil_merge(outs, lses)
            w_o = p["b_w_o"][j]
        else:
            act, logf = _hgrn_in(x, g[0], p["c_w_in"][j], p["c_lb_logits"], layer)
            act, logf = act.reshape(b, s, -1), logf.reshape(b, s, -1)
            o_fw = _hgrn_scan(act, logf, False)
            o = _hgrn_scan(act, logf, True, o_fw, p["c_gnorm_g"][j])
            w_o = p["c_w_o"][j]
        x = _mm_norm_res(o.reshape(b * s, -1), w_o, g[1], x)
        u = _ffn_in(x, g[2], p["f_w_in"][layer], p["f_conv_w"][layer], p["f_conv_b"][layer], s)
        x = _mm_norm_res(u, p["f_w_out"][layer], g[3], x)
    return x.reshape(b, s, d)


def kernel(x_prompt, x_sample, norm_g, a_w_qkv, a_lambda, a_subln_g, a_w_o, b_w_qkv, b_w_o, c_w_in,
           c_lb_logits, c_gnorm_g, c_w_o, f_w_in, f_conv_w, f_conv_b, f_w_out):
    bf = lambda w: w.astype(BF16)
    p = dict(norm_g=norm_g, a_w_qkv=bf(a_w_qkv), a_lambda=a_lambda, a_subln_g=a_subln_g, a_w_o=bf(a_w_o),
             b_w_qkv=bf(b_w_qkv), b_w_o=bf(b_w_o), c_w_in=bf(c_w_in), c_lb_logits=c_lb_logits,
             c_gnorm_g=c_gnorm_g, c_w_o=bf(c_w_o), f_w_in=bf(f_w_in), f_conv_w=f_conv_w,
             f_conv_b=f_conv_b, f_w_out=bf(f_w_out))
    return _trunk(x_prompt, p), _trunk(x_sample, p)
```

```python
import functools
import math

import numpy as np
import jax
import jax.numpy as jnp
from jax import lax
from jax.experimental import pallas as pl
from jax.experimental.pallas import tpu as pltpu

F32 = jnp.float32
BF16 = jnp.bfloat16

D_MODEL = 1024
DEPTH = 4
N_MIXERS = 3
ROPE_THETA = 500000.0
NORM_EPS = 1e-6

A_HEADS = 8
A_HEAD_DIM = 64
A_V_DIM = 128
A_SUBLN_EPS = 1e-5

B_HEADS = 16
B_HEAD_DIM = 64
B_PATTERNS = ((128, 1), (512, 4), (2048, 16))
B_HALF_W = 64
B_QBLK = 2 * B_HALF_W
B_KBLK = 4 * B_HALF_W

C_HEADS = 8
C_DIM = 128
C_CHUNK = 128
C_LEVELS = (64, 32, 16, 8)
C_DIAG = 8

D_FF = 2816
QKV_GROUP = 3 * D_MODEL
ROPE_ROT = 16
LANES = 128
LOG2E = math.log2(math.e)
BF16_ROWS = 16
SUBLANES = 8

V7X_VMEM_LIMIT = 56 * 1024 * 1024


def _cparams(n_axes):
    return pltpu.CompilerParams(dimension_semantics=("arbitrary",) * n_axes,
                                vmem_limit_bytes=V7X_VMEM_LIMIT)


def _const_spec(shape, index_map):
    return pl.BlockSpec(shape, index_map, pipeline_mode=pl.Buffered(1))


def _rms(x, g, eps):
    ms = jnp.mean(x * x, axis=-1, keepdims=True)
    return x * lax.rsqrt(ms + eps) * g


def _softplus_neg_abs(d):
    return jnp.log(1.0 + jnp.exp(-jnp.abs(d)))


def _logaddexp(a, b):
    return jnp.maximum(a, b) + _softplus_neg_abs(a - b)


def _rope_tables(seq):
    half = ROPE_ROT // 2
    inv_freq = ROPE_THETA ** (-jnp.arange(half, dtype=F32) / half)
    ang = jnp.arange(seq, dtype=F32)[:, None] * inv_freq[None, :]
    cos, sin = jnp.cos(ang), jnp.sin(ang)
    rest = B_HEAD_DIM - ROPE_ROT
    c = jnp.concatenate([cos, cos, jnp.ones((seq, rest), F32)], axis=1)
    s1 = jnp.concatenate([-sin, jnp.zeros((seq, rest + half), F32)], axis=1)
    s2 = jnp.concatenate([jnp.zeros((seq, half), F32), sin, jnp.zeros((seq, rest), F32)], axis=1)
    rep = LANES // B_HEAD_DIM
    return tuple(jnp.tile(t, (1, rep)) for t in (c, s1, s2))


def _residue_perm(tm, dilation):
    rows = tm // dilation
    rho = np.arange(tm)
    tok = (rho % rows) * dilation + rho // rows
    perm = np.zeros((tm, tm), np.float32)
    perm[rho, tok] = 1.0
    return perm


def _qkv_rope_kernel(*refs, tn, q_scale, dilation):
    if dilation > 1:
        x_ref, g_ref, w_ref, c_ref, s1_ref, s2_ref, perm_ref, o_ref = refs
    else:
        x_ref, g_ref, w_ref, c_ref, s1_ref, s2_ref, o_ref = refs
    tm = x_ref.shape[0]
    rows = tm // dilation

    h = _rms(x_ref[...], g_ref[...], NORM_EPS).astype(BF16)
    if dilation > 1:
        h = jnp.dot(perm_ref[...], h, preferred_element_type=F32).astype(BF16)

    def put(col, val):
        for rr in range(dilation):
            o_ref[rr, :, col:col + LANES] = val[rr * rows:(rr + 1) * rows].astype(o_ref.dtype)

    c, s1, s2 = c_ref[...], s1_ref[...], s2_ref[...]
    cq, s1q, s2q = c * F32(q_scale), s1 * F32(q_scale), s2 * F32(q_scale)
    for cn in range(QKV_GROUP // tn):
        acc = jnp.dot(h, w_ref[:, cn * tn:(cn + 1) * tn], preferred_element_type=F32)
        for cb in range(tn // LANES):
            col = cn * tn + cb * LANES
            a = acc[:, cb * LANES:(cb + 1) * LANES]
            if col < 2 * D_MODEL:
                tc, t1, t2 = (cq, s1q, s2q) if col < D_MODEL else (c, s1, s2)
                a = a * tc + pltpu.roll(a, LANES - ROPE_ROT // 2, 1) * t1 + pltpu.roll(a, ROPE_ROT // 2, 1) * t2
            put(col, a)


def _qkv_rope(x, g, w, tables, q_scale, group=0, dilation=1, *, tm=512, tn=512):
    b, s, d = x.shape
    rows = tm // dilation
    if dilation > 1:
        tables = tuple(t.reshape(s // tm, rows, dilation, LANES).transpose(0, 2, 1, 3).reshape(s, LANES)
                       for t in tables)
    tab_spec = pl.BlockSpec((tm, LANES), lambda bi, i: (i, 0))
    in_specs = [pl.BlockSpec((None, tm, d), lambda bi, i: (bi, i, 0)),
                _const_spec((1, d), lambda bi, i: (0, 0)),
                _const_spec((d, QKV_GROUP), lambda bi, i: (0, group)),
                tab_spec, tab_spec, tab_spec]
    args = [x, g.reshape(1, d), w, *tables]
    if dilation > 1:
        in_specs.append(_const_spec((tm, tm), lambda bi, i: (0, 0)))
        args.append(jnp.asarray(_residue_perm(tm, dilation), BF16))
    return pl.pallas_call(
        functools.partial(_qkv_rope_kernel, tn=tn, q_scale=q_scale, dilation=dilation),
        grid=(b, s // tm),
        in_specs=in_specs,
        out_specs=pl.BlockSpec((None, dilation, rows, QKV_GROUP), lambda bi, i: (bi, 0, i, 0)),
        out_shape=jax.ShapeDtypeStruct((b, dilation, s // dilation, QKV_GROUP), BF16),
        compiler_params=_cparams(2),
        name=f"qkv_rope_d{dilation}",
    )(*args)


def _hgrn_in_kernel(x_ref, g_ref, w_ref, lb_ref, act_ref, logf_ref, *, tn, layer):
    h = _rms(x_ref[...], g_ref[...], NORM_EPS).astype(BF16)
    logits = lb_ref[...]
    e = jnp.exp(logits - jnp.max(logits, axis=0, keepdims=True))
    p = e / jnp.sum(e, axis=0, keepdims=True)
    cum = p[0:1]
    for r in range(1, layer + 1):
        cum = cum + p[r:r + 1]
    lb = cum - p[0:1]
    log_lb = jnp.log(lb)
    log_1m_lb = jnp.log1p(-lb)

    for cn in range(5 * D_MODEL // tn):
        col = cn * tn
        part, off = col // D_MODEL, col % D_MODEL
        acc = jnp.dot(h, w_ref[:, col:col + tn], preferred_element_type=F32)
        if part == 0:
            act_ref[:, off:off + tn] = (acc * jax.nn.sigmoid(acc) * F32(C_DIM ** -0.5)).astype(act_ref.dtype)
        elif part in (1, 2):
            log_sig = jnp.minimum(acc, 0.0) - _softplus_neg_abs(acc)
            lo = (part - 1) * D_MODEL + off
            logf_ref[:, lo:lo + tn] = _logaddexp(log_lb[:, off:off + tn], log_1m_lb[:, off:off + tn] + log_sig)
        elif part == 3:
            act_ref[:, D_MODEL + off:D_MODEL + off + tn] = acc.astype(act_ref.dtype)
        else:
            act_ref[:, 2 * D_MODEL + off:2 * D_MODEL + off + tn] = (acc * jax.nn.sigmoid(acc)).astype(act_ref.dtype)


def _hgrn_in(x, g, w, lb_logits, layer, *, tm=512, tn=512):
    m, d = x.shape
    return pl.pallas_call(
        functools.partial(_hgrn_in_kernel, tn=tn, layer=layer),
        grid=(m // tm,),
        in_specs=[pl.BlockSpec((tm, d), lambda i: (i, 0)),
                  _const_spec((1, d), lambda i: (0, 0)),
                  _const_spec((d, 5 * D_MODEL), lambda i: (0, 0)),
                  _const_spec((DEPTH, D_MODEL), lambda i: (0, 0))],
        out_specs=[pl.BlockSpec((tm, 3 * D_MODEL), lambda i: (i, 0)),
                   pl.BlockSpec((tm, 2 * D_MODEL), lambda i: (i, 0))],
        out_shape=[jax.ShapeDtypeStruct((m, 3 * D_MODEL), BF16),
                   jax.ShapeDtypeStruct((m, 2 * D_MODEL), F32)],
        compiler_params=_cparams(1),
        name="hgrn_in",
    )(x, g.reshape(1, d), w, lb_logits)


def _mm_norm_res_kernel(a_ref, w_ref, g_ref, x_ref, o_ref, *, sub):
    for rb in range(a_ref.shape[0] // sub):
        rows = slice(rb * sub, (rb + 1) * sub)
        h = jnp.dot(a_ref[rows, :], w_ref[...], preferred_element_type=F32)
        o_ref[rows, :] = x_ref[rows, :] + _rms(h, g_ref[...], NORM_EPS)


def _mm_norm_res(a, w, g, x, *, tm=1024, sub=256):
    m, k = a.shape
    d = w.shape[1]
    return pl.pallas_call(
        functools.partial(_mm_norm_res_kernel, sub=sub),
        grid=(m // tm,),
        in_specs=[pl.BlockSpec((tm, k), lambda i: (i, 0)),
                  _const_spec((k, d), lambda i: (0, 0)),
                  _const_spec((1, d), lambda i: (0, 0)),
                  pl.BlockSpec((tm, d), lambda i: (i, 0))],
        out_specs=pl.BlockSpec((tm, d), lambda i: (i, 0)),
        out_shape=jax.ShapeDtypeStruct((m, d), F32),
        compiler_params=_cparams(1),
        name="mm_norm_res",
    )(a, w, g.reshape(1, d), x)


def _ffn_in_kernel(x_ref, xp_ref, xn_ref, g_ref, w_ref, cw_ref, cb_ref, o_ref, h_scr, u_scr, *, tm, tn, seq):
    i = pl.program_id(0)
    n_tiles = pl.num_programs(0) - 1
    halo = BF16_ROWS
    rows = tm + 2 * halo
    n_chunks = D_FF // tn
    slot_new = i % 2
    slot_old = 1 - slot_new

    @pl.when(i == 0)
    def _():
        u_scr[...] = jnp.zeros(u_scr.shape, F32)

    t = jnp.minimum(i, n_tiles - 1)
    g = g_ref[...]
    first = (t * tm) % seq == 0
    last = ((t + 1) * tm) % seq == 0
    hp = _rms(xp_ref[...], g, NORM_EPS)
    hn = _rms(xn_ref[...], g, NORM_EPS)
    h_scr[0:halo] = jnp.where(first, 0.0, hp).astype(BF16)
    h_scr[halo:halo + tm] = _rms(x_ref[...], g, NORM_EPS).astype(BF16)
    h_scr[halo + tm:rows] = jnp.where(last, 0.0, hn).astype(BF16)
    def matmul(cn):
        u_scr[slot_new, cn] = jnp.dot(h_scr[...], w_ref[:, cn * tn:(cn + 1) * tn], preferred_element_type=F32)

    def conv(cn, scale=None):
        cw = cw_ref[:, cn * tn:(cn + 1) * tn]
        cb = cb_ref[:, cn * tn:(cn + 1) * tn]
        if scale is not None:
            cw, cb = cw * scale, cb * scale
        return (cb + u_scr[slot_old, cn, halo - 1:halo - 1 + tm, :] * cw[0:1]
                + u_scr[slot_old, cn, halo:halo + tm, :] * cw[1:2]
                + u_scr[slot_old, cn, halo + 1:halo + 1 + tm, :] * cw[2:3])

    k0 = math.sqrt(2.0 / math.pi)
    for cn in range(n_chunks):
        matmul(2 * cn)
        matmul(2 * cn + 1)
        a = conv(cn)
        half_b = conv(n_chunks + cn, F32(0.5))
        th = jnp.tanh(a * (F32(k0) + F32(k0 * 0.044715) * (a * a)))
        o_ref[:, cn * tn:(cn + 1) * tn] = ((a * half_b) * (1.0 + th)).astype(o_ref.dtype)


def _ffn_in(x, g, w, conv_w, conv_b, seq, *, tm=512, tn=256):
    m, d = x.shape
    halo = BF16_ROWS
    hb = tm // halo
    last_blk = m // halo - 1
    n_tiles = m // tm
    cur = lambda i: jnp.minimum(i, n_tiles - 1)
    return pl.pallas_call(
        functools.partial(_ffn_in_kernel, tm=tm, tn=tn, seq=seq),
        grid=(n_tiles + 1,),
        in_specs=[pl.BlockSpec((tm, d), lambda i: (cur(i), 0)),
                  pl.BlockSpec((halo, d), lambda i: (jnp.maximum(cur(i) * hb - 1, 0), 0)),
                  pl.BlockSpec((halo, d), lambda i: (jnp.minimum((cur(i) + 1) * hb, last_blk), 0)),
                  _const_spec((1, d), lambda i: (0, 0)),
                  _const_spec((d, 2 * D_FF), lambda i: (0, 0)),
                  _const_spec((3, 2 * D_FF), lambda i: (0, 0)),
                  _const_spec((1, 2 * D_FF), lambda i: (0, 0))],
        out_specs=pl.BlockSpec((tm, D_FF), lambda i: (jnp.maximum(i - 1, 0), 0)),
        out_shape=jax.ShapeDtypeStruct((m, D_FF), BF16),
        scratch_shapes=[pltpu.VMEM((tm + 2 * halo, d), BF16),
                        pltpu.VMEM((2, 2 * D_FF // tn, tm + 2 * halo, tn), F32)],
        compiler_params=_cparams(1),
        name="ffn_in",
    )(x, x, x, g.reshape(1, d), w, conv_w, conv_b.reshape(1, -1))


def _diff_attn_kernel(q_ref, k_ref, v_ref, lam_ref, g_ref, o_ref, q2_scr, vx_scr, m_scr, acc_scr, s_scr,
                      *, tq, tc, unroll, lam_init):
    seq = k_ref.shape[0]
    q = q_ref[...]
    lane = lax.broadcasted_iota(jnp.int32, q.shape, 1)
    zero = jnp.zeros_like(q)
    q2_scr[0:tq] = jnp.where(lane < A_HEAD_DIM, q, zero)
    q2_scr[tq:2 * tq] = jnp.where(lane >= A_HEAD_DIM, q, zero)
    m_scr[...] = jnp.full(m_scr.shape, -jnp.inf, F32)
    acc_scr[...] = jnp.zeros(acc_scr.shape, F32)

    @pl.when(pl.program_id(2) == 0)
    def _():
        vx_scr[:, 0:A_V_DIM] = v_ref[...]
        vx_scr[:, A_V_DIM:2 * A_V_DIM] = jnp.ones((seq, A_V_DIM), BF16)

    def scores(c, slot):
        r0 = c * tc if isinstance(c, int) else pl.multiple_of(c * tc, tc)
        s_scr[slot] = lax.dot_general(q2_scr[...], k_ref[pl.ds(r0, tc), :], (((1,), (1,)), ((), ())),
                                      preferred_element_type=F32)

    def softmax_pv(c, slot):
        r0 = c * tc if isinstance(c, int) else pl.multiple_of(c * tc, tc)
        slabs = [s_scr[slot, :, j * LANES:(j + 1) * LANES] for j in range(tc // LANES)]
        mx = slabs[0]
        for sl in slabs[1:]:
            mx = jnp.maximum(mx, sl)
        m_prev = m_scr[...]
        m_new = jnp.maximum(m_prev, jnp.max(mx, axis=1, keepdims=True))
        alpha = jnp.exp2(m_prev - m_new)
        p = jnp.concatenate([jnp.exp2(sl - m_new).astype(BF16) for sl in slabs], axis=1)
        pv = jnp.dot(p, vx_scr[pl.ds(r0, tc), :], preferred_element_type=F32)
        acc_scr[:, 0:A_V_DIM] = alpha * acc_scr[:, 0:A_V_DIM] + pv[:, 0:A_V_DIM]
        acc_scr[:, A_V_DIM:2 * A_V_DIM] = alpha * acc_scr[:, A_V_DIM:2 * A_V_DIM] + pv[:, A_V_DIM:2 * A_V_DIM]
        m_scr[...] = m_new

    n_chunks = seq // tc

    def body(it, carry):
        for u in range(unroll):
            c = 2 * (unroll * it + u)
            scores(c + 1, 1)
            softmax_pv(c, 0)
            scores(jnp.minimum(c + 2, n_chunks - 1), 0)
            softmax_pv(c + 1, 1)
        return carry

    scores(0, 0)
    lax.fori_loop(0, n_chunks // (2 * unroll), body, 0)

    o = acc_scr[:, 0:A_V_DIM] / acc_scr[:, A_V_DIM:2 * A_V_DIM]
    lv = lam_ref[...]
    lam = (jnp.exp(jnp.sum(lv[0:1] * lv[1:2], axis=1, keepdims=True))
           - jnp.exp(jnp.sum(lv[2:3] * lv[3:4], axis=1, keepdims=True)) + F32(lam_init))
    od = o[0:tq] - lam * o[tq:2 * tq]
    y = _rms(od, g_ref[...], A_SUBLN_EPS) * F32(1.0 - lam_init)
    o_ref[...] = y.astype(o_ref.dtype)


def _diff_attn(qkv, lam_vecs, subln_g, lam_init, *, tq=1024, tc=512, unroll=2):
    b, s, _ = qkv.shape
    tq = min(tq, s)
    blk = lambda rows, off: pl.BlockSpec((None, rows, LANES), off)
    return pl.pallas_call(
        functools.partial(_diff_attn_kernel, tq=tq, tc=tc, unroll=unroll, lam_init=lam_init),
        grid=(b, A_HEADS, s // tq),
        in_specs=[blk(tq, lambda bi, h, i: (bi, i, h)),
                  blk(s, lambda bi, h, i: (bi, 0, A_HEADS + h)),
                  blk(s, lambda bi, h, i: (bi, 0, 2 * A_HEADS + h)),
                  pl.BlockSpec((4, A_HEAD_DIM), lambda bi, h, i: (0, 0)),
                  pl.BlockSpec((1, A_V_DIM), lambda bi, h, i: (0, 0))],
        out_specs=blk(tq, lambda bi, h, i: (bi, i, h)),
        out_shape=jax.ShapeDtypeStruct((b, s, A_HEADS * A_V_DIM), BF16),
        scratch_shapes=[pltpu.VMEM((2 * tq, LANES), BF16),
                        pltpu.VMEM((s, 2 * A_V_DIM), BF16),
                        pltpu.VMEM((2 * tq, LANES), F32),
                        pltpu.VMEM((2 * tq, 2 * A_V_DIM), F32),
                        pltpu.VMEM((2, 2 * tq, tc), F32)],
        compiler_params=_cparams(3),
        name="diff_attn",
    )(qkv, qkv, qkv, lam_vecs, subln_g.reshape(1, A_V_DIM))


def _dil_attn_kernel(q_ref, k_ref, kp_ref, kn_ref, v_ref, vp_ref, vn_ref, o_ref, lse_ref, kbuf, vbuf,
                     *, tl, length):
    hw = B_HALF_W
    t0 = pl.program_id(2) * tl

    for buf, main, prev, nxt in ((kbuf, k_ref, kp_ref, kn_ref), (vbuf, v_ref, vp_ref, vn_ref)):
        buf[0:hw] = prev[...]
        buf[hw:hw + tl] = main[...]
        buf[hw + tl:2 * hw + tl] = nxt[...]

    a_io = lax.broadcasted_iota(jnp.int32, (B_QBLK, B_KBLK), 0)
    c_io = lax.broadcasted_iota(jnp.int32, (B_QBLK, B_KBLK), 1)
    band = (c_io >= a_io) & (c_io <= a_io + 2 * hw)
    lane = lax.broadcasted_iota(jnp.int32, (B_QBLK, LANES), 1)
    lane_q = lane

    def q_block(jb, carry):
        r0 = pl.multiple_of(jb * B_QBLK, B_QBLK)
        ki = t0 + r0 - hw + c_io
        valid = band & (ki >= 0) & (ki < length)
        valid2 = jnp.concatenate([valid, valid], axis=0)
        lse_tile = jnp.zeros((B_QBLK, LANES), F32)
        for hp in range(B_HEADS // 2):
            cols = slice(hp * LANES, (hp + 1) * LANES)
            q = q_ref[pl.ds(r0, B_QBLK), cols]
            zero = jnp.zeros_like(q)
            q2 = jnp.concatenate([jnp.where(lane_q < B_HEAD_DIM, q, zero),
                                  jnp.where(lane_q >= B_HEAD_DIM, q, zero)], axis=0)
            s = lax.dot_general(q2, kbuf[pl.ds(r0, B_KBLK), cols], (((1,), (1,)), ((), ())),
                                preferred_element_type=F32)
            s = jnp.where(valid2, s, -jnp.inf)
            m = jnp.max(s, axis=1, keepdims=True)
            p = jnp.exp(s - m)
            l = jnp.sum(p, axis=1, keepdims=True)
            pv = jnp.dot(p.astype(BF16), vbuf[pl.ds(r0, B_KBLK), cols], preferred_element_type=F32) / l
            o_ref[pl.ds(r0, B_QBLK), cols] = jnp.where(lane < B_HEAD_DIM, pv[0:B_QBLK],
                                                       pv[B_QBLK:2 * B_QBLK]).astype(o_ref.dtype)
            lse = m + jnp.log(l)
            lse_tile = jnp.where(lane == 2 * hp, lse[0:B_QBLK], lse_tile)
            lse_tile = jnp.where(lane == 2 * hp + 1, lse[B_QBLK:2 * B_QBLK], lse_tile)
        lse_ref[pl.ds(r0, B_QBLK), :] = lse_tile
        return carry

    lax.fori_loop(0, tl // B_QBLK, q_block, 0)


def _dil_attn_group(qkv, *, tl=512):
    b, dilation, length, _ = qkv.shape
    tl = min(tl, length)
    hw = B_HALF_W
    width = B_HEADS * B_HEAD_DIM
    per_tile = tl // hw
    n_hw = length // hw
    main = lambda part: pl.BlockSpec((None, None, tl, width), lambda bi, r, i: (bi, r, i, part))
    before = lambda part: pl.BlockSpec((None, None, hw, width),
                                       lambda bi, r, i: (bi, r, jnp.maximum(i * per_tile - 1, 0), part))
    after = lambda part: pl.BlockSpec((None, None, hw, width),
                                      lambda bi, r, i: (bi, r, jnp.minimum((i + 1) * per_tile, n_hw - 1), part))
    return pl.pallas_call(
        functools.partial(_dil_attn_kernel, tl=tl, length=length),
        grid=(b, dilation, length // tl),
        in_specs=[main(0), main(1), before(1), after(1), main(2), before(2), after(2)],
        out_specs=[pl.BlockSpec((None, None, tl, width), lambda bi, r, i: (bi, r, i, 0)),
                   pl.BlockSpec((None, None, tl, LANES), lambda bi, r, i: (bi, r, i, 0))],
        out_shape=[jax.ShapeDtypeStruct((b, dilation, length, width), BF16),
                   jax.ShapeDtypeStruct((b, dilation, length, LANES), F32)],
        scratch_shapes=[pltpu.VMEM((tl + 2 * hw, width), BF16), pltpu.VMEM((tl + 2 * hw, width), BF16)],
        compiler_params=_cparams(3),
        name=f"dil_attn_d{dilation}",
    )(*([qkv] * 7))


def _split3(x):
    hi = x.astype(BF16)
    r1 = x - hi.astype(F32)
    mid = r1.astype(BF16)
    lo = (r1 - mid.astype(F32)).astype(BF16)
    return hi, mid, lo


def _dil_merge_kernel(*refs, n_groups):
    o_refs = refs[:n_groups]
    l_refs = refs[n_groups:2 * n_groups]
    perm_refs = refs[2 * n_groups:3 * n_groups - 1]
    expand_ref, out_ref = refs[3 * n_groups - 1:]
    t = out_ref.shape[0]
    outs, lses = [], []
    for gi in range(n_groups):
        o = o_refs[gi][...].reshape(t, o_refs[gi].shape[-1])
        lse = l_refs[gi][...].reshape(t, LANES)
        if gi == 0:
            outs.append(o.astype(F32))
            lses.append(lse)
        else:
            perm = perm_refs[gi - 1][...]
            outs.append(jnp.dot(perm, o, preferred_element_type=F32))
            lses.append(sum(jnp.dot(perm, piece, preferred_element_type=F32) for piece in _split3(lse)))
    m = functools.reduce(jnp.maximum, lses)
    es = [jnp.exp(l - m) for l in lses]
    den = functools.reduce(lambda a, b: a + b, es)
    acc = jnp.zeros((t, out_ref.shape[1]), F32)
    for gi in range(n_groups):
        w = es[gi] / den
        hi = w.astype(BF16)
        lo = (w - hi.astype(F32)).astype(BF16)
        wide = (jnp.dot(hi, expand_ref[...], preferred_element_type=F32)
                + jnp.dot(lo, expand_ref[...], preferred_element_type=F32))
        acc = acc + wide * outs[gi]
    out_ref[...] = acc.astype(out_ref.dtype)


def _dil_merge(outs, lses, *, t=512):
    b, _, s, width = outs[0].shape
    n_groups = len(outs)
    in_specs, args = [], []
    for arrs, lanes in ((outs, width), (lses, LANES)):
        for a in arrs:
            d = a.shape[1]
            in_specs.append(pl.BlockSpec((None, d, t // d, lanes), lambda bi, i: (bi, 0, i, 0)))
            args.append(a)
    for a in outs[1:]:
        in_specs.append(pl.BlockSpec((t, t), lambda bi, i: (0, 0)))
        args.append(jnp.asarray(_residue_perm(t, a.shape[1]).T, BF16))
    expand = np.zeros((LANES, width), np.float32)
    expand[np.arange(width) // B_HEAD_DIM, np.arange(width)] = 1.0
    in_specs.append(pl.BlockSpec((LANES, width), lambda bi, i: (0, 0)))
    args.append(jnp.asarray(expand, BF16))
    return pl.pallas_call(
        functools.partial(_dil_merge_kernel, n_groups=n_groups),
        grid=(b, s // t),
        in_specs=in_specs,
        out_specs=pl.BlockSpec((None, t, width), lambda bi, i: (bi, i, 0)),
        out_shape=jax.ShapeDtypeStruct((b, s, width), BF16),
        compiler_params=_cparams(2),
        name="dil_merge",
    )(*args)


def _scan_constants(reverse):
    c = C_CHUNK
    t = np.arange(c)[:, None]
    u = np.arange(c)[None, :]
    if not reverse:
        mats = [u <= t, u > t]
    else:
        mats = [u >= t, u < t]
    masks = []
    for hs in C_LEVELS:
        base = (t // (2 * hs)) * (2 * hs)
        mid = base + hs - 1
        same = (u // (2 * hs)) == (t // (2 * hs))
        if not reverse:
            m = np.where(t > mid, (u > mid) & (u <= t), (u > t) & (u <= mid))
            masks.append(same & (t > mid) & (u <= mid))
        else:
            m = np.where(t <= mid, (u >= t) & (u <= mid), (u > mid) & (u < t))
            masks.append(same & (t <= mid) & (u > mid))
        mats.append(m & same)
    stack = np.concatenate([np.asarray(m, np.float32) for m in mats], axis=0)
    return jnp.asarray(stack, BF16), jnp.asarray(np.stack(masks).astype(np.float32))


def _hgrn_scan_kernel(*refs, tt, hp, reverse, final):
    q_ref, v_ref, lf_ref, mstack_ref, masks_ref = refs[:5]
    refs = refs[5:]
    if final:
        ofw_ref, gate_ref, gn_ref = refs[:3]
        refs = refs[3:]
    o_ref, st_scr, g_scr, k_scr = refs
    c = C_CHUNK
    n_chunks = tt // c
    nt_contract = (((1,), (1,)), ((), ()))

    @pl.when(pl.program_id(2) == 0)
    def _():
        st_scr[...] = jnp.zeros(st_scr.shape, F32)

    row8 = lax.broadcasted_iota(jnp.int32, (C_DIAG, C_DIM), 0)
    lane_c = lax.broadcasted_iota(jnp.int32, (C_DIAG, c), 1)

    def head_chunk(r0, hi):
        cols = slice(hi * C_DIM, (hi + 1) * C_DIM)
        lf = lf_ref[pl.ds(r0, c), cols] * F32(LOG2E)
        top = lf.astype(BF16)
        low = (lf - top.astype(F32)).astype(BF16)
        d2 = jnp.dot(mstack_ref[...], jnp.concatenate([top, low], axis=1), preferred_element_type=F32)
        d = d2[:, 0:C_DIM] + d2[:, C_DIM:2 * C_DIM]
        gq = d[0:c]
        gk = d[c:2 * c]

        q = q_ref[pl.ds(r0, c), cols].astype(F32)
        v = v_ref[pl.ds(r0, c), cols]
        kk = 1.0 - jnp.exp2(lf)
        st = st_scr[hi]
        g_scr[hi] = gq
        k_scr[hi] = kk

        o = lax.dot_general((q * jnp.exp2(gq)).astype(BF16), st.astype(BF16), nt_contract,
                            preferred_element_type=F32)

        a = jnp.zeros((c, c), F32)
        for li in range(len(C_LEVELS)):
            e = jnp.exp2(-jnp.abs(d[(2 + li) * c:(3 + li) * c]))
            pq = (q * e).astype(BF16)
            pk = (kk * e).astype(BF16)
            a = a + lax.dot_general(pq, pk, nt_contract, preferred_element_type=F32) * masks_ref[li]

        blocks = []
        for bi in range(c // C_DIAG):
            rows = slice(bi * C_DIAG, (bi + 1) * C_DIAG)
            gb, qb = gq[rows], q[rows]
            ablk = jnp.zeros((C_DIAG, c), F32)
            for si in range(C_DIAG):
                row = bi * C_DIAG + si
                g_row = jnp.broadcast_to(g_scr[hi, row:row + 1, :], (C_DIAG, C_DIM))
                k_row = jnp.broadcast_to(k_scr[hi, row:row + 1, :], (C_DIAG, C_DIM))
                keep = (row8 <= si) if reverse else (row8 >= si)
                dec = jnp.exp2(jnp.where(keep, gb - g_row, -jnp.inf))
                sc = jnp.sum(qb * dec * k_row, axis=1, keepdims=True)
                ablk = jnp.where(lane_c == row, sc, ablk)
            blocks.append(ablk)
        a = a + jnp.concatenate(blocks, axis=0)

        o = o + jnp.dot(a.astype(BF16), v, preferred_element_type=F32)

        kd = (kk * jnp.exp2(gk)).astype(BF16)
        upd = lax.dot_general(v, kd, (((0,), (0,)), ((), ())), preferred_element_type=F32)
        g_tot = gq[0:1] if reverse else gq[c - 1:c]
        st_scr[hi] = st * jnp.exp2(g_tot) + upd

        if final:
            tot = ofw_ref[pl.ds(r0, c), cols] + o
            y = _rms(tot, gn_ref[...], NORM_EPS) * gate_ref[pl.ds(r0, c), cols].astype(F32)
            o_ref[pl.ds(r0, c), cols] = y.astype(o_ref.dtype)
        else:
            o_ref[pl.ds(r0, c), cols] = o

    def chunk(ci, carry):
        cidx = (n_chunks - 1 - ci) if reverse else ci
        r0 = pl.multiple_of(cidx * c, c)
        for hi in range(hp):
            head_chunk(r0, hi)
        return carry

    lax.fori_loop(0, n_chunks, chunk, 0)


def _hgrn_scan(act, logf, reverse, o_fw=None, gnorm_g=None, *, tt=512, hp=8):
    b, s, _ = act.shape
    tt = min(tt, s)
    nt = s // tt
    final = o_fw is not None
    mstack, masks = _scan_constants(reverse)
    groups = C_HEADS // hp
    tmap = (lambda t: nt - 1 - t) if reverse else (lambda t: t)
    blk = lambda part: pl.BlockSpec((None, tt, hp * C_DIM), lambda bi, hg, t: (bi, tmap(t), part * groups + hg))
    in_specs = [blk(0), blk(1), blk(1 if reverse else 0),
                _const_spec(mstack.shape, lambda bi, hg, t: (0, 0)),
                _const_spec(masks.shape, lambda bi, hg, t: (0, 0, 0))]
    args = [act, act, logf, mstack, masks]
    if final:
        in_specs += [blk(0), blk(2), _const_spec((1, C_DIM), lambda bi, hg, t: (0, 0))]
        args += [o_fw, act, gnorm_g.reshape(1, C_DIM)]
    return pl.pallas_call(
        functools.partial(_hgrn_scan_kernel, tt=tt, hp=hp, reverse=reverse, final=final),
        grid=(b, groups, nt),
        in_specs=in_specs,
        out_specs=blk(0),
        out_shape=jax.ShapeDtypeStruct((b, s, C_HEADS * C_DIM), BF16 if final else F32),
        scratch_shapes=[pltpu.VMEM((hp, C_DIM, C_DIM), F32),
                        pltpu.VMEM((hp, C_CHUNK, C_DIM), F32),
                        pltpu.VMEM((hp, C_CHUNK, C_DIM), F32)],
        compiler_params=_cparams(3),
        name="hgrn_scan_bw" if reverse else "hgrn_scan_fw",
    )(*args)


def _diff_lambda_init(layer):
    return 0.8 - 0.6 * math.exp(-0.3 * layer)


def _trunk(x3, p):
    b, s, d = x3.shape
    x = x3.reshape(b * s, d)
    tables = _rope_tables(s)
    for layer in range(DEPTH):
        kind = layer % N_MIXERS
        j = layer // N_MIXERS
        g = p["norm_g"][layer]
        x3 = x.reshape(b, s, d)
        if kind == 0:
            qkv = _qkv_rope(x3, g[0], p["a_w_qkv"][j], tables, A_HEAD_DIM ** -0.5 * LOG2E)
            o = _diff_attn(qkv.reshape(b, s, -1), p["a_lambda"][j], p["a_subln_g"][j], _diff_lambda_init(layer))
            w_o = p["a_w_o"][j]
        elif kind == 1:
            outs, lses = [], []
            for group, (_, dilation) in enumerate(B_PATTERNS):
                qkv = _qkv_rope(x3, g[0], p["b_w_qkv"][j], tables, B_HEAD_DIM ** -0.5, group, dilation)
                o_g, lse_g = _dil_attn_group(qkv)
                outs.append(o_g)
                lses.append(lse_g)
            o = _dil_merge(outs, lses)
            w_o = p["b_w_o"][j]
        else:
            act, logf = _hgrn_in(x, g[0], p["c_w_in"][j], p["c_lb_logits"], layer)
            act, logf = act.reshape(b, s, -1), logf.reshape(b, s, -1)
            o_fw = _hgrn_scan(act, logf, False)
            o = _hgrn_scan(act, logf, True, o_fw, p["c_gnorm_g"][j])
            w_o = p["c_w_o"][j]
        x = _mm_norm_res(o.reshape(b * s, -1), w_o, g[1], x)
        u = _ffn_in(x, g[2], p["f_w_in"][layer], p["f_conv_w"][layer], p["f_conv_b"][layer], s)
        x = _mm_norm_res(u, p["f_w_out"][layer], g[3], x)
    return x.reshape(b, s, d)


def kernel(x_prompt, x_sample, norm_g, a_w_qkv, a_lambda, a_subln_g, a_w_o, b_w_qkv, b_w_o, c_w_in,
           c_lb_logits, c_gnorm_g, c_w_o, f_w_in, f_conv_w, f_conv_b, f_w_out):
    bf = lambda w: w.astype(BF16)
    p = dict(norm_g=norm_g, a_w_qkv=bf(a_w_qkv), a_lambda=a_lambda, a_subln_g=a_subln_g, a_w_o=bf(a_w_o),
             b_w_qkv=bf(b_w_qkv), b_w_o=bf(b_w_o), c_w_in=bf(c_w_in), c_lb_logits=c_lb_logits,
             c_gnorm_g=c_gnorm_g, c_w_o=bf(c_w_o), f_w_in=bf(f_w_in), f_conv_w=f_conv_w,
             f_conv_b=f_conv_b, f_w_out=bf(f_w_out))
    return _trunk(x_prompt, p), _trunk(x_sample, p)
```

```python
import functools
import math

import numpy as np
import jax
import jax.numpy as jnp
from jax import lax
from jax.experimental import pallas as pl
from jax.experimental.pallas import tpu as pltpu

F32 = jnp.float32
BF16 = jnp.bfloat16

D_MODEL = 1024
DEPTH = 4
N_MIXERS = 3
ROPE_THETA = 500000.0
NORM_EPS = 1e-6

A_HEADS = 8
A_HEAD_DIM = 64
A_V_DIM = 128
A_SUBLN_EPS = 1e-5

B_HEADS = 16
B_HEAD_DIM = 64
B_PATTERNS = ((128, 1), (512, 4), (2048, 16))
B_HALF_W = 64
B_QBLK = 2 * B_HALF_W
B_KBLK = 4 * B_HALF_W

C_HEADS = 8
C_DIM = 128
C_CHUNK = 128
C_LEVELS = (64, 32, 16, 8)
C_DIAG = 8

D_FF = 2816
FFN_RUN = 128
QKV_GROUP = 3 * D_MODEL
ROPE_ROT = 16
LANES = 128
LOG2E = math.log2(math.e)
BF16_ROWS = 16
SUBLANES = 8

V7X_VMEM_LIMIT = 56 * 1024 * 1024


def _cparams(n_axes):
    return pltpu.CompilerParams(dimension_semantics=("arbitrary",) * n_axes,
                                vmem_limit_bytes=V7X_VMEM_LIMIT)


def _const_spec(shape, index_map):
    return pl.BlockSpec(shape, index_map, pipeline_mode=pl.Buffered(1))


def _rms(x, g, eps):
    ms = jnp.mean(x * x, axis=-1, keepdims=True)
    return x * lax.rsqrt(ms + eps) * g


def _softplus_neg_abs(d):
    return jnp.log(1.0 + jnp.exp(-jnp.abs(d)))


def _logaddexp(a, b):
    return jnp.maximum(a, b) + _softplus_neg_abs(a - b)


def _rope_tables(seq):
    half = ROPE_ROT // 2
    inv_freq = ROPE_THETA ** (-jnp.arange(half, dtype=F32) / half)
    ang = jnp.arange(seq, dtype=F32)[:, None] * inv_freq[None, :]
    cos, sin = jnp.cos(ang), jnp.sin(ang)
    rest = B_HEAD_DIM - ROPE_ROT
    c = jnp.concatenate([cos, cos, jnp.ones((seq, rest), F32)], axis=1)
    s1 = jnp.concatenate([-sin, jnp.zeros((seq, rest + half), F32)], axis=1)
    s2 = jnp.concatenate([jnp.zeros((seq, half), F32), sin, jnp.zeros((seq, rest), F32)], axis=1)
    rep = LANES // B_HEAD_DIM
    return tuple(jnp.tile(t, (1, rep)) for t in (c, s1, s2))


def _residue_perm(tm, dilation):
    rows = tm // dilation
    rho = np.arange(tm)
    tok = (rho % rows) * dilation + rho // rows
    perm = np.zeros((tm, tm), np.float32)
    perm[rho, tok] = 1.0
    return perm


def _qkv_rope_kernel(*refs, tn, q_scale, dilation):
    if dilation > 1:
        x_ref, g_ref, w_ref, c_ref, s1_ref, s2_ref, perm_ref, o_ref = refs
    else:
        x_ref, g_ref, w_ref, c_ref, s1_ref, s2_ref, o_ref = refs
    tm = x_ref.shape[0]
    rows = tm // dilation

    h = _rms(x_ref[...], g_ref[...], NORM_EPS).astype(BF16)
    if dilation > 1:
        h = jnp.dot(perm_ref[...], h, preferred_element_type=F32).astype(BF16)

    def put(col, val):
        for rr in range(dilation):
            o_ref[rr, :, col:col + LANES] = val[rr * rows:(rr + 1) * rows].astype(o_ref.dtype)

    c, s1, s2 = c_ref[...], s1_ref[...], s2_ref[...]
    cq, s1q, s2q = c * F32(q_scale), s1 * F32(q_scale), s2 * F32(q_scale)
    for cn in range(QKV_GROUP // tn):
        acc = jnp.dot(h, w_ref[:, cn * tn:(cn + 1) * tn], preferred_element_type=F32)
        for cb in range(tn // LANES):
            col = cn * tn + cb * LANES
            a = acc[:, cb * LANES:(cb + 1) * LANES]
            if col < 2 * D_MODEL:
                tc, t1, t2 = (cq, s1q, s2q) if col < D_MODEL else (c, s1, s2)
                a = a * tc + pltpu.roll(a, LANES - ROPE_ROT // 2, 1) * t1 + pltpu.roll(a, ROPE_ROT // 2, 1) * t2
            put(col, a)


def _qkv_rope(x, g, w, tables, q_scale, group=0, dilation=1, *, tm=512, tn=512):
    b, s, d = x.shape
    rows = tm // dilation
    if dilation > 1:
        tables = tuple(t.reshape(s // tm, rows, dilation, LANES).transpose(0, 2, 1, 3).reshape(s, LANES)
                       for t in tables)
    tab_spec = pl.BlockSpec((tm, LANES), lambda bi, i: (i, 0))
    in_specs = [pl.BlockSpec((None, tm, d), lambda bi, i: (bi, i, 0)),
                _const_spec((1, d), lambda bi, i: (0, 0)),
                _const_spec((d, QKV_GROUP), lambda bi, i: (0, group)),
                tab_spec, tab_spec, tab_spec]
    args = [x, g.reshape(1, d), w, *tables]
    if dilation > 1:
        in_specs.append(_const_spec((tm, tm), lambda bi, i: (0, 0)))
        args.append(jnp.asarray(_residue_perm(tm, dilation), BF16))
    return pl.pallas_call(
        functools.partial(_qkv_rope_kernel, tn=tn, q_scale=q_scale, dilation=dilation),
        grid=(b, s // tm),
        in_specs=in_specs,
        out_specs=pl.BlockSpec((None, dilation, rows, QKV_GROUP), lambda bi, i: (bi, 0, i, 0)),
        out_shape=jax.ShapeDtypeStruct((b, dilation, s // dilation, QKV_GROUP), BF16),
        compiler_params=_cparams(2),
        name=f"qkv_rope_d{dilation}",
    )(*args)


def _hgrn_in_kernel(x_ref, g_ref, w_ref, lb_ref, act_ref, logf_ref, *, tn, layer):
    h = _rms(x_ref[...], g_ref[...], NORM_EPS).astype(BF16)
    logits = lb_ref[...]
    e = jnp.exp(logits - jnp.max(logits, axis=0, keepdims=True))
    p = e / jnp.sum(e, axis=0, keepdims=True)
    cum = p[0:1]
    for r in range(1, layer + 1):
        cum = cum + p[r:r + 1]
    lb = cum - p[0:1]
    log_lb = jnp.log(lb)
    log_1m_lb = jnp.log1p(-lb)

    for cn in range(5 * D_MODEL // tn):
        col = cn * tn
        part, off = col // D_MODEL, col % D_MODEL
        acc = jnp.dot(h, w_ref[:, col:col + tn], preferred_element_type=F32)
        if part == 0:
            act_ref[:, off:off + tn] = (acc * jax.nn.sigmoid(acc) * F32(C_DIM ** -0.5)).astype(act_ref.dtype)
        elif part in (1, 2):
            log_sig = jnp.minimum(acc, 0.0) - _softplus_neg_abs(acc)
            lo = (part - 1) * D_MODEL + off
            logf_ref[:, lo:lo + tn] = _logaddexp(log_lb[:, off:off + tn], log_1m_lb[:, off:off + tn] + log_sig)
        elif part == 3:
            act_ref[:, D_MODEL + off:D_MODEL + off + tn] = acc.astype(act_ref.dtype)
        else:
            act_ref[:, 2 * D_MODEL + off:2 * D_MODEL + off + tn] = (acc * jax.nn.sigmoid(acc)).astype(act_ref.dtype)


def _hgrn_in(x, g, w, lb_logits, layer, *, tm=512, tn=512):
    m, d = x.shape
    return pl.pallas_call(
        functools.partial(_hgrn_in_kernel, tn=tn, layer=layer),
        grid=(m // tm,),
        in_specs=[pl.BlockSpec((tm, d), lambda i: (i, 0)),
                  _const_spec((1, d), lambda i: (0, 0)),
                  _const_spec((d, 5 * D_MODEL), lambda i: (0, 0)),
                  _const_spec((DEPTH, D_MODEL), lambda i: (0, 0))],
        out_specs=[pl.BlockSpec((tm, 3 * D_MODEL), lambda i: (i, 0)),
                   pl.BlockSpec((tm, 2 * D_MODEL), lambda i: (i, 0))],
        out_shape=[jax.ShapeDtypeStruct((m, 3 * D_MODEL), BF16),
                   jax.ShapeDtypeStruct((m, 2 * D_MODEL), F32)],
        compiler_params=_cparams(1),
        name="hgrn_in",
    )(x, g.reshape(1, d), w, lb_logits)


def _mm_norm_res_kernel(a_ref, w_ref, g_ref, x_ref, o_ref, *, sub):
    for rb in range(a_ref.shape[0] // sub):
        rows = slice(rb * sub, (rb + 1) * sub)
        h = jnp.dot(a_ref[rows, :], w_ref[...], preferred_element_type=F32)
        o_ref[rows, :] = x_ref[rows, :] + _rms(h, g_ref[...], NORM_EPS)


def _mm_norm_res(a, w, g, x, *, tm=1024, sub=256):
    m, k = a.shape
    d = w.shape[1]
    return pl.pallas_call(
        functools.partial(_mm_norm_res_kernel, sub=sub),
        grid=(m // tm,),
        in_specs=[pl.BlockSpec((tm, k), lambda i: (i, 0)),
                  _const_spec((k, d), lambda i: (0, 0)),
                  _const_spec((1, d), lambda i: (0, 0)),
                  pl.BlockSpec((tm, d), lambda i: (i, 0))],
        out_specs=pl.BlockSpec((tm, d), lambda i: (i, 0)),
        out_shape=jax.ShapeDtypeStruct((m, d), F32),
        compiler_params=_cparams(1),
        name="mm_norm_res",
    )(a, w, g.reshape(1, d), x)


def _interleave_perm(n):
    g = n // SUBLANES
    t = np.arange(n)
    perm = np.zeros((n, n), np.float32)
    perm[(t % g) * SUBLANES + t // g, t] = 1.0
    return perm


def _ffn_in_kernel(x_ref, xp_ref, xn_ref, g_ref, w_ref, cw_ref, cb_ref, perm_ref, unperm_ref, o_ref,
                   h_scr, u_scr, *, tm, tn, seq):
    i = pl.program_id(0)
    n_tiles = pl.num_programs(0) - 1
    halo = BF16_ROWS
    rows = tm + halo
    groups = tm // SUBLANES
    n_chunks = D_FF // tn

    @pl.when(i == 0)
    def _():
        u_scr[...] = jnp.zeros(u_scr.shape, F32)

    t = jnp.minimum(i, n_tiles - 1)
    g = g_ref[...]
    first = (t * tm) % seq == 0
    last = ((t + 1) * tm) % seq == 0
    hp = pltpu.roll(_rms(xp_ref[...], g, NORM_EPS), 1, 0)
    hn = pltpu.roll(_rms(xn_ref[...], g, NORM_EPS), 1, 0)
    row = lax.broadcasted_iota(jnp.int32, hp.shape, 0)
    edge = jnp.where(row == 0, jnp.where(first, 0.0, hp), 0.0) + jnp.where(row == 1, jnp.where(last, 0.0, hn), 0.0)
    h = _rms(x_ref[...], g, NORM_EPS).astype(BF16)
    for qi in range(tm // FFN_RUN):
        run = slice(qi * FFN_RUN, (qi + 1) * FFN_RUN)
        h_scr[run] = jnp.dot(perm_ref[...], h[run], preferred_element_type=F32).astype(BF16)
    h_scr[tm:rows] = edge.astype(BF16)

    def matmul(cn, slot_new):
        u_scr[slot_new, cn] = jnp.dot(h_scr[...], w_ref[:, cn * tn:(cn + 1) * tn], preferred_element_type=F32)

    sub = lax.broadcasted_iota(jnp.int32, (SUBLANES, tn), 0)

    def conv(cn, slot_old, scale=None):
        cw = cw_ref[:, cn * tn:(cn + 1) * tn]
        cb = cb_ref[:, cn * tn:(cn + 1) * tn]
        if scale is not None:
            cw, cb = cw * scale, cb * scale
        u = u_scr[slot_old, cn, 0:tm, :].reshape(groups, SUBLANES, tn)
        ends = u_scr[slot_old, cn, tm:tm + SUBLANES, :]
        per_run = FFN_RUN // SUBLANES
        prev_parts, next_parts = [], []
        for qi in range(tm // FFN_RUN):
            lo, hi = qi * per_run, (qi + 1) * per_run
            before = ends[0:1] if qi == 0 else u[lo - 1][SUBLANES - 1:SUBLANES]
            after = ends[1:2] if hi == groups else u[hi][0:1]
            head = jnp.where(sub == 0, before, pltpu.roll(u[hi - 1], 1, 0))
            tail = jnp.where(sub == SUBLANES - 1, after, pltpu.roll(u[lo], SUBLANES - 1, 0))
            prev_parts += [head[None], u[lo:hi - 1]]
            next_parts += [u[lo + 1:hi], tail[None]]
        u_prev = jnp.concatenate(prev_parts, axis=0)
        u_next = jnp.concatenate(next_parts, axis=0)
        return (cb + u_prev * cw[0:1] + u * cw[1:2] + u_next * cw[2:3]).reshape(tm, tn)

    k0 = math.sqrt(2.0 / math.pi)

    def step(slot_new):
        slot_old = 1 - slot_new
        for cn in range(n_chunks):
            matmul(2 * cn, slot_new)
            matmul(2 * cn + 1, slot_new)
            a = conv(cn, slot_old)
            half_b = conv(n_chunks + cn, slot_old, F32(0.5))
            th = jnp.tanh(a * (F32(k0) + F32(k0 * 0.044715) * (a * a)))
            gated = ((a * half_b) * (1.0 + th)).astype(BF16)
            for qi in range(tm // FFN_RUN):
                run = slice(qi * FFN_RUN, (qi + 1) * FFN_RUN)
                o_ref[run, cn * tn:(cn + 1) * tn] = jnp.dot(unperm_ref[...], gated[run],
                                                            preferred_element_type=F32).astype(o_ref.dtype)

    for parity in range(2):
        pl.when(i % 2 == parity)(functools.partial(step, parity))


def _ffn_in(x, g, w, conv_w, conv_b, seq, *, tm=512, tn=256):
    m, d = x.shape
    halo = BF16_ROWS
    hb = tm // halo
    last_blk = m // halo - 1
    n_tiles = m // tm
    cur = lambda i: jnp.minimum(i, n_tiles - 1)
    perm = _interleave_perm(FFN_RUN)
    return pl.pallas_call(
        functools.partial(_ffn_in_kernel, tm=tm, tn=tn, seq=seq),
        grid=(n_tiles + 1,),
        in_specs=[pl.BlockSpec((tm, d), lambda i: (cur(i), 0)),
                  pl.BlockSpec((halo, d), lambda i: (jnp.maximum(cur(i) * hb - 1, 0), 0)),
                  pl.BlockSpec((halo, d), lambda i: (jnp.minimum((cur(i) + 1) * hb, last_blk), 0)),
                  _const_spec((1, d), lambda i: (0, 0)),
                  _const_spec((d, 2 * D_FF), lambda i: (0, 0)),
                  _const_spec((3, 2 * D_FF), lambda i: (0, 0)),
                  _const_spec((1, 2 * D_FF), lambda i: (0, 0)),
                  _const_spec((FFN_RUN, FFN_RUN), lambda i: (0, 0)),
                  _const_spec((FFN_RUN, FFN_RUN), lambda i: (0, 0))],
        out_specs=pl.BlockSpec((tm, D_FF), lambda i: (jnp.maximum(i - 1, 0), 0)),
        out_shape=jax.ShapeDtypeStruct((m, D_FF), BF16),
        scratch_shapes=[pltpu.VMEM((tm + halo, d), BF16),
                        pltpu.VMEM((2, 2 * D_FF // tn, tm + halo, tn), F32)],
        compiler_params=_cparams(1),
        name="ffn_in",
    )(x, x, x, g.reshape(1, d), w, conv_w, conv_b.reshape(1, -1),
      jnp.asarray(perm, BF16), jnp.asarray(perm.T, BF16))


def _diff_attn_kernel(q_ref, k_ref, v_ref, lam_ref, g_ref, o_ref, q2_scr, vx_scr, m_scr, acc_scr, s_scr,
                      *, tq, tc, unroll, lam_init):
    seq = k_ref.shape[0]
    q = q_ref[...]
    lane = lax.broadcasted_iota(jnp.int32, q.shape, 1)
    zero = jnp.zeros_like(q)
    q2_scr[0:tq] = jnp.where(lane < A_HEAD_DIM, q, zero)
    q2_scr[tq:2 * tq] = jnp.where(lane >= A_HEAD_DIM, q, zero)

    @pl.when(pl.program_id(2) == 0)
    def _():
        vx_scr[:, 0:A_V_DIM] = v_ref[...]
        vx_scr[:, A_V_DIM:2 * A_V_DIM] = jnp.ones((seq, A_V_DIM), BF16)

    def scores(c, slot):
        r0 = c * tc if isinstance(c, int) else pl.multiple_of(c * tc, tc)
        s_scr[slot] = lax.dot_general(q2_scr[...], k_ref[pl.ds(r0, tc), :], (((1,), (1,)), ((), ())),
                                      preferred_element_type=F32)

    def softmax_pv(c, slot, first=False):
        r0 = c * tc if isinstance(c, int) else pl.multiple_of(c * tc, tc)
        slabs = [s_scr[slot, :, j * LANES:(j + 1) * LANES] for j in range(tc // LANES)]
        mx = slabs[0]
        for sl in slabs[1:]:
            mx = jnp.maximum(mx, sl)
        m_cur = jnp.max(mx, axis=1, keepdims=True)
        if first:
            m_new = jnp.broadcast_to(m_cur, m_scr.shape)
        else:
            m_prev = m_scr[...]
            m_new = jnp.maximum(m_prev, m_cur)
            alpha = jnp.exp2(m_prev - m_new)
        p = jnp.concatenate([jnp.exp2(sl - m_new).astype(BF16) for sl in slabs], axis=1)
        pv = jnp.dot(p, vx_scr[pl.ds(r0, tc), :], preferred_element_type=F32)
        if first:
            acc_scr[...] = pv
        else:
            acc_scr[:, 0:A_V_DIM] = alpha * acc_scr[:, 0:A_V_DIM] + pv[:, 0:A_V_DIM]
            acc_scr[:, A_V_DIM:2 * A_V_DIM] = alpha * acc_scr[:, A_V_DIM:2 * A_V_DIM] + pv[:, A_V_DIM:2 * A_V_DIM]
        m_scr[...] = m_new

    n_chunks = seq // tc

    def pairs(p0, first):
        for u in range(unroll):
            c = 2 * (p0 + u)
            nxt = min(c + 2, n_chunks - 1) if isinstance(c, int) else jnp.minimum(c + 2, n_chunks - 1)
            scores(c + 1, 1)
            softmax_pv(c, 0, first=first and u == 0)
            scores(nxt, 0)
            softmax_pv(c + 1, 1)

    def body(it, carry):
        pairs(unroll * it, False)
        return carry

    scores(0, 0)
    pairs(0, True)
    lax.fori_loop(1, n_chunks // (2 * unroll), body, 0)

    o = acc_scr[:, 0:A_V_DIM] / acc_scr[:, A_V_DIM:2 * A_V_DIM]
    lv = lam_ref[...]
    lam = (jnp.exp(jnp.sum(lv[0:1] * lv[1:2], axis=1, keepdims=True))
           - jnp.exp(jnp.sum(lv[2:3] * lv[3:4], axis=1, keepdims=True)) + F32(lam_init))
    od = o[0:tq] - lam * o[tq:2 * tq]
    y = _rms(od, g_ref[...], A_SUBLN_EPS) * F32(1.0 - lam_init)
    o_ref[...] = y.astype(o_ref.dtype)


def _diff_attn(qkv, lam_vecs, subln_g, lam_init, *, tq=1024, tc=512, unroll=2):
    b, s, _ = qkv.shape
    tq = min(tq, s)
    blk = lambda rows, off: pl.BlockSpec((None, rows, LANES), off)
    return pl.pallas_call(
        functools.partial(_diff_attn_kernel, tq=tq, tc=tc, unroll=unroll, lam_init=lam_init),
        grid=(b, A_HEADS, s // tq),
        in_specs=[blk(tq, lambda bi, h, i: (bi, i, h)),
                  blk(s, lambda bi, h, i: (bi, 0, A_HEADS + h)),
                  blk(s, lambda bi, h, i: (bi, 0, 2 * A_HEADS + h)),
                  pl.BlockSpec((4, A_HEAD_DIM), lambda bi, h, i: (0, 0)),
                  pl.BlockSpec((1, A_V_DIM), lambda bi, h, i: (0, 0))],
        out_specs=blk(tq, lambda bi, h, i: (bi, i, h)),
        out_shape=jax.ShapeDtypeStruct((b, s, A_HEADS * A_V_DIM), BF16),
        scratch_shapes=[pltpu.VMEM((2 * tq, LANES), BF16),
                        pltpu.VMEM((s, 2 * A_V_DIM), BF16),
                        pltpu.VMEM((2 * tq, LANES), F32),
                        pltpu.VMEM((2 * tq, 2 * A_V_DIM), F32),
                        pltpu.VMEM((2, 2 * tq, tc), F32)],
        compiler_params=_cparams(3),
        name="diff_attn",
    )(qkv, qkv, qkv, lam_vecs, subln_g.reshape(1, A_V_DIM))


def _dil_attn_kernel(q_ref, k_ref, kp_ref, kn_ref, v_ref, vp_ref, vn_ref, o_ref, lse_ref, kbuf, vbuf,
                     *, tl, length):
    hw = B_HALF_W
    t0 = pl.program_id(2) * tl

    for buf, main, prev, nxt in ((kbuf, k_ref, kp_ref, kn_ref), (vbuf, v_ref, vp_ref, vn_ref)):
        buf[0:hw] = prev[...]
        buf[hw:hw + tl] = main[...]
        buf[hw + tl:2 * hw + tl] = nxt[...]

    a_io = lax.broadcasted_iota(jnp.int32, (B_QBLK, B_KBLK), 0)
    c_io = lax.broadcasted_iota(jnp.int32, (B_QBLK, B_KBLK), 1)
    band = (c_io >= a_io) & (c_io <= a_io + 2 * hw)
    lane = lax.broadcasted_iota(jnp.int32, (B_QBLK, LANES), 1)
    lane_q = lane

    def q_block(jb, carry):
        r0 = pl.multiple_of(jb * B_QBLK, B_QBLK)
        ki = t0 + r0 - hw + c_io
        valid = band & (ki >= 0) & (ki < length)
        valid2 = jnp.concatenate([valid, valid], axis=0)
        lse_tile = jnp.zeros((B_QBLK, LANES), F32)
        for hp in range(B_HEADS // 2):
            cols = slice(hp * LANES, (hp + 1) * LANES)
            q = q_ref[pl.ds(r0, B_QBLK), cols]
            zero = jnp.zeros_like(q)
            q2 = jnp.concatenate([jnp.where(lane_q < B_HEAD_DIM, q, zero),
                                  jnp.where(lane_q >= B_HEAD_DIM, q, zero)], axis=0)
            s = lax.dot_general(q2, kbuf[pl.ds(r0, B_KBLK), cols], (((1,), (1,)), ((), ())),
                                preferred_element_type=F32)
            s = jnp.where(valid2, s, -jnp.inf)
            m = jnp.max(s, axis=1, keepdims=True)
            p = jnp.exp(s - m)
            l = jnp.sum(p, axis=1, keepdims=True)
            pv = jnp.dot(p.astype(BF16), vbuf[pl.ds(r0, B_KBLK), cols], preferred_element_type=F32) / l
            o_ref[pl.ds(r0, B_QBLK), cols] = jnp.where(lane < B_HEAD_DIM, pv[0:B_QBLK],
                                                       pv[B_QBLK:2 * B_QBLK]).astype(o_ref.dtype)
            lse = m + jnp.log(l)
            lse_tile = jnp.where(lane == 2 * hp, lse[0:B_QBLK], lse_tile)
            lse_tile = jnp.where(lane == 2 * hp + 1, lse[B_QBLK:2 * B_QBLK], lse_tile)
        lse_ref[pl.ds(r0, B_QBLK), :] = lse_tile
        return carry

    lax.fori_loop(0, tl // B_QBLK, q_block, 0)


def _dil_attn_group(qkv, *, tl=512):
    b, dilation, length, _ = qkv.shape
    tl = min(tl, length)
    hw = B_HALF_W
    width = B_HEADS * B_HEAD_DIM
    per_tile = tl // hw
    n_hw = length // hw
    main = lambda part: pl.BlockSpec((None, None, tl, width), lambda bi, r, i: (bi, r, i, part))
    before = lambda part: pl.BlockSpec((None, None, hw, width),
                                       lambda bi, r, i: (bi, r, jnp.maximum(i * per_tile - 1, 0), part))
    after = lambda part: pl.BlockSpec((None, None, hw, width),
                                      lambda bi, r, i: (bi, r, jnp.minimum((i + 1) * per_tile, n_hw - 1), part))
    return pl.pallas_call(
        functools.partial(_dil_attn_kernel, tl=tl, length=length),
        grid=(b, dilation, length // tl),
        in_specs=[main(0), main(1), before(1), after(1), main(2), before(2), after(2)],
        out_specs=[pl.BlockSpec((None, None, tl, width), lambda bi, r, i: (bi, r, i, 0)),
                   pl.BlockSpec((None, None, tl, LANES), lambda bi, r, i: (bi, r, i, 0))],
        out_shape=[jax.ShapeDtypeStruct((b, dilation, length, width), BF16),
                   jax.ShapeDtypeStruct((b, dilation, length, LANES), F32)],
        scratch_shapes=[pltpu.VMEM((tl + 2 * hw, width), BF16), pltpu.VMEM((tl + 2 * hw, width), BF16)],
        compiler_params=_cparams(3),
        name=f"dil_attn_d{dilation}",
    )(*([qkv] * 7))


def _split3(x):
    hi = x.astype(BF16)
    r1 = x - hi.astype(F32)
    mid = r1.astype(BF16)
    lo = (r1 - mid.astype(F32)).astype(BF16)
    return hi, mid, lo


def _dil_merge_kernel(*refs, n_groups):
    o_refs = refs[:n_groups]
    l_refs = refs[n_groups:2 * n_groups]
    perm_refs = refs[2 * n_groups:3 * n_groups - 1]
    expand_ref, out_ref = refs[3 * n_groups - 1:]
    t = out_ref.shape[0]
    outs, lses = [], []
    for gi in range(n_groups):
        o = o_refs[gi][...].reshape(t, o_refs[gi].shape[-1])
        lse = l_refs[gi][...].reshape(t, LANES)
        if gi == 0:
            outs.append(o.astype(F32))
            lses.append(lse)
        else:
            perm = perm_refs[gi - 1][...]
            outs.append(jnp.dot(perm, o, preferred_element_type=F32))
            l3 = jnp.dot(perm, jnp.concatenate(_split3(lse), axis=1), preferred_element_type=F32)
            lses.append(l3[:, 0:LANES] + l3[:, LANES:2 * LANES] + l3[:, 2 * LANES:3 * LANES])
    m = functools.reduce(jnp.maximum, lses)
    es = [jnp.exp(l - m) for l in lses]
    den = functools.reduce(lambda a, b: a + b, es)
    acc = jnp.zeros((t, out_ref.shape[1]), F32)
    for gi in range(n_groups):
        w = es[gi] / den
        hi = w.astype(BF16)
        lo = (w - hi.astype(F32)).astype(BF16)
        wide = jnp.dot(jnp.concatenate([hi, lo], axis=1), expand_ref[...], preferred_element_type=F32)
        acc = acc + wide * outs[gi]
    out_ref[...] = acc.astype(out_ref.dtype)


def _dil_merge(outs, lses, *, t=512):
    b, _, s, width = outs[0].shape
    n_groups = len(outs)
    in_specs, args = [], []
    for arrs, lanes in ((outs, width), (lses, LANES)):
        for a in arrs:
            d = a.shape[1]
            in_specs.append(pl.BlockSpec((None, d, t // d, lanes), lambda bi, i: (bi, 0, i, 0)))
            args.append(a)
    for a in outs[1:]:
        in_specs.append(_const_spec((t, t), lambda bi, i: (0, 0)))
        args.append(jnp.asarray(_residue_perm(t, a.shape[1]).T, BF16))
    expand = np.zeros((LANES, width), np.float32)
    expand[np.arange(width) // B_HEAD_DIM, np.arange(width)] = 1.0
    in_specs.append(_const_spec((2 * LANES, width), lambda bi, i: (0, 0)))
    args.append(jnp.asarray(np.concatenate([expand, expand], axis=0), BF16))
    return pl.pallas_call(
        functools.partial(_dil_merge_kernel, n_groups=n_groups),
        grid=(b, s // t),
        in_specs=in_specs,
        out_specs=pl.BlockSpec((None, t, width), lambda bi, i: (bi, i, 0)),
        out_shape=jax.ShapeDtypeStruct((b, s, width), BF16),
        compiler_params=_cparams(2),
        name="dil_merge",
    )(*args)


def _scan_constants(reverse):
    c = C_CHUNK
    t = np.arange(c)[:, None]
    u = np.arange(c)[None, :]
    if not reverse:
        mats = [u <= t, u > t]
    else:
        mats = [u >= t, u < t]
    masks = []
    for hs in C_LEVELS:
        base = (t // (2 * hs)) * (2 * hs)
        mid = base + hs - 1
        same = (u // (2 * hs)) == (t // (2 * hs))
        if not reverse:
            m = np.where(t > mid, (u > mid) & (u <= t), (u > t) & (u <= mid))
            masks.append(same & (t > mid) & (u <= mid))
        else:
            m = np.where(t <= mid, (u >= t) & (u <= mid), (u > mid) & (u < t))
            masks.append(same & (t <= mid) & (u > mid))
        mats.append(m & same)
    stack = np.concatenate([np.asarray(m, np.float32) for m in mats], axis=0)
    return jnp.asarray(stack, BF16), jnp.asarray(np.stack(masks).astype(np.float32))


def _hgrn_scan_kernel(*refs, tt, hp, reverse, final):
    q_ref, v_ref, lf_ref, mstack_ref, masks_ref = refs[:5]
    refs = refs[5:]
    if final:
        ofw_ref, gate_ref, gn_ref = refs[:3]
        refs = refs[3:]
    o_ref, st_scr, g_scr, k_scr = refs
    c = C_CHUNK
    n_chunks = tt // c
    nt_contract = (((1,), (1,)), ((), ()))

    @pl.when(pl.program_id(2) == 0)
    def _():
        st_scr[...] = jnp.zeros(st_scr.shape, F32)

    row8 = lax.broadcasted_iota(jnp.int32, (C_DIAG, C_DIM), 0)
    lane_c = lax.broadcasted_iota(jnp.int32, (C_DIAG, c), 1)

    def head_chunk(r0, hi):
        cols = slice(hi * C_DIM, (hi + 1) * C_DIM)
        lf = lf_ref[pl.ds(r0, c), cols] * F32(LOG2E)
        top = lf.astype(BF16)
        low = (lf - top.astype(F32)).astype(BF16)
        d2 = jnp.dot(mstack_ref[...], jnp.concatenate([top, low], axis=1), preferred_element_type=F32)
        d = d2[:, 0:C_DIM] + d2[:, C_DIM:2 * C_DIM]
        gq = d[0:c]
        gk = d[c:2 * c]

        q = q_ref[pl.ds(r0, c), cols].astype(F32)
        v = v_ref[pl.ds(r0, c), cols]
        kk = 1.0 - jnp.exp2(lf)
        st = st_scr[hi]
        g_scr[hi] = gq
        k_scr[hi] = kk

        o = lax.dot_general((q * jnp.exp2(gq)).astype(BF16), st.astype(BF16), nt_contract,
                            preferred_element_type=F32)

        a = jnp.zeros((c, c), F32)
        for li in range(len(C_LEVELS)):
            e = jnp.exp2(-jnp.abs(d[(2 + li) * c:(3 + li) * c]))
            pq = (q * e).astype(BF16)
            pk = (kk * e).astype(BF16)
            a = a + lax.dot_general(pq, pk, nt_contract, preferred_element_type=F32) * masks_ref[li]

        blocks = []
        for bi in range(c // C_DIAG):
            rows = slice(bi * C_DIAG, (bi + 1) * C_DIAG)
            gb, qb = gq[rows], q[rows]
            ablk = jnp.zeros((C_DIAG, c), F32)
            for si in range(C_DIAG):
                row = bi * C_DIAG + si
                g_row = jnp.broadcast_to(g_scr[hi, row:row + 1, :], (C_DIAG, C_DIM))
                k_row = jnp.broadcast_to(k_scr[hi, row:row + 1, :], (C_DIAG, C_DIM))
                keep = (row8 <= si) if reverse else (row8 >= si)
                dec = jnp.exp2(jnp.where(keep, gb - g_row, -jnp.inf))
                sc = jnp.sum(qb * dec * k_row, axis=1, keepdims=True)
                ablk = jnp.where(lane_c == row, sc, ablk)
            blocks.append(ablk)
        a = a + jnp.concatenate(blocks, axis=0)

        o = o + jnp.dot(a.astype(BF16), v, preferred_element_type=F32)

        kd = (kk * jnp.exp2(gk)).astype(BF16)
        upd = lax.dot_general(v, kd, (((0,), (0,)), ((), ())), preferred_element_type=F32)
        g_tot = gq[0:1] if reverse else gq[c - 1:c]
        st_scr[hi] = st * jnp.exp2(g_tot) + upd

        if final:
            tot = ofw_ref[pl.ds(r0, c), cols] + o
            y = _rms(tot, gn_ref[...], NORM_EPS) * gate_ref[pl.ds(r0, c), cols].astype(F32)
            o_ref[pl.ds(r0, c), cols] = y.astype(o_ref.dtype)
        else:
            o_ref[pl.ds(r0, c), cols] = o

    def chunk(ci, carry):
        cidx = (n_chunks - 1 - ci) if reverse else ci
        r0 = pl.multiple_of(cidx * c, c)
        for hi in range(hp):
            head_chunk(r0, hi)
        return carry

    lax.fori_loop(0, n_chunks, chunk, 0)


def _hgrn_scan(act, logf, reverse, o_fw=None, gnorm_g=None, *, tt=512, hp=8):
    b, s, _ = act.shape
    tt = min(tt, s)
    nt = s // tt
    final = o_fw is not None
    mstack, masks = _scan_constants(reverse)
    groups = C_HEADS // hp
    tmap = (lambda t: nt - 1 - t) if reverse else (lambda t: t)
    blk = lambda part: pl.BlockSpec((None, tt, hp * C_DIM), lambda bi, hg, t: (bi, tmap(t), part * groups + hg))
    in_specs = [blk(0), blk(1), blk(1 if reverse else 0),
                _const_spec(mstack.shape, lambda bi, hg, t: (0, 0)),
                _const_spec(masks.shape, lambda bi, hg, t: (0, 0, 0))]
    args = [act, act, logf, mstack, masks]
    if final:
        in_specs += [blk(0), blk(2), _const_spec((1, C_DIM), lambda bi, hg, t: (0, 0))]
        args += [o_fw, act, gnorm_g.reshape(1, C_DIM)]
    return pl.pallas_call(
        functools.partial(_hgrn_scan_kernel, tt=tt, hp=hp, reverse=reverse, final=final),
        grid=(b, groups, nt),
        in_specs=in_specs,
        out_specs=blk(0),
        out_shape=jax.ShapeDtypeStruct((b, s, C_HEADS * C_DIM), BF16 if final else F32),
        scratch_shapes=[pltpu.VMEM((hp, C_DIM, C_DIM), F32),
                        pltpu.VMEM((hp, C_CHUNK, C_DIM), F32),
                        pltpu.VMEM((hp, C_CHUNK, C_DIM), F32)],
        compiler_params=_cparams(3),
        name="hgrn_scan_bw" if reverse else "hgrn_scan_fw",
    )(*args)


def _diff_lambda_init(layer):
    return 0.8 - 0.6 * math.exp(-0.3 * layer)


def _trunk(x3, p):
    b, s, d = x3.shape
    x = x3.reshape(b * s, d)
    tables = _rope_tables(s)
    for layer in range(DEPTH):
        kind = layer % N_MIXERS
        j = layer // N_MIXERS
        g = p["norm_g"][layer]
        x3 = x.reshape(b, s, d)
        if kind == 0:
            qkv = _qkv_rope(x3, g[0], p["a_w_qkv"][j], tables, A_HEAD_DIM ** -0.5 * LOG2E)
            o = _diff_attn(qkv.reshape(b, s, -1), p["a_lambda"][j], p["a_subln_g"][j], _diff_lambda_init(layer))
            w_o = p["a_w_o"][j]
        elif kind == 1:
            outs, lses = [], []
            for group, (_, dilation) in enumerate(B_PATTERNS):
                qkv = _qkv_rope(x3, g[0], p["b_w_qkv"][j], tables, B_HEAD_DIM ** -0.5, group, dilation)
                o_g, lse_g = _dil_attn_group(qkv)
                outs.append(o_g)
                lses.append(lse_g)
            o = _dil_merge(outs, lses)
            w_o = p["b_w_o"][j]
        else:
            act, logf = _hgrn_in(x, g[0], p["c_w_in"][j], p["c_lb_logits"], layer)
            act, logf = act.reshape(b, s, -1), logf.reshape(b, s, -1)
            o_fw = _hgrn_scan(act, logf, False)
            o = _hgrn_scan(act, logf, True, o_fw, p["c_gnorm_g"][j])
            w_o = p["c_w_o"][j]
        x = _mm_norm_res(o.reshape(b * s, -1), w_o, g[1], x)
        u = _ffn_in(x, g[2], p["f_w_in"][layer], p["f_conv_w"][layer], p["f_conv_b"][layer], s)
        x = _mm_norm_res(u, p["f_w_out"][layer], g[3], x)
    return x.reshape(b, s, d)


def kernel(x_prompt, x_sample, norm_g, a_w_qkv, a_lambda, a_subln_g, a_w_o, b_w_qkv, b_w_o, c_w_in,
           c_lb_logits, c_gnorm_g, c_w_o, f_w_in, f_conv_w, f_conv_b, f_w_out):
    bf = lambda w: w.astype(BF16)
    p = dict(norm_g=norm_g, a_w_qkv=bf(a_w_qkv), a_lambda=a_lambda, a_subln_g=a_subln_g, a_w_o=bf(a_w_o),
             b_w_qkv=bf(b_w_qkv), b_w_o=bf(b_w_o), c_w_in=bf(c_w_in), c_lb_logits=c_lb_logits,
             c_gnorm_g=c_gnorm_g, c_w_o=bf(c_w_o), f_w_in=bf(f_w_in), f_conv_w=f_conv_w,
             f_conv_b=f_conv_b, f_w_out=bf(f_w_out))
    return _trunk(x_prompt, p), _trunk(x_sample, p)
```

```python
import functools
import math

import numpy as np
import jax
import jax.numpy as jnp
from jax import lax
from jax.experimental import pallas as pl
from jax.experimental.pallas import tpu as pltpu

F32 = jnp.float32
BF16 = jnp.bfloat16

D_MODEL = 1024
DEPTH = 4
N_MIXERS = 3
ROPE_THETA = 500000.0
NORM_EPS = 1e-6

A_HEADS = 8
A_HEAD_DIM = 64
A_V_DIM = 128
A_SUBLN_EPS = 1e-5

B_HEADS = 16
B_HEAD_DIM = 64
B_PATTERNS = ((128, 1), (512, 4), (2048, 16))
B_HALF_W = 64
B_QBLK = 2 * B_HALF_W
B_KBLK = 4 * B_HALF_W

C_HEADS = 8
C_DIM = 128
C_CHUNK = 128
C_LEVELS = (64, 32, 16, 8)
C_DIAG = 8

D_FF = 2816
QKV_GROUP = 3 * D_MODEL
ROPE_ROT = 16
LANES = 128
LOG2E = math.log2(math.e)
BF16_ROWS = 16

V7X_VMEM_BYTES = 64 * 1024 * 1024
V7X_VMEM_LIMIT = V7X_VMEM_BYTES - V7X_VMEM_BYTES // 8


def _cparams(n_axes):
    return pltpu.CompilerParams(dimension_semantics=("arbitrary",) * n_axes,
                                vmem_limit_bytes=V7X_VMEM_LIMIT)


def _const_spec(shape, index_map):
    return pl.BlockSpec(shape, index_map, pipeline_mode=pl.Buffered(1))


def _rms(x, g, eps):
    ms = jnp.mean(x * x, axis=-1, keepdims=True)
    return x * lax.rsqrt(ms + eps) * g


def _softplus_neg_abs(d):
    return jnp.log(1.0 + jnp.exp(-jnp.abs(d)))


def _logaddexp(a, b):
    return jnp.maximum(a, b) + _softplus_neg_abs(a - b)


def _rope_tables(seq):
    half = ROPE_ROT // 2
    inv_freq = ROPE_THETA ** (-jnp.arange(half, dtype=F32) / half)
    ang = jnp.arange(seq, dtype=F32)[:, None] * inv_freq[None, :]
    cos, sin = jnp.cos(ang), jnp.sin(ang)
    rest = B_HEAD_DIM - ROPE_ROT
    c = jnp.concatenate([cos, cos, jnp.ones((seq, rest), F32)], axis=1)
    s1 = jnp.concatenate([-sin, jnp.zeros((seq, rest + half), F32)], axis=1)
    s2 = jnp.concatenate([jnp.zeros((seq, half), F32), sin, jnp.zeros((seq, rest), F32)], axis=1)
    rep = LANES // B_HEAD_DIM
    return tuple(jnp.tile(t, (1, rep)) for t in (c, s1, s2))


def _residue_perm(tm, dilation):
    rows = tm // dilation
    rho = np.arange(tm)
    tok = (rho % rows) * dilation + rho // rows
    perm = np.zeros((tm, tm), np.float32)
    perm[rho, tok] = 1.0
    return perm


def _qkv_rope_kernel(*refs, tn, q_scale, dilation):
    if dilation > 1:
        x_ref, g_ref, w_ref, c_ref, s1_ref, s2_ref, perm_ref, o_ref = refs
    else:
        x_ref, g_ref, w_ref, c_ref, s1_ref, s2_ref, o_ref = refs
    tm = x_ref.shape[0]
    rows = tm // dilation

    h = _rms(x_ref[...], g_ref[...], NORM_EPS).astype(BF16)
    if dilation > 1:
        h = jnp.dot(perm_ref[...], h, preferred_element_type=F32).astype(BF16)

    def put(col, val):
        for rr in range(dilation):
            o_ref[rr, :, col:col + LANES] = val[rr * rows:(rr + 1) * rows].astype(o_ref.dtype)

    c, s1, s2 = c_ref[...], s1_ref[...], s2_ref[...]
    cq, s1q, s2q = c * F32(q_scale), s1 * F32(q_scale), s2 * F32(q_scale)
    for cn in range(QKV_GROUP // tn):
        acc = jnp.dot(h, w_ref[:, cn * tn:(cn + 1) * tn], preferred_element_type=F32)
        for cb in range(tn // LANES):
            col = cn * tn + cb * LANES
            a = acc[:, cb * LANES:(cb + 1) * LANES]
            if col < 2 * D_MODEL:
                tc, t1, t2 = (cq, s1q, s2q) if col < D_MODEL else (c, s1, s2)
                a = a * tc + pltpu.roll(a, LANES - ROPE_ROT // 2, 1) * t1 + pltpu.roll(a, ROPE_ROT // 2, 1) * t2
            put(col, a)


def _qkv_rope(x, g, w, tables, q_scale, group=0, dilation=1, *, tm=512, tn=512):
    b, s, d = x.shape
    assert s % tm == 0 and tm % (dilation * BF16_ROWS) == 0 and QKV_GROUP % tn == 0
    rows = tm // dilation
    if dilation > 1:
        tables = tuple(t.reshape(s // tm, rows, dilation, LANES).transpose(0, 2, 1, 3).reshape(s, LANES)
                       for t in tables)
    tab_spec = pl.BlockSpec((tm, LANES), lambda bi, i: (i, 0))
    in_specs = [pl.BlockSpec((None, tm, d), lambda bi, i: (bi, i, 0)),
                _const_spec((1, d), lambda bi, i: (0, 0)),
                _const_spec((d, QKV_GROUP), lambda bi, i: (0, group)),
                tab_spec, tab_spec, tab_spec]
    args = [x, g.reshape(1, d), w, *tables]
    if dilation > 1:
        in_specs.append(_const_spec((tm, tm), lambda bi, i: (0, 0)))
        args.append(jnp.asarray(_residue_perm(tm, dilation), BF16))
    return pl.pallas_call(
        functools.partial(_qkv_rope_kernel, tn=tn, q_scale=q_scale, dilation=dilation),
        grid=(b, s // tm),
        in_specs=in_specs,
        out_specs=pl.BlockSpec((None, dilation, rows, QKV_GROUP), lambda bi, i: (bi, 0, i, 0)),
        out_shape=jax.ShapeDtypeStruct((b, dilation, s // dilation, QKV_GROUP), BF16),
        compiler_params=_cparams(2),
        name=f"qkv_rope_d{dilation}",
    )(*args)


def _hgrn_in_kernel(x_ref, g_ref, w_ref, lb_ref, act_ref, logf_ref, *, tn, layer):
    h = _rms(x_ref[...], g_ref[...], NORM_EPS).astype(BF16)
    logits = lb_ref[...]
    e = jnp.exp(logits - jnp.max(logits, axis=0, keepdims=True))
    p = e / jnp.sum(e, axis=0, keepdims=True)
    cum = p[0:1]
    for r in range(1, layer + 1):
        cum = cum + p[r:r + 1]
    lb = cum - p[0:1]
    log_lb = jnp.log(lb)
    log_1m_lb = jnp.log1p(-lb)

    for cn in range(5 * D_MODEL // tn):
        col = cn * tn
        part, off = col // D_MODEL, col % D_MODEL
        acc = jnp.dot(h, w_ref[:, col:col + tn], preferred_element_type=F32)
        if part == 0:
            act_ref[:, off:off + tn] = (acc * jax.nn.sigmoid(acc) * F32(C_DIM ** -0.5)).astype(act_ref.dtype)
        elif part in (1, 2):
            log_sig = jnp.minimum(acc, 0.0) - _softplus_neg_abs(acc)
            lo = (part - 1) * D_MODEL + off
            logf_ref[:, lo:lo + tn] = _logaddexp(log_lb[:, off:off + tn], log_1m_lb[:, off:off + tn] + log_sig)
        elif part == 3:
            act_ref[:, D_MODEL + off:D_MODEL + off + tn] = acc.astype(act_ref.dtype)
        else:
            act_ref[:, 2 * D_MODEL + off:2 * D_MODEL + off + tn] = (acc * jax.nn.sigmoid(acc)).astype(act_ref.dtype)


def _hgrn_in(x, g, w, lb_logits, layer, *, tm=512, tn=512):
    m, d = x.shape
    assert m % tm == 0 and D_MODEL % tn == 0
    return pl.pallas_call(
        functools.partial(_hgrn_in_kernel, tn=tn, layer=layer),
        grid=(m // tm,),
        in_specs=[pl.BlockSpec((tm, d), lambda i: (i, 0)),
                  _const_spec((1, d), lambda i: (0, 0)),
                  _const_spec((d, 5 * D_MODEL), lambda i: (0, 0)),
                  _const_spec((DEPTH, D_MODEL), lambda i: (0, 0))],
        out_specs=[pl.BlockSpec((tm, 3 * D_MODEL), lambda i: (i, 0)),
                   pl.BlockSpec((tm, 2 * D_MODEL), lambda i: (i, 0))],
        out_shape=[jax.ShapeDtypeStruct((m, 3 * D_MODEL), BF16),
                   jax.ShapeDtypeStruct((m, 2 * D_MODEL), F32)],
        compiler_params=_cparams(1),
        name="hgrn_in",
    )(x, g.reshape(1, d), w, lb_logits)


def _mm_norm_res_kernel(a_ref, w_ref, g_ref, x_ref, o_ref, *, sub):
    for rb in range(a_ref.shape[0] // sub):
        rows = slice(rb * sub, (rb + 1) * sub)
        h = jnp.dot(a_ref[rows, :], w_ref[...], preferred_element_type=F32)
        o_ref[rows, :] = x_ref[rows, :] + _rms(h, g_ref[...], NORM_EPS)


def _mm_norm_res(a, w, g, x, *, tm=1024, sub=256):
    m, k = a.shape
    d = w.shape[1]
    assert m % tm == 0 and tm % sub == 0
    return pl.pallas_call(
        functools.partial(_mm_norm_res_kernel, sub=sub),
        grid=(m // tm,),
        in_specs=[pl.BlockSpec((tm, k), lambda i: (i, 0)),
                  _const_spec((k, d), lambda i: (0, 0)),
                  _const_spec((1, d), lambda i: (0, 0)),
                  pl.BlockSpec((tm, d), lambda i: (i, 0))],
        out_specs=pl.BlockSpec((tm, d), lambda i: (i, 0)),
        out_shape=jax.ShapeDtypeStruct((m, d), F32),
        compiler_params=_cparams(1),
        name="mm_norm_res",
    )(a, w, g.reshape(1, d), x)


def _ffn_in_kernel(x_ref, xp_ref, xn_ref, g_ref, w_ref, cw_ref, cb_ref, o_ref, h_scr, u_scr, *, tm, tn, seq):
    i = pl.program_id(0)
    n_tiles = pl.num_programs(0) - 1
    halo = BF16_ROWS
    rows = tm + 2 * halo
    n_chunks = D_FF // tn
    slot_new = i % 2
    slot_old = 1 - slot_new

    @pl.when(i == 0)
    def _():
        u_scr[...] = jnp.zeros(u_scr.shape, F32)

    t = jnp.minimum(i, n_tiles - 1)
    g = g_ref[...]
    first = (t * tm) % seq == 0
    last = ((t + 1) * tm) % seq == 0
    hp = _rms(xp_ref[...], g, NORM_EPS)
    hn = _rms(xn_ref[...], g, NORM_EPS)
    h_scr[0:halo] = jnp.where(first, 0.0, hp).astype(BF16)
    h_scr[halo:halo + tm] = _rms(x_ref[...], g, NORM_EPS).astype(BF16)
    h_scr[halo + tm:rows] = jnp.where(last, 0.0, hn).astype(BF16)

    def matmul(cn):
        u_scr[slot_new, cn] = jnp.dot(h_scr[...], w_ref[:, cn * tn:(cn + 1) * tn], preferred_element_type=F32)

    def conv(cn, scale=None):
        cw = cw_ref[:, cn * tn:(cn + 1) * tn]
        cb = cb_ref[:, cn * tn:(cn + 1) * tn]
        if scale is not None:
            cw, cb = cw * scale, cb * scale
        return (cb + u_scr[slot_old, cn, halo - 1:halo - 1 + tm, :] * cw[0:1]
                + u_scr[slot_old, cn, halo:halo + tm, :] * cw[1:2]
                + u_scr[slot_old, cn, halo + 1:halo + 1 + tm, :] * cw[2:3])

    k0 = math.sqrt(2.0 / math.pi)
    for cn in range(n_chunks):
        matmul(2 * cn)
        matmul(2 * cn + 1)
        a = conv(cn)
        half_b = conv(n_chunks + cn, F32(0.5))
        th = jnp.tanh(a * (F32(k0) + F32(k0 * 0.044715) * (a * a)))
        o_ref[:, cn * tn:(cn + 1) * tn] = ((a * half_b) * (1.0 + th)).astype(o_ref.dtype)


def _ffn_in(x, g, w, conv_w, conv_b, seq, *, tm=512, tn=256):
    m, d = x.shape
    assert seq % tm == 0 and m % seq == 0 and D_FF % tn == 0
    halo = BF16_ROWS
    hb = tm // halo
    last_blk = m // halo - 1
    n_tiles = m // tm
    cur = lambda i: jnp.minimum(i, n_tiles - 1)
    return pl.pallas_call(
        functools.partial(_ffn_in_kernel, tm=tm, tn=tn, seq=seq),
        grid=(n_tiles + 1,),
        in_specs=[pl.BlockSpec((tm, d), lambda i: (cur(i), 0)),
                  pl.BlockSpec((halo, d), lambda i: (jnp.maximum(cur(i) * hb - 1, 0), 0)),
                  pl.BlockSpec((halo, d), lambda i: (jnp.minimum((cur(i) + 1) * hb, last_blk), 0)),
                  _const_spec((1, d), lambda i: (0, 0)),
                  _const_spec((d, 2 * D_FF), lambda i: (0, 0)),
                  _const_spec((3, 2 * D_FF), lambda i: (0, 0)),
                  _const_spec((1, 2 * D_FF), lambda i: (0, 0))],
        out_specs=pl.BlockSpec((tm, D_FF), lambda i: (jnp.maximum(i - 1, 0), 0)),
        out_shape=jax.ShapeDtypeStruct((m, D_FF), BF16),
        scratch_shapes=[pltpu.VMEM((tm + 2 * halo, d), BF16),
                        pltpu.VMEM((2, 2 * D_FF // tn, tm + 2 * halo, tn), F32)],
        compiler_params=_cparams(1),
        name="ffn_in",
    )(x, x, x, g.reshape(1, d), w, conv_w, conv_b.reshape(1, -1))


def _diff_attn_kernel(q_ref, k_ref, v_ref, lam_ref, g_ref, o_ref, q2_scr, vx_scr, m_scr, acc_scr, s_scr,
                      *, tq, tc, unroll, lam_init):
    seq = k_ref.shape[0]
    q = q_ref[...]
    lane = lax.broadcasted_iota(jnp.int32, q.shape, 1)
    zero = jnp.zeros_like(q)
    q2_scr[0:tq] = jnp.where(lane < A_HEAD_DIM, q, zero)
    q2_scr[tq:2 * tq] = jnp.where(lane >= A_HEAD_DIM, q, zero)

    @pl.when(pl.program_id(2) == 0)
    def _():
        vx_scr[:, 0:A_V_DIM] = v_ref[...]
        vx_scr[:, A_V_DIM:2 * A_V_DIM] = jnp.ones((seq, A_V_DIM), BF16)

    def scores(c, slot):
        r0 = c * tc if isinstance(c, int) else pl.multiple_of(c * tc, tc)
        s_scr[slot] = lax.dot_general(q2_scr[...], k_ref[pl.ds(r0, tc), :], (((1,), (1,)), ((), ())),
                                      preferred_element_type=F32)

    def softmax_pv(c, slot, first=False):
        r0 = c * tc if isinstance(c, int) else pl.multiple_of(c * tc, tc)
        slabs = [s_scr[slot, :, j * LANES:(j + 1) * LANES] for j in range(tc // LANES)]
        mx = slabs[0]
        for sl in slabs[1:]:
            mx = jnp.maximum(mx, sl)
        m_cur = jnp.max(mx, axis=1, keepdims=True)
        if first:
            m_new = jnp.broadcast_to(m_cur, m_scr.shape)
        else:
            m_prev = m_scr[...]
            m_new = jnp.maximum(m_prev, m_cur)
            alpha = jnp.exp2(m_prev - m_new)
        p = jnp.concatenate([jnp.exp2(sl - m_new).astype(BF16) for sl in slabs], axis=1)
        pv = jnp.dot(p, vx_scr[pl.ds(r0, tc), :], preferred_element_type=F32)
        if first:
            acc_scr[...] = pv
        else:
            acc_scr[:, 0:A_V_DIM] = alpha * acc_scr[:, 0:A_V_DIM] + pv[:, 0:A_V_DIM]
            acc_scr[:, A_V_DIM:2 * A_V_DIM] = alpha * acc_scr[:, A_V_DIM:2 * A_V_DIM] + pv[:, A_V_DIM:2 * A_V_DIM]
        m_scr[...] = m_new

    n_chunks = seq // tc

    def pairs(p0, first):
        for u in range(unroll):
            c = 2 * (p0 + u)
            nxt = min(c + 2, n_chunks - 1) if isinstance(c, int) else jnp.minimum(c + 2, n_chunks - 1)
            scores(c + 1, 1)
            softmax_pv(c, 0, first=first and u == 0)
            scores(nxt, 0)
            softmax_pv(c + 1, 1)

    def body(it, carry):
        pairs(unroll * it, False)
        return carry

    scores(0, 0)
    pairs(0, True)
    lax.fori_loop(1, n_chunks // (2 * unroll), body, 0)

    o = acc_scr[:, 0:A_V_DIM] / acc_scr[:, A_V_DIM:2 * A_V_DIM]
    lv = lam_ref[...]
    lam = (jnp.exp(jnp.sum(lv[0:1] * lv[1:2], axis=1, keepdims=True))
           - jnp.exp(jnp.sum(lv[2:3] * lv[3:4], axis=1, keepdims=True)) + F32(lam_init))
    od = o[0:tq] - lam * o[tq:2 * tq]
    y = _rms(od, g_ref[...], A_SUBLN_EPS) * F32(1.0 - lam_init)
    o_ref[...] = y.astype(o_ref.dtype)


def _diff_attn(qkv, lam_vecs, subln_g, lam_init, *, tq=1024, tc=512, unroll=2):
    b, s, _ = qkv.shape
    tq = min(tq, s)
    assert s % tq == 0 and s % (2 * unroll * tc) == 0
    blk = lambda rows, off: pl.BlockSpec((None, rows, LANES), off)
    return pl.pallas_call(
        functools.partial(_diff_attn_kernel, tq=tq, tc=tc, unroll=unroll, lam_init=lam_init),
        grid=(b, A_HEADS, s // tq),
        in_specs=[blk(tq, lambda bi, h, i: (bi, i, h)),
                  blk(s, lambda bi, h, i: (bi, 0, A_HEADS + h)),
                  blk(s, lambda bi, h, i: (bi, 0, 2 * A_HEADS + h)),
                  pl.BlockSpec((4, A_HEAD_DIM), lambda bi, h, i: (0, 0)),
                  pl.BlockSpec((1, A_V_DIM), lambda bi, h, i: (0, 0))],
        out_specs=blk(tq, lambda bi, h, i: (bi, i, h)),
        out_shape=jax.ShapeDtypeStruct((b, s, A_HEADS * A_V_DIM), BF16),
        scratch_shapes=[pltpu.VMEM((2 * tq, LANES), BF16),
                        pltpu.VMEM((s, 2 * A_V_DIM), BF16),
                        pltpu.VMEM((2 * tq, LANES), F32),
                        pltpu.VMEM((2 * tq, 2 * A_V_DIM), F32),
                        pltpu.VMEM((2, 2 * tq, tc), F32)],
        compiler_params=_cparams(3),
        name="diff_attn",
    )(qkv, qkv, qkv, lam_vecs, subln_g.reshape(1, A_V_DIM))


def _dil_attn_kernel(q_ref, k_ref, kp_ref, kn_ref, v_ref, vp_ref, vn_ref, o_ref, lse_ref, kbuf, vbuf,
                     *, tl, length):
    hw = B_HALF_W
    t0 = pl.program_id(2) * tl

    for buf, main, prev, nxt in ((kbuf, k_ref, kp_ref, kn_ref), (vbuf, v_ref, vp_ref, vn_ref)):
        buf[0:hw] = prev[...]
        buf[hw:hw + tl] = main[...]
        buf[hw + tl:2 * hw + tl] = nxt[...]

    a_io = lax.broadcasted_iota(jnp.int32, (B_QBLK, B_KBLK), 0)
    c_io = lax.broadcasted_iota(jnp.int32, (B_QBLK, B_KBLK), 1)
    band = (c_io >= a_io) & (c_io <= a_io + 2 * hw)
    lane = lax.broadcasted_iota(jnp.int32, (B_QBLK, LANES), 1)

    def q_block(jb, carry):
        r0 = pl.multiple_of(jb * B_QBLK, B_QBLK)
        ki = t0 + r0 - hw + c_io
        valid = band & (ki >= 0) & (ki < length)
        valid2 = jnp.concatenate([valid, valid], axis=0)
        lse_tile = jnp.zeros((B_QBLK, LANES), F32)
        for hp in range(B_HEADS // 2):
            cols = slice(hp * LANES, (hp + 1) * LANES)
            q = q_ref[pl.ds(r0, B_QBLK), cols]
            zero = jnp.zeros_like(q)
            q2 = jnp.concatenate([jnp.where(lane < B_HEAD_DIM, q, zero),
                                  jnp.where(lane >= B_HEAD_DIM, q, zero)], axis=0)
            s = lax.dot_general(q2, kbuf[pl.ds(r0, B_KBLK), cols], (((1,), (1,)), ((), ())),
                                preferred_element_type=F32)
            s = jnp.where(valid2, s, -jnp.inf)
            m = jnp.max(s, axis=1, keepdims=True)
            p = jnp.exp(s - m)
            l = jnp.sum(p, axis=1, keepdims=True)
            pv = jnp.dot(p.astype(BF16), vbuf[pl.ds(r0, B_KBLK), cols], preferred_element_type=F32) / l
            o_ref[pl.ds(r0, B_QBLK), cols] = jnp.where(lane < B_HEAD_DIM, pv[0:B_QBLK],
                                                       pv[B_QBLK:2 * B_QBLK]).astype(o_ref.dtype)
            lse = m + jnp.log(l)
            lse_tile = jnp.where(lane == 2 * hp, lse[0:B_QBLK], lse_tile)
            lse_tile = jnp.where(lane == 2 * hp + 1, lse[B_QBLK:2 * B_QBLK], lse_tile)
        lse_ref[pl.ds(r0, B_QBLK), :] = lse_tile
        return carry

    lax.fori_loop(0, tl // B_QBLK, q_block, 0)


def _dil_attn_group(qkv, *, tl=512):
    b, dilation, length, _ = qkv.shape
    tl = min(tl, length)
    assert length % tl == 0 and tl % B_QBLK == 0
    hw = B_HALF_W
    width = B_HEADS * B_HEAD_DIM
    per_tile = tl // hw
    n_hw = length // hw
    main = lambda part: pl.BlockSpec((None, None, tl, width), lambda bi, r, i: (bi, r, i, part))
    before = lambda part: pl.BlockSpec((None, None, hw, width),
                                       lambda bi, r, i: (bi, r, jnp.maximum(i * per_tile - 1, 0), part))
    after = lambda part: pl.BlockSpec((None, None, hw, width),
                                      lambda bi, r, i: (bi, r, jnp.minimum((i + 1) * per_tile, n_hw - 1), part))
    return pl.pallas_call(
        functools.partial(_dil_attn_kernel, tl=tl, length=length),
        grid=(b, dilation, length // tl),
        in_specs=[main(0), main(1), before(1), after(1), main(2), before(2), after(2)],
        out_specs=[pl.BlockSpec((None, None, tl, width), lambda bi, r, i: (bi, r, i, 0)),
                   pl.BlockSpec((None, None, tl, LANES), lambda bi, r, i: (bi, r, i, 0))],
        out_shape=[jax.ShapeDtypeStruct((b, dilation, length, width), BF16),
                   jax.ShapeDtypeStruct((b, dilation, length, LANES), F32)],
        scratch_shapes=[pltpu.VMEM((tl + 2 * hw, width), BF16), pltpu.VMEM((tl + 2 * hw, width), BF16)],
        compiler_params=_cparams(3),
        name=f"dil_attn_d{dilation}",
    )(*([qkv] * 7))


def _split3(x):
    hi = x.astype(BF16)
    r1 = x - hi.astype(F32)
    mid = r1.astype(BF16)
    lo = (r1 - mid.astype(F32)).astype(BF16)
    return hi, mid, lo


def _dil_merge_kernel(*refs, n_groups):
    o_refs = refs[:n_groups]
    l_refs = refs[n_groups:2 * n_groups]
    perm_refs = refs[2 * n_groups:3 * n_groups - 1]
    expand_ref, out_ref = refs[3 * n_groups - 1:]
    t = out_ref.shape[0]
    outs, lses = [], []
    for gi in range(n_groups):
        o = o_refs[gi][...].reshape(t, o_refs[gi].shape[-1])
        lse = l_refs[gi][...].reshape(t, LANES)
        if gi == 0:
            outs.append(o.astype(F32))
            lses.append(lse)
        else:
            perm = perm_refs[gi - 1][...]
            outs.append(jnp.dot(perm, o, preferred_element_type=F32))
            l3 = jnp.dot(perm, jnp.concatenate(_split3(lse), axis=1), preferred_element_type=F32)
            lses.append(l3[:, 0:LANES] + l3[:, LANES:2 * LANES] + l3[:, 2 * LANES:3 * LANES])
    m = functools.reduce(jnp.maximum, lses)
    es = [jnp.exp(l - m) for l in lses]
    den = functools.reduce(lambda a, b: a + b, es)
    acc = jnp.zeros((t, out_ref.shape[1]), F32)
    for gi in range(n_groups):
        w = es[gi] / den
        hi = w.astype(BF16)
        lo = (w - hi.astype(F32)).astype(BF16)
        wide = jnp.dot(jnp.concatenate([hi, lo], axis=1), expand_ref[...], preferred_element_type=F32)
        acc = acc + wide * outs[gi]
    out_ref[...] = acc.astype(out_ref.dtype)


def _dil_merge(outs, lses, *, t=512):
    b, _, s, width = outs[0].shape
    n_groups = len(outs)
    in_specs, args = [], []
    for arrs, lanes in ((outs, width), (lses, LANES)):
        for a in arrs:
            d = a.shape[1]
            in_specs.append(pl.BlockSpec((None, d, t // d, lanes), lambda bi, i: (bi, 0, i, 0)))
            args.append(a)
    for a in outs[1:]:
        in_specs.append(_const_spec((t, t), lambda bi, i: (0, 0)))
        args.append(jnp.asarray(_residue_perm(t, a.shape[1]).T, BF16))
    expand = np.zeros((LANES, width), np.float32)
    expand[np.arange(width) // B_HEAD_DIM, np.arange(width)] = 1.0
    in_specs.append(_const_spec((2 * LANES, width), lambda bi, i: (0, 0)))
    args.append(jnp.asarray(np.concatenate([expand, expand], axis=0), BF16))
    return pl.pallas_call(
        functools.partial(_dil_merge_kernel, n_groups=n_groups),
        grid=(b, s // t),
        in_specs=in_specs,
        out_specs=pl.BlockSpec((None, t, width), lambda bi, i: (bi, i, 0)),
        out_shape=jax.ShapeDtypeStruct((b, s, width), BF16),
        compiler_params=_cparams(2),
        name="dil_merge",
    )(*args)


def _scan_constants(reverse):
    c = C_CHUNK
    t = np.arange(c)[:, None]
    u = np.arange(c)[None, :]
    if not reverse:
        mats = [u <= t, u > t]
    else:
        mats = [u >= t, u < t]
    masks = []
    for hs in C_LEVELS:
        base = (t // (2 * hs)) * (2 * hs)
        mid = base + hs - 1
        same = (u // (2 * hs)) == (t // (2 * hs))
        if not reverse:
            m = np.where(t > mid, (u > mid) & (u <= t), (u > t) & (u <= mid))
            masks.append(same & (t > mid) & (u <= mid))
        else:
            m = np.where(t <= mid, (u >= t) & (u <= mid), (u > mid) & (u < t))
            masks.append(same & (t <= mid) & (u > mid))
        mats.append(m & same)
    stack = np.concatenate([np.asarray(m, np.float32) for m in mats], axis=0)
    return jnp.asarray(stack, BF16), jnp.asarray(np.stack(masks).astype(np.float32))


def _hgrn_scan_kernel(*refs, tt, hp, reverse, final):
    q_ref, v_ref, lf_ref, mstack_ref, masks_ref = refs[:5]
    refs = refs[5:]
    if final:
        ofw_ref, gate_ref, gn_ref = refs[:3]
        refs = refs[3:]
    o_ref, st_scr, g_scr, k_scr = refs
    c = C_CHUNK
    n_chunks = tt // c
    nt_contract = (((1,), (1,)), ((), ()))

    @pl.when(pl.program_id(2) == 0)
    def _():
        st_scr[...] = jnp.zeros(st_scr.shape, F32)

    row8 = lax.broadcasted_iota(jnp.int32, (C_DIAG, C_DIM), 0)
    lane_c = lax.broadcasted_iota(jnp.int32, (C_DIAG, c), 1)

    def head_chunk(r0, hi):
        cols = slice(hi * C_DIM, (hi + 1) * C_DIM)
        lf = lf_ref[pl.ds(r0, c), cols] * F32(LOG2E)
        top = lf.astype(BF16)
        low = (lf - top.astype(F32)).astype(BF16)
        d2 = jnp.dot(mstack_ref[...], jnp.concatenate([top, low], axis=1), preferred_element_type=F32)
        d = d2[:, 0:C_DIM] + d2[:, C_DIM:2 * C_DIM]
        gq = d[0:c]
        gk = d[c:2 * c]

        q = q_ref[pl.ds(r0, c), cols].astype(F32)
        v = v_ref[pl.ds(r0, c), cols]
        kk = 1.0 - jnp.exp2(lf)
        st = st_scr[hi]
        g_scr[hi] = gq
        k_scr[hi] = kk

        o = lax.dot_general((q * jnp.exp2(gq)).astype(BF16), st.astype(BF16), nt_contract,
                            preferred_element_type=F32)

        a = jnp.zeros((c, c), F32)
        for li in range(len(C_LEVELS)):
            e = jnp.exp2(-jnp.abs(d[(2 + li) * c:(3 + li) * c]))
            pq = (q * e).astype(BF16)
            pk = (kk * e).astype(BF16)
            a = a + lax.dot_general(pq, pk, nt_contract, preferred_element_type=F32) * masks_ref[li]

        blocks = []
        for bi in range(c // C_DIAG):
            rows = slice(bi * C_DIAG, (bi + 1) * C_DIAG)
            gb, qb = gq[rows], q[rows]
            ablk = jnp.zeros((C_DIAG, c), F32)
            for si in range(C_DIAG):
                row = bi * C_DIAG + si
                g_row = jnp.broadcast_to(g_scr[hi, row:row + 1, :], (C_DIAG, C_DIM))
                k_row = jnp.broadcast_to(k_scr[hi, row:row + 1, :], (C_DIAG, C_DIM))
                keep = (row8 <= si) if reverse else (row8 >= si)
                dec = jnp.exp2(jnp.where(keep, gb - g_row, -jnp.inf))
                sc = jnp.sum(qb * dec * k_row, axis=1, keepdims=True)
                ablk = jnp.where(lane_c == row, sc, ablk)
            blocks.append(ablk)
        a = a + jnp.concatenate(blocks, axis=0)

        o = o + jnp.dot(a.astype(BF16), v, preferred_element_type=F32)

        kd = (kk * jnp.exp2(gk)).astype(BF16)
        upd = lax.dot_general(v, kd, (((0,), (0,)), ((), ())), preferred_element_type=F32)
        g_tot = gq[0:1] if reverse else gq[c - 1:c]
        st_scr[hi] = st * jnp.exp2(g_tot) + upd

        if final:
            tot = ofw_ref[pl.ds(r0, c), cols] + o
            y = _rms(tot, gn_ref[...], NORM_EPS) * gate_ref[pl.ds(r0, c), cols].astype(F32)
            o_ref[pl.ds(r0, c), cols] = y.astype(o_ref.dtype)
        else:
            o_ref[pl.ds(r0, c), cols] = o

    def chunk(ci, carry):
        cidx = (n_chunks - 1 - ci) if reverse else ci
        r0 = pl.multiple_of(cidx * c, c)
        for hi in range(hp):
            head_chunk(r0, hi)
        return carry

    lax.fori_loop(0, n_chunks, chunk, 0)


def _hgrn_scan(act, logf, reverse, o_fw=None, gnorm_g=None, *, tt=512, hp=8):
    b, s, _ = act.shape
    tt = min(tt, s)
    assert s % tt == 0 and tt % C_CHUNK == 0 and C_HEADS % hp == 0
    nt = s // tt
    final = o_fw is not None
    mstack, masks = _scan_constants(reverse)
    groups = C_HEADS // hp
    tmap = (lambda t: nt - 1 - t) if reverse else (lambda t: t)
    blk = lambda part: pl.BlockSpec((None, tt, hp * C_DIM), lambda bi, hg, t: (bi, tmap(t), part * groups + hg))
    in_specs = [blk(0), blk(1), blk(1 if reverse else 0),
                _const_spec(mstack.shape, lambda bi, hg, t: (0, 0)),
                _const_spec(masks.shape, lambda bi, hg, t: (0, 0, 0))]
    args = [act, act, logf, mstack, masks]
    if final:
        in_specs += [blk(0), blk(2), _const_spec((1, C_DIM), lambda bi, hg, t: (0, 0))]
        args += [o_fw, act, gnorm_g.reshape(1, C_DIM)]
    return pl.pallas_call(
        functools.partial(_hgrn_scan_kernel, tt=tt, hp=hp, reverse=reverse, final=final),
        grid=(b, groups, nt),
        in_specs=in_specs,
        out_specs=blk(0),
        out_shape=jax.ShapeDtypeStruct((b, s, C_HEADS * C_DIM), BF16 if final else F32),
        scratch_shapes=[pltpu.VMEM((hp, C_DIM, C_DIM), F32),
                        pltpu.VMEM((hp, C_CHUNK, C_DIM), F32),
                        pltpu.VMEM((hp, C_CHUNK, C_DIM), F32)],
        compiler_params=_cparams(3),
        name="hgrn_scan_bw" if reverse else "hgrn_scan_fw",
    )(*args)


def _diff_lambda_init(layer):
    return 0.8 - 0.6 * math.exp(-0.3 * layer)


def _trunk(x3, p):
    b, s, d = x3.shape
    x = x3.reshape(b * s, d)
    tables = _rope_tables(s)
    for layer in range(DEPTH):
        kind = layer % N_MIXERS
        j = layer // N_MIXERS
        g = p["norm_g"][layer]
        x3 = x.reshape(b, s, d)
        if kind == 0:
            qkv = _qkv_rope(x3, g[0], p["a_w_qkv"][j], tables, A_HEAD_DIM ** -0.5 * LOG2E)
            o = _diff_attn(qkv.reshape(b, s, -1), p["a_lambda"][j], p["a_subln_g"][j], _diff_lambda_init(layer))
            w_o = p["a_w_o"][j]
        elif kind == 1:
            outs, lses = [], []
            for group, (_, dilation) in enumerate(B_PATTERNS):
                qkv = _qkv_rope(x3, g[0], p["b_w_qkv"][j], tables, B_HEAD_DIM ** -0.5, group, dilation)
                o_g, lse_g = _dil_attn_group(qkv)
                outs.append(o_g)
                lses.append(lse_g)
            o = _dil_merge(outs, lses)
            w_o = p["b_w_o"][j]
        else:
            act, logf = _hgrn_in(x, g[0], p["c_w_in"][j], p["c_lb_logits"], layer)
            act, logf = act.reshape(b, s, -1), logf.reshape(b, s, -1)
            o_fw = _hgrn_scan(act, logf, False)
            o = _hgrn_scan(act, logf, True, o_fw, p["c_gnorm_g"][j])
            w_o = p["c_w_o"][j]
        x = _mm_norm_res(o.reshape(b * s, -1), w_o, g[1], x)
        u = _ffn_in(x, g[2], p["f_w_in"][layer], p["f_conv_w"][layer], p["f_conv_b"][layer], s)
        x = _mm_norm_res(u, p["f_w_out"][layer], g[3], x)
    return x.reshape(b, s, d)


def kernel(x_prompt, x_sample, norm_g, a_w_qkv, a_lambda, a_subln_g, a_w_o, b_w_qkv, b_w_o, c_w_in,
           c_lb_logits, c_gnorm_g, c_w_o, f_w_in, f_conv_w, f_conv_b, f_w_out):
    bf = lambda w: w.astype(BF16)
    p = dict(norm_g=norm_g, a_w_qkv=bf(a_w_qkv), a_lambda=a_lambda, a_subln_g=a_subln_g, a_w_o=bf(a_w_o),
             b_w_qkv=bf(b_w_qkv), b_w_o=bf(b_w_o), c_w_in=bf(c_w_in), c_lb_logits=c_lb_logits,
             c_gnorm_g=c_gnorm_g, c_w_o=bf(c_w_o), f_w_in=bf(f_w_in), f_conv_w=f_conv_w,
             f_conv_b=f_conv_b, f_w_out=bf(f_w_out))
    return _trunk(x_prompt, p), _trunk(x_sample, p)
```

```python
import functools
import math

import numpy as np
import jax
import jax.numpy as jnp
from jax import lax
from jax.experimental import pallas as pl
from jax.experimental.pallas import tpu as pltpu

F32 = jnp.float32
BF16 = jnp.bfloat16

D_MODEL = 1024
DEPTH = 4
N_MIXERS = 3
ROPE_THETA = 500000.0
NORM_EPS = 1e-6

A_HEADS = 8
A_HEAD_DIM = 64
A_V_DIM = 128
A_SUBLN_EPS = 1e-5

B_HEADS = 16
B_HEAD_DIM = 64
B_PATTERNS = ((128, 1), (512, 4), (2048, 16))
B_HALF_W = 64
B_QBLK = 2 * B_HALF_W
B_KBLK = 4 * B_HALF_W

C_HEADS = 8
C_DIM = 128
C_CHUNK = 128
C_LEVELS = (64, 32, 16, 8)
C_DIAG = 8

D_FF = 2816
QKV_GROUP = 3 * D_MODEL
ROPE_ROT = 16
LANES = 128
LOG2E = math.log2(math.e)
BF16_ROWS = 16

V7X_VMEM_BYTES = 64 * 1024 * 1024
V7X_VMEM_LIMIT = V7X_VMEM_BYTES - V7X_VMEM_BYTES // 8


def _cparams(n_axes):
    return pltpu.CompilerParams(dimension_semantics=("arbitrary",) * n_axes,
                                vmem_limit_bytes=V7X_VMEM_LIMIT)


def _const_spec(shape, index_map):
    return pl.BlockSpec(shape, index_map, pipeline_mode=pl.Buffered(1))


def _rms(x, g, eps):
    ms = jnp.mean(x * x, axis=-1, keepdims=True)
    return x * lax.rsqrt(ms + eps) * g


def _softplus_neg_abs(d):
    return jnp.log(1.0 + jnp.exp(-jnp.abs(d)))


def _logaddexp(a, b):
    return jnp.maximum(a, b) + _softplus_neg_abs(a - b)


def _rope_tables(seq):
    half = ROPE_ROT // 2
    inv_freq = ROPE_THETA ** (-jnp.arange(half, dtype=F32) / half)
    ang = jnp.arange(seq, dtype=F32)[:, None] * inv_freq[None, :]
    cos, sin = jnp.cos(ang), jnp.sin(ang)
    rest = B_HEAD_DIM - ROPE_ROT
    c = jnp.concatenate([cos, cos, jnp.ones((seq, rest), F32)], axis=1)
    s1 = jnp.concatenate([-sin, jnp.zeros((seq, rest + half), F32)], axis=1)
    s2 = jnp.concatenate([jnp.zeros((seq, half), F32), sin, jnp.zeros((seq, rest), F32)], axis=1)
    rep = LANES // B_HEAD_DIM
    return tuple(jnp.tile(t, (1, rep)) for t in (c, s1, s2))


def _residue_perm(tm, dilation):
    rows = tm // dilation
    rho = np.arange(tm)
    tok = (rho % rows) * dilation + rho // rows
    perm = np.zeros((tm, tm), np.float32)
    perm[rho, tok] = 1.0
    return perm


def _qkv_rope_kernel(*refs, tn, q_scale, dilation):
    if dilation > 1:
        x_ref, g_ref, w_ref, c_ref, s1_ref, s2_ref, perm_ref, o_ref = refs
    else:
        x_ref, g_ref, w_ref, c_ref, s1_ref, s2_ref, o_ref = refs
    tm = x_ref.shape[0]
    rows = tm // dilation

    h = _rms(x_ref[...], g_ref[...], NORM_EPS).astype(BF16)
    if dilation > 1:
        h = jnp.dot(perm_ref[...], h, preferred_element_type=F32).astype(BF16)

    def put(col, val):
        for rr in range(dilation):
            o_ref[rr, :, col:col + LANES] = val[rr * rows:(rr + 1) * rows].astype(o_ref.dtype)

    c, s1, s2 = c_ref[...], s1_ref[...], s2_ref[...]
    cq, s1q, s2q = c * F32(q_scale), s1 * F32(q_scale), s2 * F32(q_scale)
    for cn in range(QKV_GROUP // tn):
        acc = jnp.dot(h, w_ref[:, cn * tn:(cn + 1) * tn], preferred_element_type=F32)
        for cb in range(tn // LANES):
            col = cn * tn + cb * LANES
            a = acc[:, cb * LANES:(cb + 1) * LANES]
            if col < 2 * D_MODEL:
                tc, t1, t2 = (cq, s1q, s2q) if col < D_MODEL else (c, s1, s2)
                a = a * tc + pltpu.roll(a, LANES - ROPE_ROT // 2, 1) * t1 + pltpu.roll(a, ROPE_ROT // 2, 1) * t2
            put(col, a)


def _qkv_rope(x, g, w, tables, q_scale, group=0, dilation=1, *, tm=512, tn=512):
    b, s, d = x.shape
    assert s % tm == 0 and tm % (dilation * BF16_ROWS) == 0 and QKV_GROUP % tn == 0
    rows = tm // dilation
    if dilation > 1:
        tables = tuple(t.reshape(s // tm, rows, dilation, LANES).transpose(0, 2, 1, 3).reshape(s, LANES)
                       for t in tables)
    tab_spec = pl.BlockSpec((tm, LANES), lambda bi, i: (i, 0))
    in_specs = [pl.BlockSpec((None, tm, d), lambda bi, i: (bi, i, 0)),
                _const_spec((1, d), lambda bi, i: (0, 0)),
                _const_spec((d, QKV_GROUP), lambda bi, i: (0, group)),
                tab_spec, tab_spec, tab_spec]
    args = [x, g.reshape(1, d), w, *tables]
    if dilation > 1:
        in_specs.append(_const_spec((tm, tm), lambda bi, i: (0, 0)))
        args.append(jnp.asarray(_residue_perm(tm, dilation), BF16))
    return pl.pallas_call(
        functools.partial(_qkv_rope_kernel, tn=tn, q_scale=q_scale, dilation=dilation),
        grid=(b, s // tm),
        in_specs=in_specs,
        out_specs=pl.BlockSpec((None, dilation, rows, QKV_GROUP), lambda bi, i: (bi, 0, i, 0)),
        out_shape=jax.ShapeDtypeStruct((b, dilation, s // dilation, QKV_GROUP), BF16),
        compiler_params=_cparams(2),
        name=f"qkv_rope_d{dilation}",
    )(*args)


def _hgrn_in_kernel(x_ref, g_ref, w_ref, lb_ref, act_ref, logf_ref, *, tn, layer):
    h = _rms(x_ref[...], g_ref[...], NORM_EPS).astype(BF16)
    logits = lb_ref[...]
    e = jnp.exp(logits - jnp.max(logits, axis=0, keepdims=True))
    p = e / jnp.sum(e, axis=0, keepdims=True)
    cum = p[0:1]
    for r in range(1, layer + 1):
        cum = cum + p[r:r + 1]
    lb = cum - p[0:1]
    log_lb = jnp.log(lb)
    log_1m_lb = jnp.log1p(-lb)

    for cn in range(5 * D_MODEL // tn):
        col = cn * tn
        part, off = col // D_MODEL, col % D_MODEL
        acc = jnp.dot(h, w_ref[:, col:col + tn], preferred_element_type=F32)
        if part == 0:
            act_ref[:, off:off + tn] = (acc * jax.nn.sigmoid(acc) * F32(C_DIM ** -0.5)).astype(act_ref.dtype)
        elif part in (1, 2):
            log_sig = jnp.minimum(acc, 0.0) - _softplus_neg_abs(acc)
            lo = (part - 1) * D_MODEL + off
            logf_ref[:, lo:lo + tn] = _logaddexp(log_lb[:, off:off + tn], log_1m_lb[:, off:off + tn] + log_sig)
        elif part == 3:
            act_ref[:, D_MODEL + off:D_MODEL + off + tn] = acc.astype(act_ref.dtype)
        else:
            act_ref[:, 2 * D_MODEL + off:2 * D_MODEL + off + tn] = (acc * jax.nn.sigmoid(acc)).astype(act_ref.dtype)


def _hgrn_in(x, g, w, lb_logits, layer, *, tm=512, tn=512):
    m, d = x.shape
    assert m % tm == 0 and D_MODEL % tn == 0
    return pl.pallas_call(
        functools.partial(_hgrn_in_kernel, tn=tn, layer=layer),
        grid=(m // tm,),
        in_specs=[pl.BlockSpec((tm, d), lambda i: (i, 0)),
                  _const_spec((1, d), lambda i: (0, 0)),
                  _const_spec((d, 5 * D_MODEL), lambda i: (0, 0)),
                  _const_spec((DEPTH, D_MODEL), lambda i: (0, 0))],
        out_specs=[pl.BlockSpec((tm, 3 * D_MODEL), lambda i: (i, 0)),
                   pl.BlockSpec((tm, 2 * D_MODEL), lambda i: (i, 0))],
        out_shape=[jax.ShapeDtypeStruct((m, 3 * D_MODEL), BF16),
                   jax.ShapeDtypeStruct((m, 2 * D_MODEL), F32)],
        compiler_params=_cparams(1),
        name="hgrn_in",
    )(x, g.reshape(1, d), w, lb_logits)


def _mm_norm_res_kernel(a_ref, w_ref, g_ref, x_ref, o_ref, *, sub):
    for rb in range(a_ref.shape[0] // sub):
        rows = slice(rb * sub, (rb + 1) * sub)
        h = jnp.dot(a_ref[rows, :], w_ref[...], preferred_element_type=F32)
        o_ref[rows, :] = x_ref[rows, :] + _rms(h, g_ref[...], NORM_EPS)


def _mm_norm_res(a, w, g, x, *, tm=1024, sub=256):
    m, k = a.shape
    d = w.shape[1]
    assert m % tm == 0 and tm % sub == 0
    return pl.pallas_call(
        functools.partial(_mm_norm_res_kernel, sub=sub),
        grid=(m // tm,),
        in_specs=[pl.BlockSpec((tm, k), lambda i: (i, 0)),
                  _const_spec((k, d), lambda i: (0, 0)),
                  _const_spec((1, d), lambda i: (0, 0)),
                  pl.BlockSpec((tm, d), lambda i: (i, 0))],
        out_specs=pl.BlockSpec((tm, d), lambda i: (i, 0)),
        out_shape=jax.ShapeDtypeStruct((m, d), F32),
        compiler_params=_cparams(1),
        name="mm_norm_res",
    )(a, w, g.reshape(1, d), x)


def _ffn_in_kernel(x_ref, xp_ref, xn_ref, g_ref, w_ref, cw_ref, cb_ref, o_ref, h_scr, u_scr, *, tm, tn, seq):
    i = pl.program_id(0)
    n_tiles = pl.num_programs(0) - 1
    halo = BF16_ROWS
    rows = tm + 2 * halo
    n_chunks = D_FF // tn
    slot_new = i % 2
    slot_old = 1 - slot_new

    @pl.when(i == 0)
    def _():
        u_scr[...] = jnp.zeros(u_scr.shape, F32)

    t = jnp.minimum(i, n_tiles - 1)
    g = g_ref[...]
    first = (t * tm) % seq == 0
    last = ((t + 1) * tm) % seq == 0
    hp = _rms(xp_ref[...], g, NORM_EPS)
    hn = _rms(xn_ref[...], g, NORM_EPS)
    h_scr[0:halo] = jnp.where(first, 0.0, hp).astype(BF16)
    h_scr[halo:halo + tm] = _rms(x_ref[...], g, NORM_EPS).astype(BF16)
    h_scr[halo + tm:rows] = jnp.where(last, 0.0, hn).astype(BF16)

    def matmul(cn):
        u_scr[slot_new, cn] = jnp.dot(h_scr[...], w_ref[:, cn * tn:(cn + 1) * tn], preferred_element_type=F32)

    def conv(cn, scale=None):
        cw = cw_ref[:, cn * tn:(cn + 1) * tn]
        cb = cb_ref[:, cn * tn:(cn + 1) * tn]
        if scale is not None:
            cw, cb = cw * scale, cb * scale
        return (cb + u_scr[slot_old, cn, halo - 1:halo - 1 + tm, :] * cw[0:1]
                + u_scr[slot_old, cn, halo:halo + tm, :] * cw[1:2]
                + u_scr[slot_old, cn, halo + 1:halo + 1 + tm, :] * cw[2:3])

    k0 = math.sqrt(2.0 / math.pi)
    for cn in range(n_chunks):
        matmul(2 * cn)
        matmul(2 * cn + 1)
        a = conv(cn)
        half_b = conv(n_chunks + cn, F32(0.5))
        th = jnp.tanh(a * (F32(k0) + F32(k0 * 0.044715) * (a * a)))
        o_ref[:, cn * tn:(cn + 1) * tn] = ((a * half_b) * (1.0 + th)).astype(o_ref.dtype)


def _ffn_in(x, g, w, conv_w, conv_b, seq, *, tm=512, tn=256):
    m, d = x.shape
    assert seq % tm == 0 and m % seq == 0 and D_FF % tn == 0
    halo = BF16_ROWS
    hb = tm // halo
    last_blk = m // halo - 1
    n_tiles = m // tm
    cur = lambda i: jnp.minimum(i, n_tiles - 1)
    return pl.pallas_call(
        functools.partial(_ffn_in_kernel, tm=tm, tn=tn, seq=seq),
        grid=(n_tiles + 1,),
        in_specs=[pl.BlockSpec((tm, d), lambda i: (cur(i), 0)),
                  pl.BlockSpec((halo, d), lambda i: (jnp.maximum(cur(i) * hb - 1, 0), 0)),
                  pl.BlockSpec((halo, d), lambda i: (jnp.minimum((cur(i) + 1) * hb, last_blk), 0)),
                  _const_spec((1, d), lambda i: (0, 0)),
                  _const_spec((d, 2 * D_FF), lambda i: (0, 0)),
                  _const_spec((3, 2 * D_FF), lambda i: (0, 0)),
                  _const_spec((1, 2 * D_FF), lambda i: (0, 0))],
        out_specs=pl.BlockSpec((tm, D_FF), lambda i: (jnp.maximum(i - 1, 0), 0)),
        out_shape=jax.ShapeDtypeStruct((m, D_FF), BF16),
        scratch_shapes=[pltpu.VMEM((tm + 2 * halo, d), BF16),
                        pltpu.VMEM((2, 2 * D_FF // tn, tm + 2 * halo, tn), F32)],
        compiler_params=_cparams(1),
        name="ffn_in",
    )(x, x, x, g.reshape(1, d), w, conv_w, conv_b.reshape(1, -1))


def _diff_attn_kernel(q_ref, k_ref, v_ref, lam_ref, g_ref, o_ref, q2_scr, vx_scr, m_scr, acc_scr, s_scr,
                      *, tq, tc, unroll, lam_init):
    seq = k_ref.shape[0]
    q = q_ref[...]
    lane = lax.broadcasted_iota(jnp.int32, q.shape, 1)
    zero = jnp.zeros_like(q)
    q2_scr[0:tq] = jnp.where(lane < A_HEAD_DIM, q, zero)
    q2_scr[tq:2 * tq] = jnp.where(lane >= A_HEAD_DIM, q, zero)

    @pl.when(pl.program_id(2) == 0)
    def _():
        vx_scr[:, 0:A_V_DIM] = v_ref[...]
        vx_scr[:, A_V_DIM:2 * A_V_DIM] = jnp.ones((seq, A_V_DIM), BF16)

    def scores(c, slot):
        r0 = c * tc if isinstance(c, int) else pl.multiple_of(c * tc, tc)
        s_scr[slot] = lax.dot_general(q2_scr[...], k_ref[pl.ds(r0, tc), :], (((1,), (1,)), ((), ())),
                                      preferred_element_type=F32)

    def softmax_pv(c, slot, first=False):
        r0 = c * tc if isinstance(c, int) else pl.multiple_of(c * tc, tc)
        slabs = [s_scr[slot, :, j * LANES:(j + 1) * LANES] for j in range(tc // LANES)]
        mx = slabs[0]
        for sl in slabs[1:]:
            mx = jnp.maximum(mx, sl)
        m_cur = jnp.max(mx, axis=1, keepdims=True)
        if first:
            m_new = jnp.broadcast_to(m_cur, m_scr.shape)
        else:
            m_prev = m_scr[...]
            m_new = jnp.maximum(m_prev, m_cur)
            alpha = jnp.exp2(m_prev - m_new)
        p = jnp.concatenate([jnp.exp2(sl - m_new).astype(BF16) for sl in slabs], axis=1)
        pv = jnp.dot(p, vx_scr[pl.ds(r0, tc), :], preferred_element_type=F32)
        if first:
            acc_scr[...] = pv
        else:
            acc_scr[:, 0:A_V_DIM] = alpha * acc_scr[:, 0:A_V_DIM] + pv[:, 0:A_V_DIM]
            acc_scr[:, A_V_DIM:2 * A_V_DIM] = alpha * acc_scr[:, A_V_DIM:2 * A_V_DIM] + pv[:, A_V_DIM:2 * A_V_DIM]
        m_scr[...] = m_new

    n_chunks = seq // tc

    def pairs(p0, first):
        for u in range(unroll):
            c = 2 * (p0 + u)
            nxt = min(c + 2, n_chunks - 1) if isinstance(c, int) else jnp.minimum(c + 2, n_chunks - 1)
            scores(c + 1, 1)
            softmax_pv(c, 0, first=first and u == 0)
            scores(nxt, 0)
            softmax_pv(c + 1, 1)

    def body(it, carry):
        pairs(unroll * it, False)
        return carry

    scores(0, 0)
    pairs(0, True)
    lax.fori_loop(1, n_chunks // (2 * unroll), body, 0)

    o = acc_scr[:, 0:A_V_DIM] / acc_scr[:, A_V_DIM:2 * A_V_DIM]
    lv = lam_ref[...]
    lam = (jnp.exp(jnp.sum(lv[0:1] * lv[1:2], axis=1, keepdims=True))
           - jnp.exp(jnp.sum(lv[2:3] * lv[3:4], axis=1, keepdims=True)) + F32(lam_init))
    od = o[0:tq] - lam * o[tq:2 * tq]
    y = _rms(od, g_ref[...], A_SUBLN_EPS) * F32(1.0 - lam_init)
    o_ref[...] = y.astype(o_ref.dtype)


def _diff_attn(qkv, lam_vecs, subln_g, lam_init, *, tq=1024, tc=512, unroll=2):
    b, s, _ = qkv.shape
    tq = min(tq, s)
    assert s % tq == 0 and s % (2 * unroll * tc) == 0
    blk = lambda rows, off: pl.BlockSpec((None, rows, LANES), off)
    return pl.pallas_call(
        functools.partial(_diff_attn_kernel, tq=tq, tc=tc, unroll=unroll, lam_init=lam_init),
        grid=(b, A_HEADS, s // tq),
        in_specs=[blk(tq, lambda bi, h, i: (bi, i, h)),
                  blk(s, lambda bi, h, i: (bi, 0, A_HEADS + h)),
                  blk(s, lambda bi, h, i: (bi, 0, 2 * A_HEADS + h)),
                  pl.BlockSpec((4, A_HEAD_DIM), lambda bi, h, i: (0, 0)),
                  pl.BlockSpec((1, A_V_DIM), lambda bi, h, i: (0, 0))],
        out_specs=blk(tq, lambda bi, h, i: (bi, i, h)),
        out_shape=jax.ShapeDtypeStruct((b, s, A_HEADS * A_V_DIM), BF16),
        scratch_shapes=[pltpu.VMEM((2 * tq, LANES), BF16),
                        pltpu.VMEM((s, 2 * A_V_DIM), BF16),
                        pltpu.VMEM((2 * tq, LANES), F32),
                        pltpu.VMEM((2 * tq, 2 * A_V_DIM), F32),
                        pltpu.VMEM((2, 2 * tq, tc), F32)],
        compiler_params=_cparams(3),
        name="diff_attn",
    )(qkv, qkv, qkv, lam_vecs, subln_g.reshape(1, A_V_DIM))


def _dil_attn_kernel(q_ref, k_ref, kp_ref, kn_ref, v_ref, vp_ref, vn_ref, o_ref, lse_ref, kbuf, vbuf,
                     *, tl, length):
    hw = B_HALF_W
    t0 = pl.program_id(2) * tl

    for buf, main, prev, nxt in ((kbuf, k_ref, kp_ref, kn_ref), (vbuf, v_ref, vp_ref, vn_ref)):
        buf[0:hw] = prev[...]
        buf[hw:hw + tl] = main[...]
        buf[hw + tl:2 * hw + tl] = nxt[...]

    a_io = lax.broadcasted_iota(jnp.int32, (B_QBLK, B_KBLK), 0)
    c_io = lax.broadcasted_iota(jnp.int32, (B_QBLK, B_KBLK), 1)
    band = (c_io >= a_io) & (c_io <= a_io + 2 * hw)
    lane = lax.broadcasted_iota(jnp.int32, (B_QBLK, LANES), 1)

    def q_block(jb, carry):
        r0 = pl.multiple_of(jb * B_QBLK, B_QBLK)
        ki = t0 + r0 - hw + c_io
        valid = band & (ki >= 0) & (ki < length)
        valid2 = jnp.concatenate([valid, valid], axis=0)
        lse_tile = jnp.zeros((B_QBLK, LANES), F32)
        for hp in range(B_HEADS // 2):
            cols = slice(hp * LANES, (hp + 1) * LANES)
            q = q_ref[pl.ds(r0, B_QBLK), cols]
            zero = jnp.zeros_like(q)
            q2 = jnp.concatenate([jnp.where(lane < B_HEAD_DIM, q, zero),
                                  jnp.where(lane >= B_HEAD_DIM, q, zero)], axis=0)
            s = lax.dot_general(q2, kbuf[pl.ds(r0, B_KBLK), cols], (((1,), (1,)), ((), ())),
                                preferred_element_type=F32)
            s = jnp.where(valid2, s, -jnp.inf)
            m = jnp.max(s, axis=1, keepdims=True)
            p = jnp.exp(s - m)
            l = jnp.sum(p, axis=1, keepdims=True)
            pv = jnp.dot(p.astype(BF16), vbuf[pl.ds(r0, B_KBLK), cols], preferred_element_type=F32) / l
            o_ref[pl.ds(r0, B_QBLK), cols] = jnp.where(lane < B_HEAD_DIM, pv[0:B_QBLK],
                                                       pv[B_QBLK:2 * B_QBLK]).astype(o_ref.dtype)
            lse = m + jnp.log(l)
            lse_tile = jnp.where(lane == 2 * hp, lse[0:B_QBLK], lse_tile)
            lse_tile = jnp.where(lane == 2 * hp + 1, lse[B_QBLK:2 * B_QBLK], lse_tile)
        lse_ref[pl.ds(r0, B_QBLK), :] = lse_tile
        return carry

    lax.fori_loop(0, tl // B_QBLK, q_block, 0, unroll=2)


def _dil_attn_group(qkv, *, tl=512):
    b, dilation, length, _ = qkv.shape
    tl = min(tl, length)
    assert length % tl == 0 and tl % B_QBLK == 0
    hw = B_HALF_W
    width = B_HEADS * B_HEAD_DIM
    per_tile = tl // hw
    n_hw = length // hw
    main = lambda part: pl.BlockSpec((None, None, tl, width), lambda bi, r, i: (bi, r, i, part))
    before = lambda part: pl.BlockSpec((None, None, hw, width),
                                       lambda bi, r, i: (bi, r, jnp.maximum(i * per_tile - 1, 0), part))
    after = lambda part: pl.BlockSpec((None, None, hw, width),
                                      lambda bi, r, i: (bi, r, jnp.minimum((i + 1) * per_tile, n_hw - 1), part))
    return pl.pallas_call(
        functools.partial(_dil_attn_kernel, tl=tl, length=length),
        grid=(b, dilation, length // tl),
        in_specs=[main(0), main(1), before(1), after(1), main(2), before(2), after(2)],
        out_specs=[pl.BlockSpec((None, None, tl, width), lambda bi, r, i: (bi, r, i, 0)),
                   pl.BlockSpec((None, None, tl, LANES), lambda bi, r, i: (bi, r, i, 0))],
        out_shape=[jax.ShapeDtypeStruct((b, dilation, length, width), BF16),
                   jax.ShapeDtypeStruct((b, dilation, length, LANES), F32)],
        scratch_shapes=[pltpu.VMEM((tl + 2 * hw, width), BF16), pltpu.VMEM((tl + 2 * hw, width), BF16)],
        compiler_params=_cparams(3),
        name=f"dil_attn_d{dilation}",
    )(*([qkv] * 7))


def _split3(x):
    hi = x.astype(BF16)
    r1 = x - hi.astype(F32)
    mid = r1.astype(BF16)
    lo = (r1 - mid.astype(F32)).astype(BF16)
    return hi, mid, lo


def _dil_merge_kernel(*refs, n_groups):
    o_refs = refs[:n_groups]
    l_refs = refs[n_groups:2 * n_groups]
    perm_refs = refs[2 * n_groups:3 * n_groups - 1]
    expand_ref, out_ref = refs[3 * n_groups - 1:]
    t = out_ref.shape[0]
    outs, lses = [], []
    for gi in range(n_groups):
        o = o_refs[gi][...].reshape(t, o_refs[gi].shape[-1])
        lse = l_refs[gi][...].reshape(t, LANES)
        if gi == 0:
            outs.append(o.astype(F32))
            lses.append(lse)
        else:
            perm = perm_refs[gi - 1][...]
            outs.append(jnp.dot(perm, o, preferred_element_type=F32))
            l3 = jnp.dot(perm, jnp.concatenate(_split3(lse), axis=1), preferred_element_type=F32)
            lses.append(l3[:, 0:LANES] + l3[:, LANES:2 * LANES] + l3[:, 2 * LANES:3 * LANES])
    m = functools.reduce(jnp.maximum, lses)
    es = [jnp.exp(l - m) for l in lses]
    den = functools.reduce(lambda a, b: a + b, es)
    acc = jnp.zeros((t, out_ref.shape[1]), F32)
    for gi in range(n_groups):
        w = es[gi] / den
        hi = w.astype(BF16)
        lo = (w - hi.astype(F32)).astype(BF16)
        wide = jnp.dot(jnp.concatenate([hi, lo], axis=1), expand_ref[...], preferred_element_type=F32)
        acc = acc + wide * outs[gi]
    out_ref[...] = acc.astype(out_ref.dtype)


def _dil_merge(outs, lses, *, t=512):
    b, _, s, width = outs[0].shape
    n_groups = len(outs)
    in_specs, args = [], []
    for arrs, lanes in ((outs, width), (lses, LANES)):
        for a in arrs:
            d = a.shape[1]
            in_specs.append(pl.BlockSpec((None, d, t // d, lanes), lambda bi, i: (bi, 0, i, 0)))
            args.append(a)
    for a in outs[1:]:
        in_specs.append(_const_spec((t, t), lambda bi, i: (0, 0)))
        args.append(jnp.asarray(_residue_perm(t, a.shape[1]).T, BF16))
    expand = np.zeros((LANES, width), np.float32)
    expand[np.arange(width) // B_HEAD_DIM, np.arange(width)] = 1.0
    in_specs.append(_const_spec((2 * LANES, width), lambda bi, i: (0, 0)))
    args.append(jnp.asarray(np.concatenate([expand, expand], axis=0), BF16))
    return pl.pallas_call(
        functools.partial(_dil_merge_kernel, n_groups=n_groups),
        grid=(b, s // t),
        in_specs=in_specs,
        out_specs=pl.BlockSpec((None, t, width), lambda bi, i: (bi, i, 0)),
        out_shape=jax.ShapeDtypeStruct((b, s, width), BF16),
        compiler_params=_cparams(2),
        name="dil_merge",
    )(*args)


def _scan_constants(reverse):
    c = C_CHUNK
    t = np.arange(c)[:, None]
    u = np.arange(c)[None, :]
    if not reverse:
        mats = [u <= t, u > t]
    else:
        mats = [u >= t, u < t]
    masks = []
    for hs in C_LEVELS:
        base = (t // (2 * hs)) * (2 * hs)
        mid = base + hs - 1
        same = (u // (2 * hs)) == (t // (2 * hs))
        if not reverse:
            m = np.where(t > mid, (u > mid) & (u <= t), (u > t) & (u <= mid))
            masks.append(same & (t > mid) & (u <= mid))
        else:
            m = np.where(t <= mid, (u >= t) & (u <= mid), (u > mid) & (u < t))
            masks.append(same & (t <= mid) & (u > mid))
        mats.append(m & same)
    stack = np.concatenate([np.asarray(m, np.float32) for m in mats], axis=0)
    return jnp.asarray(stack, BF16), jnp.asarray(np.stack(masks).astype(np.float32))


def _hgrn_scan_kernel(*refs, tt, hp, reverse, final):
    q_ref, v_ref, lf_ref, mstack_ref, masks_ref = refs[:5]
    refs = refs[5:]
    if final:
        ofw_ref, gate_ref, gn_ref = refs[:3]
        refs = refs[3:]
    o_ref, st_scr, g_scr, k_scr = refs
    c = C_CHUNK
    n_chunks = tt // c
    nt_contract = (((1,), (1,)), ((), ()))

    @pl.when(pl.program_id(2) == 0)
    def _():
        st_scr[...] = jnp.zeros(st_scr.shape, F32)

    row8 = lax.broadcasted_iota(jnp.int32, (C_DIAG, C_DIM), 0)
    lane_c = lax.broadcasted_iota(jnp.int32, (C_DIAG, c), 1)

    def head_chunk(r0, hi):
        cols = slice(hi * C_DIM, (hi + 1) * C_DIM)
        lf = lf_ref[pl.ds(r0, c), cols] * F32(LOG2E)
        top = lf.astype(BF16)
        low = (lf - top.astype(F32)).astype(BF16)
        d2 = jnp.dot(mstack_ref[...], jnp.concatenate([top, low], axis=1), preferred_element_type=F32)
        d = d2[:, 0:C_DIM] + d2[:, C_DIM:2 * C_DIM]
        gq = d[0:c]
        gk = d[c:2 * c]

        q = q_ref[pl.ds(r0, c), cols].astype(F32)
        v = v_ref[pl.ds(r0, c), cols]
        kk = 1.0 - jnp.exp2(lf)
        st = st_scr[hi]
        g_scr[hi] = gq
        k_scr[hi] = kk

        o = lax.dot_general((q * jnp.exp2(gq)).astype(BF16), st.astype(BF16), nt_contract,
                            preferred_element_type=F32)

        a = jnp.zeros((c, c), F32)
        for li in range(len(C_LEVELS)):
            e = jnp.exp2(-jnp.abs(d[(2 + li) * c:(3 + li) * c]))
            pq = (q * e).astype(BF16)
            pk = (kk * e).astype(BF16)
            a = a + lax.dot_general(pq, pk, nt_contract, preferred_element_type=F32) * masks_ref[li]

        blocks = []
        for bi in range(c // C_DIAG):
            rows = slice(bi * C_DIAG, (bi + 1) * C_DIAG)
            gb, qb = gq[rows], q[rows]
            ablk = jnp.zeros((C_DIAG, c), F32)
            for si in range(C_DIAG):
                row = bi * C_DIAG + si
                g_row = jnp.broadcast_to(g_scr[hi, row:row + 1, :], (C_DIAG, C_DIM))
                k_row = jnp.broadcast_to(k_scr[hi, row:row + 1, :], (C_DIAG, C_DIM))
                keep = (row8 <= si) if reverse else (row8 >= si)
                dec = jnp.exp2(jnp.where(keep, gb - g_row, -jnp.inf))
                sc = jnp.sum(qb * dec * k_row, axis=1, keepdims=True)
                ablk = jnp.where(lane_c == row, sc, ablk)
            blocks.append(ablk)
        a = a + jnp.concatenate(blocks, axis=0)

        o = o + jnp.dot(a.astype(BF16), v, preferred_element_type=F32)

        kd = (kk * jnp.exp2(gk)).astype(BF16)
        upd = lax.dot_general(v, kd, (((0,), (0,)), ((), ())), preferred_element_type=F32)
        g_tot = gq[0:1] if reverse else gq[c - 1:c]
        st_scr[hi] = st * jnp.exp2(g_tot) + upd

        if final:
            tot = ofw_ref[pl.ds(r0, c), cols] + o
            y = _rms(tot, gn_ref[...], NORM_EPS) * gate_ref[pl.ds(r0, c), cols].astype(F32)
            o_ref[pl.ds(r0, c), cols] = y.astype(o_ref.dtype)
        else:
            o_ref[pl.ds(r0, c), cols] = o

    def chunk(ci, carry):
        cidx = (n_chunks - 1 - ci) if reverse else ci
        r0 = pl.multiple_of(cidx * c, c)
        for hi in range(hp):
            head_chunk(r0, hi)
        return carry

    lax.fori_loop(0, n_chunks, chunk, 0, unroll=2)


def _hgrn_scan(act, logf, reverse, o_fw=None, gnorm_g=None, *, tt=512, hp=8):
    b, s, _ = act.shape
    tt = min(tt, s)
    assert s % tt == 0 and tt % C_CHUNK == 0 and C_HEADS % hp == 0
    nt = s // tt
    final = o_fw is not None
    mstack, masks = _scan_constants(reverse)
    groups = C_HEADS // hp
    tmap = (lambda t: nt - 1 - t) if reverse else (lambda t: t)
    blk = lambda part: pl.BlockSpec((None, tt, hp * C_DIM), lambda bi, hg, t: (bi, tmap(t), part * groups + hg))
    in_specs = [blk(0), blk(1), blk(1 if reverse else 0),
                _const_spec(mstack.shape, lambda bi, hg, t: (0, 0)),
                _const_spec(masks.shape, lambda bi, hg, t: (0, 0, 0))]
    args = [act, act, logf, mstack, masks]
    if final:
        in_specs += [blk(0), blk(2), _const_spec((1, C_DIM), lambda bi, hg, t: (0, 0))]
        args += [o_fw, act, gnorm_g.reshape(1, C_DIM)]
    return pl.pallas_call(
        functools.partial(_hgrn_scan_kernel, tt=tt, hp=hp, reverse=reverse, final=final),
        grid=(b, groups, nt),
        in_specs=in_specs,
        out_specs=blk(0),
        out_shape=jax.ShapeDtypeStruct((b, s, C_HEADS * C_DIM), BF16 if final else F32),
        scratch_shapes=[pltpu.VMEM((hp, C_DIM, C_DIM), F32),
                        pltpu.VMEM((hp, C_CHUNK, C_DIM), F32),
                        pltpu.VMEM((hp, C_CHUNK, C_DIM), F32)],
        compiler_params=_cparams(3),
        name="hgrn_scan_bw" if reverse else "hgrn_scan_fw",
    )(*args)


def _diff_lambda_init(layer):
    return 0.8 - 0.6 * math.exp(-0.3 * layer)


def _trunk(x3, p):
    b, s, d = x3.shape
    x = x3.reshape(b * s, d)
    tables = _rope_tables(s)
    for layer in range(DEPTH):
        kind = layer % N_MIXERS
        j = layer // N_MIXERS
        g = p["norm_g"][layer]
        x3 = x.reshape(b, s, d)
        if kind == 0:
            qkv = _qkv_rope(x3, g[0], p["a_w_qkv"][j], tables, A_HEAD_DIM ** -0.5 * LOG2E)
            o = _diff_attn(qkv.reshape(b, s, -1), p["a_lambda"][j], p["a_subln_g"][j], _diff_lambda_init(layer))
            w_o = p["a_w_o"][j]
        elif kind == 1:
            outs, lses = [], []
            for group, (_, dilation) in enumerate(B_PATTERNS):
                qkv = _qkv_rope(x3, g[0], p["b_w_qkv"][j], tables, B_HEAD_DIM ** -0.5, group, dilation)
                o_g, lse_g = _dil_attn_group(qkv)
                outs.append(o_g)
                lses.append(lse_g)
            o = _dil_merge(outs, lses)
            w_o = p["b_w_o"][j]
        else:
            act, logf = _hgrn_in(x, g[0], p["c_w_in"][j], p["c_lb_logits"], layer)
            act, logf = act.reshape(b, s, -1), logf.reshape(b, s, -1)
            o_fw = _hgrn_scan(act, logf, False)
            o = _hgrn_scan(act, logf, True, o_fw, p["c_gnorm_g"][j])
            w_o = p["c_w_o"][j]
        x = _mm_norm_res(o.reshape(b * s, -1), w_o, g[1], x)
        u = _ffn_in(x, g[2], p["f_w_in"][layer], p["f_conv_w"][layer], p["f_conv_b"][layer], s)
        x = _mm_norm_res(u, p["f_w_out"][layer], g[3], x)
    return x.reshape(b, s, d)


def kernel(x_prompt, x_sample, norm_g, a_w_qkv, a_lambda, a_subln_g, a_w_o, b_w_qkv, b_w_o, c_w_in,
           c_lb_logits, c_gnorm_g, c_w_o, f_w_in, f_conv_w, f_conv_b, f_w_out):
    bf = lambda w: w.astype(BF16)
    p = dict(norm_g=norm_g, a_w_qkv=bf(a_w_qkv), a_lambda=a_lambda, a_subln_g=a_subln_g, a_w_o=bf(a_w_o),
             b_w_qkv=bf(b_w_qkv), b_w_o=bf(b_w_o), c_w_in=bf(c_w_in), c_lb_logits=c_lb_logits,
             c_gnorm_g=c_gnorm_g, c_w_o=bf(c_w_o), f_w_in=bf(f_w_in), f_conv_w=f_conv_w,
             f_conv_b=f_conv_b, f_w_out=bf(f_w_out))
    return _trunk(x_prompt, p), _trunk(x_sample, p)
```

```python
import functools
import math

import numpy as np
import jax
import jax.numpy as jnp
from jax import lax
from jax.experimental import pallas as pl
from jax.experimental.pallas import tpu as pltpu

F32 = jnp.float32
BF16 = jnp.bfloat16

D_MODEL = 1024
DEPTH = 4
N_MIXERS = 3
ROPE_THETA = 500000.0
NORM_EPS = 1e-6

A_HEADS = 8
A_HEAD_DIM = 64
A_V_DIM = 128
A_SUBLN_EPS = 1e-5

B_HEADS = 16
B_HEAD_DIM = 64
B_PATTERNS = ((128, 1), (512, 4), (2048, 16))
B_HALF_W = 64
B_QBLK = 2 * B_HALF_W
B_KBLK = 4 * B_HALF_W

C_HEADS = 8
C_DIM = 128
C_CHUNK = 128
C_LEVELS = (64, 32, 16, 8)
C_DIAG = 8

D_FF = 2816
QKV_GROUP = 3 * D_MODEL
ROPE_ROT = 16
LANES = 128
LOG2E = math.log2(math.e)
BF16_ROWS = 16

V7X_VMEM_BYTES = 64 * 1024 * 1024
V7X_VMEM_LIMIT = V7X_VMEM_BYTES - V7X_VMEM_BYTES // 8


def _cparams(n_axes):
    return pltpu.CompilerParams(dimension_semantics=("arbitrary",) * n_axes,
                                vmem_limit_bytes=V7X_VMEM_LIMIT)


def _const_spec(shape, index_map):
    return pl.BlockSpec(shape, index_map, pipeline_mode=pl.Buffered(1))


def _rms(x, g, eps):
    ms = jnp.mean(x * x, axis=-1, keepdims=True)
    return x * lax.rsqrt(ms + eps) * g


def _softplus_neg_abs(d):
    return jnp.log(1.0 + jnp.exp(-jnp.abs(d)))


def _logaddexp(a, b):
    return jnp.maximum(a, b) + _softplus_neg_abs(a - b)


def _rope_tables(seq):
    half = ROPE_ROT // 2
    inv_freq = ROPE_THETA ** (-jnp.arange(half, dtype=F32) / half)
    ang = jnp.arange(seq, dtype=F32)[:, None] * inv_freq[None, :]
    cos, sin = jnp.cos(ang), jnp.sin(ang)
    rest = B_HEAD_DIM - ROPE_ROT
    c = jnp.concatenate([cos, cos, jnp.ones((seq, rest), F32)], axis=1)
    s1 = jnp.concatenate([-sin, jnp.zeros((seq, rest + half), F32)], axis=1)
    s2 = jnp.concatenate([jnp.zeros((seq, half), F32), sin, jnp.zeros((seq, rest), F32)], axis=1)
    rep = LANES // B_HEAD_DIM
    return tuple(jnp.tile(t, (1, rep)) for t in (c, s1, s2))


def _residue_perm(tm, dilation):
    rows = tm // dilation
    rho = np.arange(tm)
    tok = (rho % rows) * dilation + rho // rows
    perm = np.zeros((tm, tm), np.float32)
    perm[rho, tok] = 1.0
    return perm


def _qkv_rope_kernel(*refs, tn, q_scale, dilation):
    if dilation > 1:
        x_ref, g_ref, w_ref, c_ref, s1_ref, s2_ref, perm_ref, o_ref = refs
    else:
        x_ref, g_ref, w_ref, c_ref, s1_ref, s2_ref, o_ref = refs
    tm = x_ref.shape[0]
    rows = tm // dilation

    h = _rms(x_ref[...], g_ref[...], NORM_EPS).astype(BF16)
    if dilation > 1:
        h = jnp.dot(perm_ref[...], h, preferred_element_type=F32).astype(BF16)

    def put(col, val):
        for rr in range(dilation):
            o_ref[rr, :, col:col + LANES] = val[rr * rows:(rr + 1) * rows].astype(o_ref.dtype)

    c, s1, s2 = c_ref[...], s1_ref[...], s2_ref[...]
    cq, s1q, s2q = c * F32(q_scale), s1 * F32(q_scale), s2 * F32(q_scale)
    for cn in range(QKV_GROUP // tn):
        acc = jnp.dot(h, w_ref[:, cn * tn:(cn + 1) * tn], preferred_element_type=F32)
        for cb in range(tn // LANES):
            col = cn * tn + cb * LANES
            a = acc[:, cb * LANES:(cb + 1) * LANES]
            if col < 2 * D_MODEL:
                tc, t1, t2 = (cq, s1q, s2q) if col < D_MODEL else (c, s1, s2)
                a = a * tc + pltpu.roll(a, LANES - ROPE_ROT // 2, 1) * t1 + pltpu.roll(a, ROPE_ROT // 2, 1) * t2
            put(col, a)


def _qkv_rope(x, g, w, tables, q_scale, group=0, dilation=1, *, tm=512, tn=512):
    b, s, d = x.shape
    assert s % tm == 0 and tm % (dilation * BF16_ROWS) == 0 and QKV_GROUP % tn == 0
    rows = tm // dilation
    if dilation > 1:
        tables = tuple(t.reshape(s // tm, rows, dilation, LANES).transpose(0, 2, 1, 3).reshape(s, LANES)
                       for t in tables)
    tab_spec = pl.BlockSpec((tm, LANES), lambda bi, i: (i, 0))
    in_specs = [pl.BlockSpec((None, tm, d), lambda bi, i: (bi, i, 0)),
                _const_spec((1, d), lambda bi, i: (0, 0)),
                _const_spec((d, QKV_GROUP), lambda bi, i: (0, group)),
                tab_spec, tab_spec, tab_spec]
    args = [x, g.reshape(1, d), w, *tables]
    if dilation > 1:
        in_specs.append(_const_spec((tm, tm), lambda bi, i: (0, 0)))
        args.append(jnp.asarray(_residue_perm(tm, dilation), BF16))
    return pl.pallas_call(
        functools.partial(_qkv_rope_kernel, tn=tn, q_scale=q_scale, dilation=dilation),
        grid=(b, s // tm),
        in_specs=in_specs,
        out_specs=pl.BlockSpec((None, dilation, rows, QKV_GROUP), lambda bi, i: (bi, 0, i, 0)),
        out_shape=jax.ShapeDtypeStruct((b, dilation, s // dilation, QKV_GROUP), BF16),
        compiler_params=_cparams(2),
        name=f"qkv_rope_d{dilation}",
    )(*args)


def _hgrn_in_kernel(x_ref, g_ref, w_ref, lb_ref, act_ref, logf_ref, *, tn, layer):
    h = _rms(x_ref[...], g_ref[...], NORM_EPS).astype(BF16)
    logits = lb_ref[...]
    e = jnp.exp(logits - jnp.max(logits, axis=0, keepdims=True))
    p = e / jnp.sum(e, axis=0, keepdims=True)
    cum = p[0:1]
    for r in range(1, layer + 1):
        cum = cum + p[r:r + 1]
    lb = cum - p[0:1]
    log_lb = jnp.log(lb)
    log_1m_lb = jnp.log1p(-lb)

    for cn in range(5 * D_MODEL // tn):
        col = cn * tn
        part, off = col // D_MODEL, col % D_MODEL
        acc = jnp.dot(h, w_ref[:, col:col + tn], preferred_element_type=F32)
        if part == 0:
            act_ref[:, off:off + tn] = (acc * jax.nn.sigmoid(acc) * F32(C_DIM ** -0.5)).astype(act_ref.dtype)
        elif part in (1, 2):
            log_sig = jnp.minimum(acc, 0.0) - _softplus_neg_abs(acc)
            lo = (part - 1) * D_MODEL + off
            logf_ref[:, lo:lo + tn] = _logaddexp(log_lb[:, off:off + tn], log_1m_lb[:, off:off + tn] + log_sig)
        elif part == 3:
            act_ref[:, D_MODEL + off:D_MODEL + off + tn] = acc.astype(act_ref.dtype)
        else:
            act_ref[:, 2 * D_MODEL + off:2 * D_MODEL + off + tn] = (acc * jax.nn.sigmoid(acc)).astype(act_ref.dtype)


def _hgrn_in(x, g, w, lb_logits, layer, *, tm=512, tn=512):
    m, d = x.shape
    assert m % tm == 0 and D_MODEL % tn == 0
    return pl.pallas_call(
        functools.partial(_hgrn_in_kernel, tn=tn, layer=layer),
        grid=(m // tm,),
        in_specs=[pl.BlockSpec((tm, d), lambda i: (i, 0)),
                  _const_spec((1, d), lambda i: (0, 0)),
                  _const_spec((d, 5 * D_MODEL), lambda i: (0, 0)),
                  _const_spec((DEPTH, D_MODEL), lambda i: (0, 0))],
        out_specs=[pl.BlockSpec((tm, 3 * D_MODEL), lambda i: (i, 0)),
                   pl.BlockSpec((tm, 2 * D_MODEL), lambda i: (i, 0))],
        out_shape=[jax.ShapeDtypeStruct((m, 3 * D_MODEL), BF16),
                   jax.ShapeDtypeStruct((m, 2 * D_MODEL), F32)],
        compiler_params=_cparams(1),
        name="hgrn_in",
    )(x, g.reshape(1, d), w, lb_logits)


def _mm_norm_res_kernel(a_ref, w_ref, g_ref, x_ref, o_ref, *, sub):
    for rb in range(a_ref.shape[0] // sub):
        rows = slice(rb * sub, (rb + 1) * sub)
        h = jnp.dot(a_ref[rows, :], w_ref[...], preferred_element_type=F32)
        o_ref[rows, :] = x_ref[rows, :] + _rms(h, g_ref[...], NORM_EPS)


def _mm_norm_res(a, w, g, x, *, tm=1024, sub=256):
    m, k = a.shape
    d = w.shape[1]
    assert m % tm == 0 and tm % sub == 0
    return pl.pallas_call(
        functools.partial(_mm_norm_res_kernel, sub=sub),
        grid=(m // tm,),
        in_specs=[pl.BlockSpec((tm, k), lambda i: (i, 0)),
                  _const_spec((k, d), lambda i: (0, 0)),
                  _const_spec((1, d), lambda i: (0, 0)),
                  pl.BlockSpec((tm, d), lambda i: (i, 0))],
        out_specs=pl.BlockSpec((tm, d), lambda i: (i, 0)),
        out_shape=jax.ShapeDtypeStruct((m, d), F32),
        compiler_params=_cparams(1),
        name="mm_norm_res",
    )(a, w, g.reshape(1, d), x)


def _ffn_in_kernel(x_ref, xp_ref, xn_ref, g_ref, w_ref, cw_ref, cb_ref, o_ref, h_scr, u_scr, *, tm, tn, seq):
    i = pl.program_id(0)
    n_tiles = pl.num_programs(0) - 1
    halo = BF16_ROWS
    rows = tm + 2 * halo
    n_chunks = D_FF // tn
    slot_new = i % 2
    slot_old = 1 - slot_new

    @pl.when(i == 0)
    def _():
        u_scr[...] = jnp.zeros(u_scr.shape, F32)

    t = jnp.minimum(i, n_tiles - 1)
    g = g_ref[...]
    first = (t * tm) % seq == 0
    last = ((t + 1) * tm) % seq == 0
    hp = _rms(xp_ref[...], g, NORM_EPS)
    hn = _rms(xn_ref[...], g, NORM_EPS)
    h_scr[0:halo] = jnp.where(first, 0.0, hp).astype(BF16)
    h_scr[halo:halo + tm] = _rms(x_ref[...], g, NORM_EPS).astype(BF16)
    h_scr[halo + tm:rows] = jnp.where(last, 0.0, hn).astype(BF16)

    def matmul(cn):
        u_scr[slot_new, cn] = jnp.dot(h_scr[...], w_ref[:, cn * tn:(cn + 1) * tn], preferred_element_type=F32)

    def conv(cn, scale=None):
        cw = cw_ref[:, cn * tn:(cn + 1) * tn]
        cb = cb_ref[:, cn * tn:(cn + 1) * tn]
        if scale is not None:
            cw, cb = cw * scale, cb * scale
        return (cb + u_scr[slot_old, cn, halo - 1:halo - 1 + tm, :] * cw[0:1]
                + u_scr[slot_old, cn, halo:halo + tm, :] * cw[1:2]
                + u_scr[slot_old, cn, halo + 1:halo + 1 + tm, :] * cw[2:3])

    k0 = math.sqrt(2.0 / math.pi)
    for cn in range(n_chunks):
        matmul(2 * cn)
        matmul(2 * cn + 1)
        a = conv(cn)
        half_b = conv(n_chunks + cn, F32(0.5))
        th = jnp.tanh(a * (F32(k0) + F32(k0 * 0.044715) * (a * a)))
        o_ref[:, cn * tn:(cn + 1) * tn] = ((a * half_b) * (1.0 + th)).astype(o_ref.dtype)


def _ffn_in(x, g, w, conv_w, conv_b, seq, *, tm=512, tn=256):
    m, d = x.shape
    assert seq % tm == 0 and m % seq == 0 and D_FF % tn == 0
    halo = BF16_ROWS
    hb = tm // halo
    last_blk = m // halo - 1
    n_tiles = m // tm
    cur = lambda i: jnp.minimum(i, n_tiles - 1)
    return pl.pallas_call(
        functools.partial(_ffn_in_kernel, tm=tm, tn=tn, seq=seq),
        grid=(n_tiles + 1,),
        in_specs=[pl.BlockSpec((tm, d), lambda i: (cur(i), 0)),
                  pl.BlockSpec((halo, d), lambda i: (jnp.maximum(cur(i) * hb - 1, 0), 0)),
                  pl.BlockSpec((halo, d), lambda i: (jnp.minimum((cur(i) + 1) * hb, last_blk), 0)),
                  _const_spec((1, d), lambda i: (0, 0)),
                  _const_spec((d, 2 * D_FF), lambda i: (0, 0)),
                  _const_spec((3, 2 * D_FF), lambda i: (0, 0)),
                  _const_spec((1, 2 * D_FF), lambda i: (0, 0))],
        out_specs=pl.BlockSpec((tm, D_FF), lambda i: (jnp.maximum(i - 1, 0), 0)),
        out_shape=jax.ShapeDtypeStruct((m, D_FF), BF16),
        scratch_shapes=[pltpu.VMEM((tm + 2 * halo, d), BF16),
                        pltpu.VMEM((2, 2 * D_FF // tn, tm + 2 * halo, tn), F32)],
        compiler_params=_cparams(1),
        name="ffn_in",
    )(x, x, x, g.reshape(1, d), w, conv_w, conv_b.reshape(1, -1))


def _diff_attn_kernel(q_ref, k_ref, v_ref, lam_ref, g_ref, o_ref, q2_scr, vx_scr, m_scr, acc_scr, s_scr,
                      *, tq, tc, unroll, lam_init):
    seq = k_ref.shape[0]
    q = q_ref[...]
    lane = lax.broadcasted_iota(jnp.int32, q.shape, 1)
    zero = jnp.zeros_like(q)
    q2_scr[0:tq] = jnp.where(lane < A_HEAD_DIM, q, zero)
    q2_scr[tq:2 * tq] = jnp.where(lane >= A_HEAD_DIM, q, zero)

    @pl.when(pl.program_id(2) == 0)
    def _():
        vx_scr[:, 0:A_V_DIM] = v_ref[...]
        vx_scr[:, A_V_DIM:2 * A_V_DIM] = jnp.ones((seq, A_V_DIM), BF16)

    def scores(c, slot):
        r0 = c * tc if isinstance(c, int) else pl.multiple_of(c * tc, tc)
        s_scr[slot] = lax.dot_general(q2_scr[...], k_ref[pl.ds(r0, tc), :], (((1,), (1,)), ((), ())),
                                      preferred_element_type=F32)

    def softmax_pv(c, slot, first=False):
        r0 = c * tc if isinstance(c, int) else pl.multiple_of(c * tc, tc)
        slabs = [s_scr[slot, :, j * LANES:(j + 1) * LANES] for j in range(tc // LANES)]
        mx = slabs[0]
        for sl in slabs[1:]:
            mx = jnp.maximum(mx, sl)
        m_cur = jnp.max(mx, axis=1, keepdims=True)
        if first:
            m_new = jnp.broadcast_to(m_cur, m_scr.shape)
        else:
            m_prev = m_scr[...]
            m_new = jnp.maximum(m_prev, m_cur)
            alpha = jnp.exp2(m_prev - m_new)
        p = jnp.concatenate([jnp.exp2(sl - m_new).astype(BF16) for sl in slabs], axis=1)
        pv = jnp.dot(p, vx_scr[pl.ds(r0, tc), :], preferred_element_type=F32)
        if first:
            acc_scr[...] = pv
        else:
            acc_scr[:, 0:A_V_DIM] = alpha * acc_scr[:, 0:A_V_DIM] + pv[:, 0:A_V_DIM]
            acc_scr[:, A_V_DIM:2 * A_V_DIM] = alpha * acc_scr[:, A_V_DIM:2 * A_V_DIM] + pv[:, A_V_DIM:2 * A_V_DIM]
        m_scr[...] = m_new

    n_chunks = seq // tc

    def pairs(p0, first):
        for u in range(unroll):
            c = 2 * (p0 + u)
            nxt = min(c + 2, n_chunks - 1) if isinstance(c, int) else jnp.minimum(c + 2, n_chunks - 1)
            scores(c + 1, 1)
            softmax_pv(c, 0, first=first and u == 0)
            scores(nxt, 0)
            softmax_pv(c + 1, 1)

    def body(it, carry):
        pairs(unroll * it, False)
        return carry

    scores(0, 0)
    pairs(0, True)
    lax.fori_loop(1, n_chunks // (2 * unroll), body, 0)

    o = acc_scr[:, 0:A_V_DIM] / acc_scr[:, A_V_DIM:2 * A_V_DIM]
    lv = lam_ref[...]
    lam = (jnp.exp(jnp.sum(lv[0:1] * lv[1:2], axis=1, keepdims=True))
           - jnp.exp(jnp.sum(lv[2:3] * lv[3:4], axis=1, keepdims=True)) + F32(lam_init))
    od = o[0:tq] - lam * o[tq:2 * tq]
    y = _rms(od, g_ref[...], A_SUBLN_EPS) * F32(1.0 - lam_init)
    o_ref[...] = y.astype(o_ref.dtype)


def _diff_attn(qkv, lam_vecs, subln_g, lam_init, *, tq=1024, tc=512, unroll=2):
    b, s, _ = qkv.shape
    tq = min(tq, s)
    assert s % tq == 0 and s % (2 * unroll * tc) == 0
    blk = lambda rows, off: pl.BlockSpec((None, rows, LANES), off)
    return pl.pallas_call(
        functools.partial(_diff_attn_kernel, tq=tq, tc=tc, unroll=unroll, lam_init=lam_init),
        grid=(b, A_HEADS, s // tq),
        in_specs=[blk(tq, lambda bi, h, i: (bi, i, h)),
                  blk(s, lambda bi, h, i: (bi, 0, A_HEADS + h)),
                  blk(s, lambda bi, h, i: (bi, 0, 2 * A_HEADS + h)),
                  pl.BlockSpec((4, A_HEAD_DIM), lambda bi, h, i: (0, 0)),
                  pl.BlockSpec((1, A_V_DIM), lambda bi, h, i: (0, 0))],
        out_specs=blk(tq, lambda bi, h, i: (bi, i, h)),
        out_shape=jax.ShapeDtypeStruct((b, s, A_HEADS * A_V_DIM), BF16),
        scratch_shapes=[pltpu.VMEM((2 * tq, LANES), BF16),
                        pltpu.VMEM((s, 2 * A_V_DIM), BF16),
                        pltpu.VMEM((2 * tq, LANES), F32),
                        pltpu.VMEM((2 * tq, 2 * A_V_DIM), F32),
                        pltpu.VMEM((2, 2 * tq, tc), F32)],
        compiler_params=_cparams(3),
        name="diff_attn",
    )(qkv, qkv, qkv, lam_vecs, subln_g.reshape(1, A_V_DIM))


def _dil_attn_kernel(q_ref, k_ref, kp_ref, kn_ref, v_ref, vp_ref, vn_ref, o_ref, lse_ref, kbuf, vbuf,
                     *, tl, length):
    hw = B_HALF_W
    t0 = pl.program_id(2) * tl

    for buf, main, prev, nxt in ((kbuf, k_ref, kp_ref, kn_ref), (vbuf, v_ref, vp_ref, vn_ref)):
        buf[0:hw] = prev[...]
        buf[hw:hw + tl] = main[...]
        buf[hw + tl:2 * hw + tl] = nxt[...]

    a_io = lax.broadcasted_iota(jnp.int32, (B_QBLK, B_KBLK), 0)
    c_io = lax.broadcasted_iota(jnp.int32, (B_QBLK, B_KBLK), 1)
    band = (c_io >= a_io) & (c_io <= a_io + 2 * hw)
    lane = lax.broadcasted_iota(jnp.int32, (B_QBLK, LANES), 1)

    def q_block(jb, carry):
        r0 = pl.multiple_of(jb * B_QBLK, B_QBLK)
        ki = t0 + r0 - hw + c_io
        valid = band & (ki >= 0) & (ki < length)
        valid2 = jnp.concatenate([valid, valid], axis=0)
        lse_tile = jnp.zeros((B_QBLK, LANES), F32)
        for hp in range(B_HEADS // 2):
            cols = slice(hp * LANES, (hp + 1) * LANES)
            q = q_ref[pl.ds(r0, B_QBLK), cols]
            zero = jnp.zeros_like(q)
            q2 = jnp.concatenate([jnp.where(lane < B_HEAD_DIM, q, zero),
                                  jnp.where(lane >= B_HEAD_DIM, q, zero)], axis=0)
            s = lax.dot_general(q2, kbuf[pl.ds(r0, B_KBLK), cols], (((1,), (1,)), ((), ())),
                                preferred_element_type=F32)
            s = jnp.where(valid2, s, -jnp.inf)
            m = jnp.max(s, axis=1, keepdims=True)
            p = jnp.exp(s - m)
            l = jnp.sum(p, axis=1, keepdims=True)
            pv = jnp.dot(p.astype(BF16), vbuf[pl.ds(r0, B_KBLK), cols], preferred_element_type=F32) / l
            o_ref[pl.ds(r0, B_QBLK), cols] = jnp.where(lane < B_HEAD_DIM, pv[0:B_QBLK],
                                                       pv[B_QBLK:2 * B_QBLK]).astype(o_ref.dtype)
            lse = m + jnp.log(l)
            lse_tile = jnp.where(lane == 2 * hp, lse[0:B_QBLK], lse_tile)
            lse_tile = jnp.where(lane == 2 * hp + 1, lse[B_QBLK:2 * B_QBLK], lse_tile)
        lse_ref[pl.ds(r0, B_QBLK), :] = lse_tile
        return carry

    lax.fori_loop(0, tl // B_QBLK, q_block, 0, unroll=4)


def _dil_attn_group(qkv, *, tl=512):
    b, dilation, length, _ = qkv.shape
    tl = min(tl, length)
    assert length % tl == 0 and tl % B_QBLK == 0
    hw = B_HALF_W
    width = B_HEADS * B_HEAD_DIM
    per_tile = tl // hw
    n_hw = length // hw
    main = lambda part: pl.BlockSpec((None, None, tl, width), lambda bi, r, i: (bi, r, i, part))
    before = lambda part: pl.BlockSpec((None, None, hw, width),
                                       lambda bi, r, i: (bi, r, jnp.maximum(i * per_tile - 1, 0), part))
    after = lambda part: pl.BlockSpec((None, None, hw, width),
                                      lambda bi, r, i: (bi, r, jnp.minimum((i + 1) * per_tile, n_hw - 1), part))
    return pl.pallas_call(
        functools.partial(_dil_attn_kernel, tl=tl, length=length),
        grid=(b, dilation, length // tl),
        in_specs=[main(0), main(1), before(1), after(1), main(2), before(2), after(2)],
        out_specs=[pl.BlockSpec((None, None, tl, width), lambda bi, r, i: (bi, r, i, 0)),
                   pl.BlockSpec((None, None, tl, LANES), lambda bi, r, i: (bi, r, i, 0))],
        out_shape=[jax.ShapeDtypeStruct((b, dilation, length, width), BF16),
                   jax.ShapeDtypeStruct((b, dilation, length, LANES), F32)],
        scratch_shapes=[pltpu.VMEM((tl + 2 * hw, width), BF16), pltpu.VMEM((tl + 2 * hw, width), BF16)],
        compiler_params=_cparams(3),
        name=f"dil_attn_d{dilation}",
    )(*([qkv] * 7))


def _split3(x):
    hi = x.astype(BF16)
    r1 = x - hi.astype(F32)
    mid = r1.astype(BF16)
    lo = (r1 - mid.astype(F32)).astype(BF16)
    return hi, mid, lo


def _dil_merge_kernel(*refs, n_groups):
    o_refs = refs[:n_groups]
    l_refs = refs[n_groups:2 * n_groups]
    perm_refs = refs[2 * n_groups:3 * n_groups - 1]
    expand_ref, out_ref = refs[3 * n_groups - 1:]
    t = out_ref.shape[0]
    outs, lses = [], []
    for gi in range(n_groups):
        o = o_refs[gi][...].reshape(t, o_refs[gi].shape[-1])
        lse = l_refs[gi][...].reshape(t, LANES)
        if gi == 0:
            outs.append(o.astype(F32))
            lses.append(lse)
        else:
            perm = perm_refs[gi - 1][...]
            outs.append(jnp.dot(perm, o, preferred_element_type=F32))
            l3 = jnp.dot(perm, jnp.concatenate(_split3(lse), axis=1), preferred_element_type=F32)
            lses.append(l3[:, 0:LANES] + l3[:, LANES:2 * LANES] + l3[:, 2 * LANES:3 * LANES])
    m = functools.reduce(jnp.maximum, lses)
    es = [jnp.exp(l - m) for l in lses]
    den = functools.reduce(lambda a, b: a + b, es)
    acc = jnp.zeros((t, out_ref.shape[1]), F32)
    for gi in range(n_groups):
        w = es[gi] / den
        hi = w.astype(BF16)
        lo = (w - hi.astype(F32)).astype(BF16)
        wide = jnp.dot(jnp.concatenate([hi, lo], axis=1), expand_ref[...], preferred_element_type=F32)
        acc = acc + wide * outs[gi]
    out_ref[...] = acc.astype(out_ref.dtype)


def _dil_merge(outs, lses, *, t=512):
    b, _, s, width = outs[0].shape
    n_groups = len(outs)
    in_specs, args = [], []
    for arrs, lanes in ((outs, width), (lses, LANES)):
        for a in arrs:
            d = a.shape[1]
            in_specs.append(pl.BlockSpec((None, d, t // d, lanes), lambda bi, i: (bi, 0, i, 0)))
            args.append(a)
    for a in outs[1:]:
        in_specs.append(_const_spec((t, t), lambda bi, i: (0, 0)))
        args.append(jnp.asarray(_residue_perm(t, a.shape[1]).T, BF16))
    expand = np.zeros((LANES, width), np.float32)
    expand[np.arange(width) // B_HEAD_DIM, np.arange(width)] = 1.0
    in_specs.append(_const_spec((2 * LANES, width), lambda bi, i: (0, 0)))
    args.append(jnp.asarray(np.concatenate([expand, expand], axis=0), BF16))
    return pl.pallas_call(
        functools.partial(_dil_merge_kernel, n_groups=n_groups),
        grid=(b, s // t),
        in_specs=in_specs,
        out_specs=pl.BlockSpec((None, t, width), lambda bi, i: (bi, i, 0)),
        out_shape=jax.ShapeDtypeStruct((b, s, width), BF16),
        compiler_params=_cparams(2),
        name="dil_merge",
    )(*args)


def _scan_constants(reverse):
    c = C_CHUNK
    t = np.arange(c)[:, None]
    u = np.arange(c)[None, :]
    if not reverse:
        mats = [u <= t, u > t]
    else:
        mats = [u >= t, u < t]
    masks = []
    for hs in C_LEVELS:
        base = (t // (2 * hs)) * (2 * hs)
        mid = base + hs - 1
        same = (u // (2 * hs)) == (t // (2 * hs))
        if not reverse:
            m = np.where(t > mid, (u > mid) & (u <= t), (u > t) & (u <= mid))
            masks.append(same & (t > mid) & (u <= mid))
        else:
            m = np.where(t <= mid, (u >= t) & (u <= mid), (u > mid) & (u < t))
            masks.append(same & (t <= mid) & (u > mid))
        mats.append(m & same)
    stack = np.concatenate([np.asarray(m, np.float32) for m in mats], axis=0)
    return jnp.asarray(stack, BF16), jnp.asarray(np.stack(masks).astype(np.float32))


def _hgrn_scan_kernel(*refs, tt, hp, reverse, final):
    q_ref, v_ref, lf_ref, mstack_ref, masks_ref = refs[:5]
    refs = refs[5:]
    if final:
        ofw_ref, gate_ref, gn_ref = refs[:3]
        refs = refs[3:]
    o_ref, st_scr, g_scr, k_scr = refs
    c = C_CHUNK
    n_chunks = tt // c
    nt_contract = (((1,), (1,)), ((), ()))

    @pl.when(pl.program_id(2) == 0)
    def _():
        st_scr[...] = jnp.zeros(st_scr.shape, F32)

    row8 = lax.broadcasted_iota(jnp.int32, (C_DIAG, C_DIM), 0)
    lane_c = lax.broadcasted_iota(jnp.int32, (C_DIAG, c), 1)

    def head_chunk(r0, hi):
        cols = slice(hi * C_DIM, (hi + 1) * C_DIM)
        lf = lf_ref[pl.ds(r0, c), cols] * F32(LOG2E)
        top = lf.astype(BF16)
        low = (lf - top.astype(F32)).astype(BF16)
        d2 = jnp.dot(mstack_ref[...], jnp.concatenate([top, low], axis=1), preferred_element_type=F32)
        d = d2[:, 0:C_DIM] + d2[:, C_DIM:2 * C_DIM]
        gq = d[0:c]
        gk = d[c:2 * c]

        q = q_ref[pl.ds(r0, c), cols].astype(F32)
        v = v_ref[pl.ds(r0, c), cols]
        kk = 1.0 - jnp.exp2(lf)
        st = st_scr[hi]
        g_scr[hi] = gq
        k_scr[hi] = kk

        o = lax.dot_general((q * jnp.exp2(gq)).astype(BF16), st.astype(BF16), nt_contract,
                            preferred_element_type=F32)

        a = jnp.zeros((c, c), F32)
        for li in range(len(C_LEVELS)):
            e = jnp.exp2(-jnp.abs(d[(2 + li) * c:(3 + li) * c]))
            pq = (q * e).astype(BF16)
            pk = (kk * e).astype(BF16)
            a = a + lax.dot_general(pq, pk, nt_contract, preferred_element_type=F32) * masks_ref[li]

        blocks = []
        for bi in range(c // C_DIAG):
            rows = slice(bi * C_DIAG, (bi + 1) * C_DIAG)
            gb, qb = gq[rows], q[rows]
            ablk = jnp.zeros((C_DIAG, c), F32)
            for si in range(C_DIAG):
                row = bi * C_DIAG + si
                g_row = jnp.broadcast_to(g_scr[hi, row:row + 1, :], (C_DIAG, C_DIM))
                k_row = jnp.broadcast_to(k_scr[hi, row:row + 1, :], (C_DIAG, C_DIM))
                keep = (row8 <= si) if reverse else (row8 >= si)
                dec = jnp.exp2(jnp.where(keep, gb - g_row, -jnp.inf))
                sc = jnp.sum(qb * dec * k_row, axis=1, keepdims=True)
                ablk = jnp.where(lane_c == row, sc, ablk)
            blocks.append(ablk)
        a = a + jnp.concatenate(blocks, axis=0)

        o = o + jnp.dot(a.astype(BF16), v, preferred_element_type=F32)

        kd = (kk * jnp.exp2(gk)).astype(BF16)
        upd = lax.dot_general(v, kd, (((0,), (0,)), ((), ())), preferred_element_type=F32)
        g_tot = gq[0:1] if reverse else gq[c - 1:c]
        st_scr[hi] = st * jnp.exp2(g_tot) + upd

        if final:
            tot = ofw_ref[pl.ds(r0, c), cols] + o
            y = _rms(tot, gn_ref[...], NORM_EPS) * gate_ref[pl.ds(r0, c), cols].astype(F32)
            o_ref[pl.ds(r0, c), cols] = y.astype(o_ref.dtype)
        else:
            o_ref[pl.ds(r0, c), cols] = o

    def chunk(ci, carry):
        cidx = (n_chunks - 1 - ci) if reverse else ci
        r0 = pl.multiple_of(cidx * c, c)
        for hi in range(hp):
            head_chunk(r0, hi)
        return carry

    lax.fori_loop(0, n_chunks, chunk, 0, unroll=4)


def _hgrn_scan(act, logf, reverse, o_fw=None, gnorm_g=None, *, tt=512, hp=8):
    b, s, _ = act.shape
    tt = min(tt, s)
    assert s % tt == 0 and tt % C_CHUNK == 0 and C_HEADS % hp == 0
    nt = s // tt
    final = o_fw is not None
    mstack, masks = _scan_constants(reverse)
    groups = C_HEADS // hp
    tmap = (lambda t: nt - 1 - t) if reverse else (lambda t: t)
    blk = lambda part: pl.BlockSpec((None, tt, hp * C_DIM), lambda bi, hg, t: (bi, tmap(t), part * groups + hg))
    in_specs = [blk(0), blk(1), blk(1 if reverse else 0),
                _const_spec(mstack.shape, lambda bi, hg, t: (0, 0)),
                _const_spec(masks.shape, lambda bi, hg, t: (0, 0, 0))]
    args = [act, act, logf, mstack, masks]
    if final:
        in_specs += [blk(0), blk(2), _const_spec((1, C_DIM), lambda bi, hg, t: (0, 0))]
        args += [o_fw, act, gnorm_g.reshape(1, C_DIM)]
    return pl.pallas_call(
        functools.partial(_hgrn_scan_kernel, tt=tt, hp=hp, reverse=reverse, final=final),
        grid=(b, groups, nt),
        in_specs=in_specs,
        out_specs=blk(0),
        out_shape=jax.ShapeDtypeStruct((b, s, C_HEADS * C_DIM), BF16 if final else F32),
        scratch_shapes=[pltpu.VMEM((hp, C_DIM, C_DIM), F32),
                        pltpu.VMEM((hp, C_CHUNK, C_DIM), F32),
                        pltpu.VMEM((hp, C_CHUNK, C_DIM), F32)],
        compiler_params=_cparams(3),
        name="hgrn_scan_bw" if reverse else "hgrn_scan_fw",
    )(*args)


def _diff_lambda_init(layer):
    return 0.8 - 0.6 * math.exp(-0.3 * layer)


def _trunk(x3, p):
    b, s, d = x3.shape
    x = x3.reshape(b * s, d)
    tables = _rope_tables(s)
    for layer in range(DEPTH):
        kind = layer % N_MIXERS
        j = layer // N_MIXERS
        g = p["norm_g"][layer]
        x3 = x.reshape(b, s, d)
        if kind == 0:
            qkv = _qkv_rope(x3, g[0], p["a_w_qkv"][j], tables, A_HEAD_DIM ** -0.5 * LOG2E)
            o = _diff_attn(qkv.reshape(b, s, -1), p["a_lambda"][j], p["a_subln_g"][j], _diff_lambda_init(layer))
            w_o = p["a_w_o"][j]
        elif kind == 1:
            outs, lses = [], []
            for group, (_, dilation) in enumerate(B_PATTERNS):
                qkv = _qkv_rope(x3, g[0], p["b_w_qkv"][j], tables, B_HEAD_DIM ** -0.5, group, dilation)
                o_g, lse_g = _dil_attn_group(qkv)
                outs.append(o_g)
                lses.append(lse_g)
            o = _dil_merge(outs, lses)
            w_o = p["b_w_o"][j]
        else:
            act, logf = _hgrn_in(x, g[0], p["c_w_in"][j], p["c_lb_logits"], layer)
            act, logf = act.reshape(b, s, -1), logf.reshape(b, s, -1)
            o_fw = _hgrn_scan(act, logf, False)
            o = _hgrn_scan(act, logf, True, o_fw, p["c_gnorm_g"][j])
            w_o = p["c_w_o"][j]
        x = _mm_norm_res(o.reshape(b * s, -1), w_o, g[1], x)
        u = _ffn_in(x, g[2], p["f_w_in"][layer], p["f_conv_w"][layer], p["f_conv_b"][layer], s)
        x = _mm_norm_res(u, p["f_w_out"][layer], g[3], x)
    return x.reshape(b, s, d)


def kernel(x_prompt, x_sample, norm_g, a_w_qkv, a_lambda, a_subln_g, a_w_o, b_w_qkv, b_w_o, c_w_in,
           c_lb_logits, c_gnorm_g, c_w_o, f_w_in, f_conv_w, f_conv_b, f_w_out):
    bf = lambda w: w.astype(BF16)
    p = dict(norm_g=norm_g, a_w_qkv=bf(a_w_qkv), a_lambda=a_lambda, a_subln_g=a_subln_g, a_w_o=bf(a_w_o),
             b_w_qkv=bf(b_w_qkv), b_w_o=bf(b_w_o), c_w_in=bf(c_w_in), c_lb_logits=c_lb_logits,
             c_gnorm_g=c_gnorm_g, c_w_o=bf(c_w_o), f_w_in=bf(f_w_in), f_conv_w=f_conv_w,
             f_conv_b=f_conv_b, f_w_out=bf(f_w_out))
    return _trunk(x_prompt, p), _trunk(x_sample, p)
```

```python
import functools
import math

import numpy as np
import jax
import jax.numpy as jnp
from jax import lax
from jax.experimental import pallas as pl
from jax.experimental.pallas import tpu as pltpu

F32 = jnp.float32
BF16 = jnp.bfloat16

D_MODEL = 1024
DEPTH = 4
N_MIXERS = 3
ROPE_THETA = 500000.0
NORM_EPS = 1e-6

A_HEADS = 8
A_HEAD_DIM = 64
A_V_DIM = 128
A_SUBLN_EPS = 1e-5

B_HEADS = 16
B_HEAD_DIM = 64
B_PATTERNS = ((128, 1), (512, 4), (2048, 16))
B_HALF_W = 64
B_QBLK = 2 * B_HALF_W
B_KBLK = 4 * B_HALF_W

C_HEADS = 8
C_DIM = 128
C_CHUNK = 128
C_LEVELS = (64, 32, 16, 8)
C_DIAG = 8

D_FF = 2816
QKV_GROUP = 3 * D_MODEL
ROPE_ROT = 16
LANES = 128
LOG2E = math.log2(math.e)
BF16_ROWS = 16

V7X_VMEM_BYTES = 64 * 1024 * 1024
V7X_VMEM_LIMIT = V7X_VMEM_BYTES - V7X_VMEM_BYTES // 8


def _cparams(n_axes):
    return pltpu.CompilerParams(dimension_semantics=("arbitrary",) * n_axes,
                                vmem_limit_bytes=V7X_VMEM_LIMIT)


def _const_spec(shape, index_map):
    return pl.BlockSpec(shape, index_map, pipeline_mode=pl.Buffered(1))


def _rms(x, g, eps):
    ms = jnp.mean(x * x, axis=-1, keepdims=True)
    return x * lax.rsqrt(ms + eps) * g


def _softplus_neg_abs(d):
    return jnp.log(1.0 + jnp.exp(-jnp.abs(d)))


def _logaddexp(a, b):
    return jnp.maximum(a, b) + _softplus_neg_abs(a - b)


def _rope_tables(seq):
    half = ROPE_ROT // 2
    inv_freq = ROPE_THETA ** (-jnp.arange(half, dtype=F32) / half)
    ang = jnp.arange(seq, dtype=F32)[:, None] * inv_freq[None, :]
    cos, sin = jnp.cos(ang), jnp.sin(ang)
    rest = B_HEAD_DIM - ROPE_ROT
    c = jnp.concatenate([cos, cos, jnp.ones((seq, rest), F32)], axis=1)
    s1 = jnp.concatenate([-sin, jnp.zeros((seq, rest + half), F32)], axis=1)
    s2 = jnp.concatenate([jnp.zeros((seq, half), F32), sin, jnp.zeros((seq, rest), F32)], axis=1)
    rep = LANES // B_HEAD_DIM
    return tuple(jnp.tile(t, (1, rep)) for t in (c, s1, s2))


def _residue_perm(tm, dilation):
    rows = tm // dilation
    rho = np.arange(tm)
    tok = (rho % rows) * dilation + rho // rows
    perm = np.zeros((tm, tm), np.float32)
    perm[rho, tok] = 1.0
    return perm


def _qkv_rope_kernel(*refs, tn, q_scale, dilation):
    if dilation > 1:
        x_ref, g_ref, w_ref, c_ref, s1_ref, s2_ref, perm_ref, o_ref = refs
    else:
        x_ref, g_ref, w_ref, c_ref, s1_ref, s2_ref, o_ref = refs
    tm = x_ref.shape[0]
    rows = tm // dilation

    h = _rms(x_ref[...], g_ref[...], NORM_EPS).astype(BF16)
    if dilation > 1:
        h = jnp.dot(perm_ref[...], h, preferred_element_type=F32).astype(BF16)

    def put(col, val):
        for rr in range(dilation):
            o_ref[rr, :, col:col + LANES] = val[rr * rows:(rr + 1) * rows].astype(o_ref.dtype)

    c, s1, s2 = c_ref[...], s1_ref[...], s2_ref[...]
    cq, s1q, s2q = c * F32(q_scale), s1 * F32(q_scale), s2 * F32(q_scale)
    for cn in range(QKV_GROUP // tn):
        acc = jnp.dot(h, w_ref[:, cn * tn:(cn + 1) * tn], preferred_element_type=F32)
        for cb in range(tn // LANES):
            col = cn * tn + cb * LANES
            a = acc[:, cb * LANES:(cb + 1) * LANES]
            if col < 2 * D_MODEL:
                tc, t1, t2 = (cq, s1q, s2q) if col < D_MODEL else (c, s1, s2)
                a = a * tc + pltpu.roll(a, LANES - ROPE_ROT // 2, 1) * t1 + pltpu.roll(a, ROPE_ROT // 2, 1) * t2
            put(col, a)


def _qkv_rope(x, g, w, tables, q_scale, group=0, dilation=1, *, tm=512, tn=512):
    b, s, d = x.shape
    assert s % tm == 0 and tm % (dilation * BF16_ROWS) == 0 and QKV_GROUP % tn == 0
    rows = tm // dilation
    if dilation > 1:
        tables = tuple(t.reshape(s // tm, rows, dilation, LANES).transpose(0, 2, 1, 3).reshape(s, LANES)
                       for t in tables)
    tab_spec = pl.BlockSpec((tm, LANES), lambda bi, i: (i, 0))
    in_specs = [pl.BlockSpec((None, tm, d), lambda bi, i: (bi, i, 0)),
                _const_spec((1, d), lambda bi, i: (0, 0)),
                _const_spec((d, QKV_GROUP), lambda bi, i: (0, group)),
                tab_spec, tab_spec, tab_spec]
    args = [x, g.reshape(1, d), w, *tables]
    if dilation > 1:
        in_specs.append(_const_spec((tm, tm), lambda bi, i: (0, 0)))
        args.append(jnp.asarray(_residue_perm(tm, dilation), BF16))
    return pl.pallas_call(
        functools.partial(_qkv_rope_kernel, tn=tn, q_scale=q_scale, dilation=dilation),
        grid=(b, s // tm),
        in_specs=in_specs,
        out_specs=pl.BlockSpec((None, dilation, rows, QKV_GROUP), lambda bi, i: (bi, 0, i, 0)),
        out_shape=jax.ShapeDtypeStruct((b, dilation, s // dilation, QKV_GROUP), BF16),
        compiler_params=_cparams(2),
        name=f"qkv_rope_d{dilation}",
    )(*args)


def _hgrn_in_kernel(x_ref, g_ref, w_ref, lb_ref, act_ref, logf_ref, *, tn, layer):
    h = _rms(x_ref[...], g_ref[...], NORM_EPS).astype(BF16)
    logits = lb_ref[...]
    e = jnp.exp(logits - jnp.max(logits, axis=0, keepdims=True))
    p = e / jnp.sum(e, axis=0, keepdims=True)
    cum = p[0:1]
    for r in range(1, layer + 1):
        cum = cum + p[r:r + 1]
    lb = cum - p[0:1]
    log_lb = jnp.log(lb)
    log_1m_lb = jnp.log1p(-lb)

    for cn in range(5 * D_MODEL // tn):
        col = cn * tn
        part, off = col // D_MODEL, col % D_MODEL
        acc = jnp.dot(h, w_ref[:, col:col + tn], preferred_element_type=F32)
        if part == 0:
            act_ref[:, off:off + tn] = (acc * jax.nn.sigmoid(acc) * F32(C_DIM ** -0.5)).astype(act_ref.dtype)
        elif part in (1, 2):
            log_sig = jnp.minimum(acc, 0.0) - _softplus_neg_abs(acc)
            lo = (part - 1) * D_MODEL + off
            logf_ref[:, lo:lo + tn] = _logaddexp(log_lb[:, off:off + tn], log_1m_lb[:, off:off + tn] + log_sig)
        elif part == 3:
            act_ref[:, D_MODEL + off:D_MODEL + off + tn] = acc.astype(act_ref.dtype)
        else:
            act_ref[:, 2 * D_MODEL + off:2 * D_MODEL + off + tn] = (acc * jax.nn.sigmoid(acc)).astype(act_ref.dtype)


def _hgrn_in(x, g, w, lb_logits, layer, *, tm=512, tn=512):
    m, d = x.shape
    assert m % tm == 0 and D_MODEL % tn == 0
    return pl.pallas_call(
        functools.partial(_hgrn_in_kernel, tn=tn, layer=layer),
        grid=(m // tm,),
        in_specs=[pl.BlockSpec((tm, d), lambda i: (i, 0)),
                  _const_spec((1, d), lambda i: (0, 0)),
                  _const_spec((d, 5 * D_MODEL), lambda i: (0, 0)),
                  _const_spec((DEPTH, D_MODEL), lambda i: (0, 0))],
        out_specs=[pl.BlockSpec((tm, 3 * D_MODEL), lambda i: (i, 0)),
                   pl.BlockSpec((tm, 2 * D_MODEL), lambda i: (i, 0))],
        out_shape=[jax.ShapeDtypeStruct((m, 3 * D_MODEL), BF16),
                   jax.ShapeDtypeStruct((m, 2 * D_MODEL), F32)],
        compiler_params=_cparams(1),
        name="hgrn_in",
    )(x, g.reshape(1, d), w, lb_logits)


def _mm_norm_res_kernel(a_ref, w_ref, g_ref, x_ref, o_ref, *, sub):
    for rb in range(a_ref.shape[0] // sub):
        rows = slice(rb * sub, (rb + 1) * sub)
        h = jnp.dot(a_ref[rows, :], w_ref[...], preferred_element_type=F32)
        o_ref[rows, :] = x_ref[rows, :] + _rms(h, g_ref[...], NORM_EPS)


def _mm_norm_res(a, w, g, x, *, tm=1024, sub=256):
    m, k = a.shape
    d = w.shape[1]
    assert m % tm == 0 and tm % sub == 0
    return pl.pallas_call(
        functools.partial(_mm_norm_res_kernel, sub=sub),
        grid=(m // tm,),
        in_specs=[pl.BlockSpec((tm, k), lambda i: (i, 0)),
                  _const_spec((k, d), lambda i: (0, 0)),
                  _const_spec((1, d), lambda i: (0, 0)),
                  pl.BlockSpec((tm, d), lambda i: (i, 0))],
        out_specs=pl.BlockSpec((tm, d), lambda i: (i, 0)),
        out_shape=jax.ShapeDtypeStruct((m, d), F32),
        compiler_params=_cparams(1),
        name="mm_norm_res",
    )(a, w, g.reshape(1, d), x)


def _ffn_in_kernel(x_ref, xp_ref, xn_ref, g_ref, w_ref, cw_ref, cb_ref, o_ref, h_scr, u_scr, *, tm, tn, seq):
    i = pl.program_id(0)
    n_tiles = pl.num_programs(0) - 1
    halo = BF16_ROWS
    rows = tm + 2 * halo
    n_chunks = D_FF // tn
    slot_new = i % 2
    slot_old = 1 - slot_new

    @pl.when(i == 0)
    def _():
        u_scr[...] = jnp.zeros(u_scr.shape, F32)

    t = jnp.minimum(i, n_tiles - 1)
    g = g_ref[...]
    first = (t * tm) % seq == 0
    last = ((t + 1) * tm) % seq == 0
    hp = _rms(xp_ref[...], g, NORM_EPS)
    hn = _rms(xn_ref[...], g, NORM_EPS)
    h_scr[0:halo] = jnp.where(first, 0.0, hp).astype(BF16)
    h_scr[halo:halo + tm] = _rms(x_ref[...], g, NORM_EPS).astype(BF16)
    h_scr[halo + tm:rows] = jnp.where(last, 0.0, hn).astype(BF16)

    def matmul(cn):
        u_scr[slot_new, cn] = jnp.dot(h_scr[...], w_ref[:, cn * tn:(cn + 1) * tn], preferred_element_type=F32)

    def conv(cn, scale=None):
        cw = cw_ref[:, cn * tn:(cn + 1) * tn]
        cb = cb_ref[:, cn * tn:(cn + 1) * tn]
        if scale is not None:
            cw, cb = cw * scale, cb * scale
        return (cb + u_scr[slot_old, cn, halo - 1:halo - 1 + tm, :] * cw[0:1]
                + u_scr[slot_old, cn, halo:halo + tm, :] * cw[1:2]
                + u_scr[slot_old, cn, halo + 1:halo + 1 + tm, :] * cw[2:3])

    k0 = math.sqrt(2.0 / math.pi)
    for cn in range(n_chunks):
        matmul(2 * cn)
        matmul(2 * cn + 1)
        a = conv(cn)
        half_b = conv(n_chunks + cn, F32(0.5))
        th = jnp.tanh(a * (F32(k0) + F32(k0 * 0.044715) * (a * a)))
        o_ref[:, cn * tn:(cn + 1) * tn] = ((a * half_b) * (1.0 + th)).astype(o_ref.dtype)


def _ffn_in(x, g, w, conv_w, conv_b, seq, *, tm=512, tn=256):
    m, d = x.shape
    assert seq % tm == 0 and m % seq == 0 and D_FF % tn == 0
    halo = BF16_ROWS
    hb = tm // halo
    last_blk = m // halo - 1
    n_tiles = m // tm
    cur = lambda i: jnp.minimum(i, n_tiles - 1)
    return pl.pallas_call(
        functools.partial(_ffn_in_kernel, tm=tm, tn=tn, seq=seq),
        grid=(n_tiles + 1,),
        in_specs=[pl.BlockSpec((tm, d), lambda i: (cur(i), 0)),
                  pl.BlockSpec((halo, d), lambda i: (jnp.maximum(cur(i) * hb - 1, 0), 0)),
                  pl.BlockSpec((halo, d), lambda i: (jnp.minimum((cur(i) + 1) * hb, last_blk), 0)),
                  _const_spec((1, d), lambda i: (0, 0)),
                  _const_spec((d, 2 * D_FF), lambda i: (0, 0)),
                  _const_spec((3, 2 * D_FF), lambda i: (0, 0)),
                  _const_spec((1, 2 * D_FF), lambda i: (0, 0))],
        out_specs=pl.BlockSpec((tm, D_FF), lambda i: (jnp.maximum(i - 1, 0), 0)),
        out_shape=jax.ShapeDtypeStruct((m, D_FF), BF16),
        scratch_shapes=[pltpu.VMEM((tm + 2 * halo, d), BF16),
                        pltpu.VMEM((2, 2 * D_FF // tn, tm + 2 * halo, tn), F32)],
        compiler_params=_cparams(1),
        name="ffn_in",
    )(x, x, x, g.reshape(1, d), w, conv_w, conv_b.reshape(1, -1))


def _diff_attn_kernel(q_ref, k_ref, v_ref, lam_ref, g_ref, o_ref, q2_scr, vx_scr, m_scr, acc_scr, s_scr,
                      *, tq, tc, unroll, lam_init):
    seq = k_ref.shape[0]
    q = q_ref[...]
    lane = lax.broadcasted_iota(jnp.int32, q.shape, 1)
    zero = jnp.zeros_like(q)
    q2_scr[0:tq] = jnp.where(lane < A_HEAD_DIM, q, zero)
    q2_scr[tq:2 * tq] = jnp.where(lane >= A_HEAD_DIM, q, zero)

    @pl.when(pl.program_id(2) == 0)
    def _():
        vx_scr[:, 0:A_V_DIM] = v_ref[...]
        vx_scr[:, A_V_DIM:2 * A_V_DIM] = jnp.ones((seq, A_V_DIM), BF16)

    def scores(c, slot):
        r0 = c * tc if isinstance(c, int) else pl.multiple_of(c * tc, tc)
        s_scr[slot] = lax.dot_general(q2_scr[...], k_ref[pl.ds(r0, tc), :], (((1,), (1,)), ((), ())),
                                      preferred_element_type=F32)

    def softmax_pv(c, slot, first=False):
        r0 = c * tc if isinstance(c, int) else pl.multiple_of(c * tc, tc)
        slabs = [s_scr[slot, :, j * LANES:(j + 1) * LANES] for j in range(tc // LANES)]
        mx = slabs[0]
        for sl in slabs[1:]:
            mx = jnp.maximum(mx, sl)
        m_cur = jnp.max(mx, axis=1, keepdims=True)
        if first:
            m_new = jnp.broadcast_to(m_cur, m_scr.shape)
        else:
            m_prev = m_scr[...]
            m_new = jnp.maximum(m_prev, m_cur)
            alpha = jnp.exp2(m_prev - m_new)
        p = jnp.concatenate([jnp.exp2(sl - m_new).astype(BF16) for sl in slabs], axis=1)
        pv = jnp.dot(p, vx_scr[pl.ds(r0, tc), :], preferred_element_type=F32)
        if first:
            acc_scr[...] = pv
        else:
            acc_scr[:, 0:A_V_DIM] = alpha * acc_scr[:, 0:A_V_DIM] + pv[:, 0:A_V_DIM]
            acc_scr[:, A_V_DIM:2 * A_V_DIM] = alpha * acc_scr[:, A_V_DIM:2 * A_V_DIM] + pv[:, A_V_DIM:2 * A_V_DIM]
        m_scr[...] = m_new

    n_chunks = seq // tc

    def pairs(p0, first):
        for u in range(unroll):
            c = 2 * (p0 + u)
            nxt = min(c + 2, n_chunks - 1) if isinstance(c, int) else jnp.minimum(c + 2, n_chunks - 1)
            scores(c + 1, 1)
            softmax_pv(c, 0, first=first and u == 0)
            scores(nxt, 0)
            softmax_pv(c + 1, 1)

    def body(it, carry):
        pairs(unroll * it, False)
        return carry

    scores(0, 0)
    pairs(0, True)
    lax.fori_loop(1, n_chunks // (2 * unroll), body, 0)

    o = acc_scr[:, 0:A_V_DIM] / acc_scr[:, A_V_DIM:2 * A_V_DIM]
    lv = lam_ref[...]
    lam = (jnp.exp(jnp.sum(lv[0:1] * lv[1:2], axis=1, keepdims=True))
           - jnp.exp(jnp.sum(lv[2:3] * lv[3:4], axis=1, keepdims=True)) + F32(lam_init))
    od = o[0:tq] - lam * o[tq:2 * tq]
    y = _rms(od, g_ref[...], A_SUBLN_EPS) * F32(1.0 - lam_init)
    o_ref[...] = y.astype(o_ref.dtype)


def _diff_attn(qkv, lam_vecs, subln_g, lam_init, *, tq=1024, tc=512, unroll=2):
    b, s, _ = qkv.shape
    tq = min(tq, s)
    assert s % tq == 0 and s % (2 * unroll * tc) == 0
    blk = lambda rows, off: pl.BlockSpec((None, rows, LANES), off)
    return pl.pallas_call(
        functools.partial(_diff_attn_kernel, tq=tq, tc=tc, unroll=unroll, lam_init=lam_init),
        grid=(b, A_HEADS, s // tq),
        in_specs=[blk(tq, lambda bi, h, i: (bi, i, h)),
                  blk(s, lambda bi, h, i: (bi, 0, A_HEADS + h)),
                  blk(s, lambda bi, h, i: (bi, 0, 2 * A_HEADS + h)),
                  pl.BlockSpec((4, A_HEAD_DIM), lambda bi, h, i: (0, 0)),
                  pl.BlockSpec((1, A_V_DIM), lambda bi, h, i: (0, 0))],
        out_specs=blk(tq, lambda bi, h, i: (bi, i, h)),
        out_shape=jax.ShapeDtypeStruct((b, s, A_HEADS * A_V_DIM), BF16),
        scratch_shapes=[pltpu.VMEM((2 * tq, LANES), BF16),
                        pltpu.VMEM((s, 2 * A_V_DIM), BF16),
                        pltpu.VMEM((2 * tq, LANES), F32),
                        pltpu.VMEM((2 * tq, 2 * A_V_DIM), F32),
                        pltpu.VMEM((2, 2 * tq, tc), F32)],
        compiler_params=_cparams(3),
        name="diff_attn",
    )(qkv, qkv, qkv, lam_vecs, subln_g.reshape(1, A_V_DIM))


def _dil_attn_kernel(q_ref, k_ref, kp_ref, kn_ref, v_ref, vp_ref, vn_ref, o_ref, lse_ref, kbuf, vbuf,
                     *, tl, length):
    hw = B_HALF_W
    t0 = pl.program_id(2) * tl

    for buf, main, prev, nxt in ((kbuf, k_ref, kp_ref, kn_ref), (vbuf, v_ref, vp_ref, vn_ref)):
        buf[0:hw] = prev[...]
        buf[hw:hw + tl] = main[...]
        buf[hw + tl:2 * hw + tl] = nxt[...]

    a_io = lax.broadcasted_iota(jnp.int32, (B_QBLK, B_KBLK), 0)
    c_io = lax.broadcasted_iota(jnp.int32, (B_QBLK, B_KBLK), 1)
    band = (c_io >= a_io) & (c_io <= a_io + 2 * hw)
    lane = lax.broadcasted_iota(jnp.int32, (B_QBLK, LANES), 1)

    def q_block(jb, carry):
        r0 = pl.multiple_of(jb * B_QBLK, B_QBLK)
        ki = t0 + r0 - hw + c_io
        valid = band & (ki >= 0) & (ki < length)
        valid2 = jnp.concatenate([valid, valid], axis=0)
        lse_tile = jnp.zeros((B_QBLK, LANES), F32)
        for hp in range(B_HEADS // 2):
            cols = slice(hp * LANES, (hp + 1) * LANES)
            q = q_ref[pl.ds(r0, B_QBLK), cols]
            zero = jnp.zeros_like(q)
            q2 = jnp.concatenate([jnp.where(lane < B_HEAD_DIM, q, zero),
                                  jnp.where(lane >= B_HEAD_DIM, q, zero)], axis=0)
            s = lax.dot_general(q2, kbuf[pl.ds(r0, B_KBLK), cols], (((1,), (1,)), ((), ())),
                                preferred_element_type=F32)
            s = jnp.where(valid2, s, -jnp.inf)
            m = jnp.max(s, axis=1, keepdims=True)
            p = jnp.exp(s - m)
            l = jnp.sum(p, axis=1, keepdims=True)
            pv = jnp.dot(p.astype(BF16), vbuf[pl.ds(r0, B_KBLK), cols], preferred_element_type=F32) / l
            o_ref[pl.ds(r0, B_QBLK), cols] = jnp.where(lane < B_HEAD_DIM, pv[0:B_QBLK],
                                                       pv[B_QBLK:2 * B_QBLK]).astype(o_ref.dtype)
            lse = m + jnp.log(l)
            lse_tile = jnp.where(lane == 2 * hp, lse[0:B_QBLK], lse_tile)
            lse_tile = jnp.where(lane == 2 * hp + 1, lse[B_QBLK:2 * B_QBLK], lse_tile)
        lse_ref[pl.ds(r0, B_QBLK), :] = lse_tile
        return carry

    lax.fori_loop(0, tl // B_QBLK, q_block, 0, unroll=4)


def _dil_attn_group(qkv, *, tl=512):
    b, dilation, length, _ = qkv.shape
    tl = min(tl, length)
    assert length % tl == 0 and tl % B_QBLK == 0
    hw = B_HALF_W
    width = B_HEADS * B_HEAD_DIM
    per_tile = tl // hw
    n_hw = length // hw
    main = lambda part: pl.BlockSpec((None, None, tl, width), lambda bi, r, i: (bi, r, i, part))
    before = lambda part: pl.BlockSpec((None, None, hw, width),
                                       lambda bi, r, i: (bi, r, jnp.maximum(i * per_tile - 1, 0), part))
    after = lambda part: pl.BlockSpec((None, None, hw, width),
                                      lambda bi, r, i: (bi, r, jnp.minimum((i + 1) * per_tile, n_hw - 1), part))
    return pl.pallas_call(
        functools.partial(_dil_attn_kernel, tl=tl, length=length),
        grid=(b, dilation, length // tl),
        in_specs=[main(0), main(1), before(1), after(1), main(2), before(2), after(2)],
        out_specs=[pl.BlockSpec((None, None, tl, width), lambda bi, r, i: (bi, r, i, 0)),
                   pl.BlockSpec((None, None, tl, LANES), lambda bi, r, i: (bi, r, i, 0))],
        out_shape=[jax.ShapeDtypeStruct((b, dilation, length, width), BF16),
                   jax.ShapeDtypeStruct((b, dilation, length, LANES), F32)],
        scratch_shapes=[pltpu.VMEM((tl + 2 * hw, width), BF16), pltpu.VMEM((tl + 2 * hw, width), BF16)],
        compiler_params=_cparams(3),
        name=f"dil_attn_d{dilation}",
    )(*([qkv] * 7))


def _split3(x):
    hi = x.astype(BF16)
    r1 = x - hi.astype(F32)
    mid = r1.astype(BF16)
    lo = (r1 - mid.astype(F32)).astype(BF16)
    return hi, mid, lo


def _dil_merge_kernel(*refs, n_groups):
    o_refs = refs[:n_groups]
    l_refs = refs[n_groups:2 * n_groups]
    perm_refs = refs[2 * n_groups:3 * n_groups - 1]
    expand_ref, w_ref, g_ref, x_ref, out_ref = refs[3 * n_groups - 1:]
    t = out_ref.shape[0]
    outs, lses = [], []
    for gi in range(n_groups):
        o = o_refs[gi][...].reshape(t, o_refs[gi].shape[-1])
        lse = l_refs[gi][...].reshape(t, LANES)
        if gi == 0:
            outs.append(o.astype(F32))
            lses.append(lse)
        else:
            perm = perm_refs[gi - 1][...]
            outs.append(jnp.dot(perm, o, preferred_element_type=F32))
            l3 = jnp.dot(perm, jnp.concatenate(_split3(lse), axis=1), preferred_element_type=F32)
            lses.append(l3[:, 0:LANES] + l3[:, LANES:2 * LANES] + l3[:, 2 * LANES:3 * LANES])
    m = functools.reduce(jnp.maximum, lses)
    es = [jnp.exp(l - m) for l in lses]
    den = functools.reduce(lambda a, b: a + b, es)
    acc = jnp.zeros((t, out_ref.shape[1]), F32)
    for gi in range(n_groups):
        w = es[gi] / den
        hi = w.astype(BF16)
        lo = (w - hi.astype(F32)).astype(BF16)
        wide = jnp.dot(jnp.concatenate([hi, lo], axis=1), expand_ref[...], preferred_element_type=F32)
        acc = acc + wide * outs[gi]
    h = jnp.dot(acc.astype(BF16), w_ref[...], preferred_element_type=F32)
    out_ref[...] = x_ref[...] + _rms(h, g_ref[...], NORM_EPS)


def _dil_merge(outs, lses, w_o, g, x, *, t=512):
    b, _, s, width = outs[0].shape
    d_out = w_o.shape[1]
    n_groups = len(outs)
    in_specs, args = [], []
    for arrs, lanes in ((outs, width), (lses, LANES)):
        for a in arrs:
            d = a.shape[1]
            in_specs.append(pl.BlockSpec((None, d, t // d, lanes), lambda bi, i: (bi, 0, i, 0)))
            args.append(a)
    for a in outs[1:]:
        in_specs.append(_const_spec((t, t), lambda bi, i: (0, 0)))
        args.append(jnp.asarray(_residue_perm(t, a.shape[1]).T, BF16))
    expand = np.zeros((LANES, width), np.float32)
    expand[np.arange(width) // B_HEAD_DIM, np.arange(width)] = 1.0
    in_specs.append(_const_spec((2 * LANES, width), lambda bi, i: (0, 0)))
    args.append(jnp.asarray(np.concatenate([expand, expand], axis=0), BF16))
    in_specs += [_const_spec((width, d_out), lambda bi, i: (0, 0)),
                 _const_spec((1, d_out), lambda bi, i: (0, 0)),
                 pl.BlockSpec((None, t, d_out), lambda bi, i: (bi, i, 0))]
    args += [w_o, g.reshape(1, d_out), x]
    return pl.pallas_call(
        functools.partial(_dil_merge_kernel, n_groups=n_groups),
        grid=(b, s // t),
        in_specs=in_specs,
        out_specs=pl.BlockSpec((None, t, d_out), lambda bi, i: (bi, i, 0)),
        out_shape=jax.ShapeDtypeStruct((b, s, d_out), F32),
        compiler_params=_cparams(2),
        name="dil_merge",
    )(*args)


def _scan_constants(reverse):
    c = C_CHUNK
    t = np.arange(c)[:, None]
    u = np.arange(c)[None, :]
    if not reverse:
        mats = [u <= t, u > t]
    else:
        mats = [u >= t, u < t]
    masks = []
    for hs in C_LEVELS:
        base = (t // (2 * hs)) * (2 * hs)
        mid = base + hs - 1
        same = (u // (2 * hs)) == (t // (2 * hs))
        if not reverse:
            m = np.where(t > mid, (u > mid) & (u <= t), (u > t) & (u <= mid))
            masks.append(same & (t > mid) & (u <= mid))
        else:
            m = np.where(t <= mid, (u >= t) & (u <= mid), (u > mid) & (u < t))
            masks.append(same & (t <= mid) & (u > mid))
        mats.append(m & same)
    stack = np.concatenate([np.asarray(m, np.float32) for m in mats], axis=0)
    return jnp.asarray(stack, BF16), jnp.asarray(np.stack(masks).astype(np.float32))


def _hgrn_scan_kernel(*refs, tt, hp, reverse, final):
    q_ref, v_ref, lf_ref, mstack_ref, masks_ref = refs[:5]
    refs = refs[5:]
    if final:
        ofw_ref, gate_ref, gn_ref = refs[:3]
        refs = refs[3:]
    o_ref, st_scr, g_scr, k_scr = refs
    c = C_CHUNK
    n_chunks = tt // c
    nt_contract = (((1,), (1,)), ((), ()))

    @pl.when(pl.program_id(2) == 0)
    def _():
        st_scr[...] = jnp.zeros(st_scr.shape, F32)

    row8 = lax.broadcasted_iota(jnp.int32, (C_DIAG, C_DIM), 0)
    lane_c = lax.broadcasted_iota(jnp.int32, (C_DIAG, c), 1)

    def head_chunk(r0, hi):
        cols = slice(hi * C_DIM, (hi + 1) * C_DIM)
        lf = lf_ref[pl.ds(r0, c), cols] * F32(LOG2E)
        top = lf.astype(BF16)
        low = (lf - top.astype(F32)).astype(BF16)
        d2 = jnp.dot(mstack_ref[...], jnp.concatenate([top, low], axis=1), preferred_element_type=F32)
        d = d2[:, 0:C_DIM] + d2[:, C_DIM:2 * C_DIM]
        gq = d[0:c]
        gk = d[c:2 * c]

        q = q_ref[pl.ds(r0, c), cols].astype(F32)
        v = v_ref[pl.ds(r0, c), cols]
        kk = 1.0 - jnp.exp2(lf)
        st = st_scr[hi]
        g_scr[hi] = gq
        k_scr[hi] = kk

        o = lax.dot_general((q * jnp.exp2(gq)).astype(BF16), st.astype(BF16), nt_contract,
                            preferred_element_type=F32)

        a = jnp.zeros((c, c), F32)
        for li in range(len(C_LEVELS)):
            e = jnp.exp2(-jnp.abs(d[(2 + li) * c:(3 + li) * c]))
            pq = (q * e).astype(BF16)
            pk = (kk * e).astype(BF16)
            a = a + lax.dot_general(pq, pk, nt_contract, preferred_element_type=F32) * masks_ref[li]

        blocks = []
        for bi in range(c // C_DIAG):
            rows = slice(bi * C_DIAG, (bi + 1) * C_DIAG)
            gb, qb = gq[rows], q[rows]
            ablk = jnp.zeros((C_DIAG, c), F32)
            for si in range(C_DIAG):
                row = bi * C_DIAG + si
                g_row = jnp.broadcast_to(g_scr[hi, row:row + 1, :], (C_DIAG, C_DIM))
                k_row = jnp.broadcast_to(k_scr[hi, row:row + 1, :], (C_DIAG, C_DIM))
                keep = (row8 <= si) if reverse else (row8 >= si)
                dec = jnp.exp2(jnp.where(keep, gb - g_row, -jnp.inf))
                sc = jnp.sum(qb * dec * k_row, axis=1, keepdims=True)
                ablk = jnp.where(lane_c == row, sc, ablk)
            blocks.append(ablk)
        a = a + jnp.concatenate(blocks, axis=0)

        o = o + jnp.dot(a.astype(BF16), v, preferred_element_type=F32)

        kd = (kk * jnp.exp2(gk)).astype(BF16)
        upd = lax.dot_general(v, kd, (((0,), (0,)), ((), ())), preferred_element_type=F32)
        g_tot = gq[0:1] if reverse else gq[c - 1:c]
        st_scr[hi] = st * jnp.exp2(g_tot) + upd

        if final:
            tot = ofw_ref[pl.ds(r0, c), cols] + o
            y = _rms(tot, gn_ref[...], NORM_EPS) * gate_ref[pl.ds(r0, c), cols].astype(F32)
            o_ref[pl.ds(r0, c), cols] = y.astype(o_ref.dtype)
        else:
            o_ref[pl.ds(r0, c), cols] = o

    def chunk(ci, carry):
        cidx = (n_chunks - 1 - ci) if reverse else ci
        r0 = pl.multiple_of(cidx * c, c)
        for hi in range(hp):
            head_chunk(r0, hi)
        return carry

    lax.fori_loop(0, n_chunks, chunk, 0, unroll=4)


def _hgrn_scan(act, logf, reverse, o_fw=None, gnorm_g=None, *, tt=512, hp=8):
    b, s, _ = act.shape
    tt = min(tt, s)
    assert s % tt == 0 and tt % C_CHUNK == 0 and C_HEADS % hp == 0
    nt = s // tt
    final = o_fw is not None
    mstack, masks = _scan_constants(reverse)
    groups = C_HEADS // hp
    tmap = (lambda t: nt - 1 - t) if reverse else (lambda t: t)
    blk = lambda part: pl.BlockSpec((None, tt, hp * C_DIM), lambda bi, hg, t: (bi, tmap(t), part * groups + hg))
    in_specs = [blk(0), blk(1), blk(1 if reverse else 0),
                _const_spec(mstack.shape, lambda bi, hg, t: (0, 0)),
                _const_spec(masks.shape, lambda bi, hg, t: (0, 0, 0))]
    args = [act, act, logf, mstack, masks]
    if final:
        in_specs += [blk(0), blk(2), _const_spec((1, C_DIM), lambda bi, hg, t: (0, 0))]
        args += [o_fw, act, gnorm_g.reshape(1, C_DIM)]
    return pl.pallas_call(
        functools.partial(_hgrn_scan_kernel, tt=tt, hp=hp, reverse=reverse, final=final),
        grid=(b, groups, nt),
        in_specs=in_specs,
        out_specs=blk(0),
        out_shape=jax.ShapeDtypeStruct((b, s, C_HEADS * C_DIM), BF16 if final else F32),
        scratch_shapes=[pltpu.VMEM((hp, C_DIM, C_DIM), F32),
                        pltpu.VMEM((hp, C_CHUNK, C_DIM), F32),
                        pltpu.VMEM((hp, C_CHUNK, C_DIM), F32)],
        compiler_params=_cparams(3),
        name="hgrn_scan_bw" if reverse else "hgrn_scan_fw",
    )(*args)


def _diff_lambda_init(layer):
    return 0.8 - 0.6 * math.exp(-0.3 * layer)


def _trunk(x3, p):
    b, s, d = x3.shape
    x = x3.reshape(b * s, d)
    tables = _rope_tables(s)
    for layer in range(DEPTH):
        kind = layer % N_MIXERS
        j = layer // N_MIXERS
        g = p["norm_g"][layer]
        x3 = x.reshape(b, s, d)
        if kind == 0:
            qkv = _qkv_rope(x3, g[0], p["a_w_qkv"][j], tables, A_HEAD_DIM ** -0.5 * LOG2E)
            o = _diff_attn(qkv.reshape(b, s, -1), p["a_lambda"][j], p["a_subln_g"][j], _diff_lambda_init(layer))
            w_o = p["a_w_o"][j]
        elif kind == 1:
            outs, lses = [], []
            for group, (_, dilation) in enumerate(B_PATTERNS):
                qkv = _qkv_rope(x3, g[0], p["b_w_qkv"][j], tables, B_HEAD_DIM ** -0.5, group, dilation)
                o_g, lse_g = _dil_attn_group(qkv)
                outs.append(o_g)
                lses.append(lse_g)
            x = _dil_merge(outs, lses, p["b_w_o"][j], g[1], x3).reshape(b * s, d)
            o = None
        else:
            act, logf = _hgrn_in(x, g[0], p["c_w_in"][j], p["c_lb_logits"], layer)
            act, logf = act.reshape(b, s, -1), logf.reshape(b, s, -1)
            o_fw = _hgrn_scan(act, logf, False)
            o = _hgrn_scan(act, logf, True, o_fw, p["c_gnorm_g"][j])
            w_o = p["c_w_o"][j]
        if o is not None:
            x = _mm_norm_res(o.reshape(b * s, -1), w_o, g[1], x)
        u = _ffn_in(x, g[2], p["f_w_in"][layer], p["f_conv_w"][layer], p["f_conv_b"][layer], s)
        x = _mm_norm_res(u, p["f_w_out"][layer], g[3], x)
    return x.reshape(b, s, d)


def kernel(x_prompt, x_sample, norm_g, a_w_qkv, a_lambda, a_subln_g, a_w_o, b_w_qkv, b_w_o, c_w_in,
           c_lb_logits, c_gnorm_g, c_w_o, f_w_in, f_conv_w, f_conv_b, f_w_out):
    bf = lambda w: w.astype(BF16)
    p = dict(norm_g=norm_g, a_w_qkv=bf(a_w_qkv), a_lambda=a_lambda, a_subln_g=a_subln_g, a_w_o=bf(a_w_o),
             b_w_qkv=bf(b_w_qkv), b_w_o=bf(b_w_o), c_w_in=bf(c_w_in), c_lb_logits=c_lb_logits,
             c_gnorm_g=c_gnorm_g, c_w_o=bf(c_w_o), f_w_in=bf(f_w_in), f_conv_w=f_conv_w,
             f_conv_b=f_conv_b, f_w_out=bf(f_w_out))
    return _trunk(x_prompt, p), _trunk(x_sample, p)
```
